```python
import math
import jax, jax.numpy as jnp
from jax import lax
import numpy as np

D_MODEL = 1024
BATCH = 16
SEQ = 2048
DEPTH = 4

GRID_W = 64
CTX_LEN = 256
HEAD_DIM = 64
NA_HEADS = D_MODEL // (2 * HEAD_DIM)
NA_WIN_H = 8
NA_WIN_W = 16
NA_COL_BLOCK = 16
NA_BAND_W = NA_COL_BLOCK + NA_WIN_W
DIFF_HEADS = D_MODEL // (4 * HEAD_DIM)
DIFF_DV = 2 * HEAD_DIM
Q_BLOCK = 128
A_W = NA_HEADS * HEAD_DIM
B_QK_W = DIFF_HEADS * 2 * HEAD_DIM
B_V_W = DIFF_HEADS * DIFF_DV
ATT_IN_W = 3 * A_W + 2 * B_QK_W + B_V_W
ATT_OUT_W = A_W + B_V_W
REC_HEADS = 8
REC_DK = D_MODEL // REC_HEADS
REC_DV = D_MODEL // REC_HEADS
REC_CHUNK = 32
REC_IN_W = 5 * D_MODEL
N_EXPERTS = 16
EXPERT_FF = 2 * D_MODEL
EC_CAPACITY_FACTOR = 2
ROPE_THETA = 10000.0
NORM_EPS = 1e-6
NEG_INF = -1e30
N_EVEN = (DEPTH + 1) // 2
N_ODD = DEPTH // 2

kernel_name = 'hybrid_na_diffattn_hgrn2_ec_dit'


def rmsnorm(x, g):
    xf = x.astype(jnp.float32)
    y = xf * lax.rsqrt(jnp.mean(xf * xf, axis=-1, keepdims=True) + NORM_EPS)
    return (y * g.astype(jnp.float32)).astype(x.dtype)


def modulation(cvec, w_mod, b_mod):
    m = jnp.einsum('...d,de->...e', jax.nn.silu(cvec), w_mod) + b_mod
    return jnp.split(m, 6, axis=-1)


def _split_heads(a, n_heads):
    b, t, _ = a.shape
    return jnp.transpose(a.reshape(b, t, n_heads, -1), (0, 2, 1, 3))


def _split_pair_heads(a):
    b, t, _ = a.shape
    return jnp.transpose(a.reshape(b, t, DIFF_HEADS, 2, HEAD_DIM), (0, 2, 3, 1, 4))


def _merge_heads(a):
    b, h, t, d = a.shape
    return jnp.transpose(a, (0, 2, 1, 3)).reshape(b, t, h * d)


def axial_rope(n, dim):
    t = jnp.arange(n, dtype=jnp.int32)
    rows = (t // GRID_W).astype(jnp.float32)
    cols = (t % GRID_W).astype(jnp.float32)
    n_freq = dim // 4
    inv = ROPE_THETA ** (-jnp.arange(n_freq, dtype=jnp.float32) / n_freq)
    ang = jnp.concatenate([rows[:, None] * inv, cols[:, None] * inv], axis=-1)
    return jnp.cos(ang), jnp.sin(ang)


def apply_rope(a, cos, sin):
    a1, a2 = jnp.split(a, 2, axis=-1)
    cos = cos.astype(a.dtype)
    sin = sin.astype(a.dtype)
    return jnp.concatenate([a1 * cos - a2 * sin, a2 * cos + a1 * sin], axis=-1)


def _softmax_attend(q, k, v, scale):
    s = jnp.einsum('bhqd,bhkd->bhqk', q, k).astype(jnp.float32) * scale
    p = jax.nn.softmax(s, axis=-1).astype(v.dtype)
    return jnp.einsum('bhqk,bhkd->bhqd', p, v)


def _diff_attend(q12, k12, v, lam, scale):
    s = jnp.einsum('bhmqd,bhmkd->bhmqk', q12, k12).astype(jnp.float32) * scale
    p = jax.nn.softmax(s, axis=-1)
    a = (p[:, :, 0] - lam * p[:, :, 1]).astype(v.dtype)
    return jnp.einsum('bhqk,bhkd->bhqd', a, v)


def diff_attention_latent(q12, k12_all, v_all, lam, scale):
    b, h, _, n, d = q12.shape
    nb = n // Q_BLOCK
    qb = jnp.moveaxis(q12.reshape(b, h, 2, nb, Q_BLOCK, d), 3, 0)
    o = lax.map(lambda qi: _diff_attend(qi, k12_all, v_all, lam, scale), qb)
    return jnp.moveaxis(o, 0, 2).reshape(b, h, n, -1)


def neighbourhood_attention(q, k, v, k_ctx, v_ctx, rpb):
    b, h, n, d = q.shape
    rows = n // GRID_W
    kh = min(NA_WIN_H, rows)
    n_cb = GRID_W // NA_COL_BLOCK
    scale = d ** -0.5
    qcol = np.arange(GRID_W).reshape(n_cb, NA_COL_BLOCK)
    cstart = np.clip(qcol - NA_WIN_W // 2, 0, GRID_W - NA_WIN_W)
    bstart = np.clip(np.arange(n_cb) * NA_COL_BLOCK - NA_WIN_W // 2, 0, GRID_W - NA_BAND_W)
    kcol = bstart[:, None] + np.arange(NA_BAND_W)[None, :]
    col_ok = (kcol[:, None, :] >= cstart[:, :, None]) & (kcol[:, None, :] < cstart[:, :, None] + NA_WIN_W)
    mask = jnp.asarray(np.broadcast_to(col_ok[:, :, None, :], (n_cb, NA_COL_BLOCK, kh, NA_BAND_W)).reshape(n_cb, NA_COL_BLOCK, kh * NA_BAND_W))
    dcol = np.clip(kcol[:, None, :] - qcol[:, :, None] + NA_WIN_W - 1, 0, 2 * NA_WIN_W - 2)
    rpb_c = rpb[:, :, dcol]
    kg = k.reshape(b, h, rows, GRID_W, d)
    vg = v.reshape(b, h, rows, GRID_W, d)
    qg = jnp.moveaxis(q.reshape(b, h, rows, n_cb, NA_COL_BLOCK, d), 2, 0)
    n_loc = kh * NA_BAND_W

    def one_row(args):
        q_r, r = args
        rs = jnp.clip(r - kh // 2, 0, rows - kh)
        k_band = lax.dynamic_slice_in_dim(kg, rs, kh, axis=2)[:, :, :, kcol]
        v_band = lax.dynamic_slice_in_dim(vg, rs, kh, axis=2)[:, :, :, kcol]
        k_band = jnp.moveaxis(k_band, 3, 2).reshape(b, h, n_cb, n_loc, d)
        v_band = jnp.moveaxis(v_band, 3, 2).reshape(b, h, n_cb, n_loc, d)
        bias = jnp.take(rpb_c, rs - r + jnp.arange(kh, dtype=jnp.int32) + NA_WIN_H - 1, axis=1)
        bias = jnp.transpose(bias, (0, 2, 3, 1, 4)).reshape(h, n_cb, NA_COL_BLOCK, n_loc)
        s_loc = jnp.einsum('bhjqd,bhjkd->bhjqk', q_r, k_band).astype(jnp.float32) * scale + bias.astype(jnp.float32)
        s_loc = jnp.where(mask, s_loc, NEG_INF)
        s_ctx = jnp.einsum('bhjqd,bhkd->bhjqk', q_r, k_ctx).astype(jnp.float32) * scale
        p = jax.nn.softmax(jnp.concatenate([s_loc, s_ctx], axis=-1), axis=-1).astype(v.dtype)
        return (jnp.einsum('bhjqk,bhjkd->bhjqd', p[..., :n_loc], v_band)
                + jnp.einsum('bhjqk,bhkd->bhjqd', p[..., n_loc:], v_ctx))

    o = lax.map(one_row, (qg, jnp.arange(rows, dtype=jnp.int32)))
    return jnp.moveaxis(o, 0, 2).reshape(b, h, n, d)


def even_layer_mixer(h_lat, h_ctx, w_in, w_out, rpb, lam_vecs, subln_g, layer_idx, with_ctx_out):
    n = h_lat.shape[1]
    scale = HEAD_DIM ** -0.5
    kv_off = A_W + B_QK_W
    kv_splits = [A_W, 2 * A_W, 2 * A_W + B_QK_W]
    p_lat = jnp.einsum('btd,de->bte', h_lat, w_in)
    q_a, q_b, kv_lat = jnp.split(p_lat, [A_W, kv_off], axis=-1)
    if with_ctx_out:
        p_ctx = jnp.einsum('btd,de->bte', h_ctx, w_in)
        q_ac, q_bc, kv_ctx = jnp.split(p_ctx, [A_W, kv_off], axis=-1)
    else:
        kv_ctx = jnp.einsum('btd,de->bte', h_ctx, w_in[:, kv_off:])
    k_a, v_a, k_b, v_b = jnp.split(kv_lat, kv_splits, axis=-1)
    k_ac, v_ac, k_bc, v_bc = jnp.split(kv_ctx, kv_splits, axis=-1)
    k_ac = _split_heads(k_ac, NA_HEADS)
    v_ac = _split_heads(v_ac, NA_HEADS)

    o_a = neighbourhood_attention(_split_heads(q_a, NA_HEADS), _split_heads(k_a, NA_HEADS),
                                  _split_heads(v_a, NA_HEADS), k_ac, v_ac, rpb)

    lam_init = 0.8 - 0.6 * math.exp(-0.3 * layer_idx)
    lv = lam_vecs.astype(jnp.float32)
    lam = jnp.exp(jnp.sum(lv[0] * lv[1])) - jnp.exp(jnp.sum(lv[2] * lv[3])) + lam_init
    cos, sin = axial_rope(n, HEAD_DIM)
    k12_ctx = _split_pair_heads(k_bc)
    v_ctx_b = _split_heads(v_bc, DIFF_HEADS)
    q12 = apply_rope(_split_pair_heads(q_b), cos, sin)
    k12_all = jnp.concatenate([apply_rope(_split_pair_heads(k_b), cos, sin), k12_ctx], axis=3)
    v_all = jnp.concatenate([_split_heads(v_b, DIFF_HEADS), v_ctx_b], axis=2)
    o_b = rmsnorm(diff_attention_latent(q12, k12_all, v_all, lam, scale), subln_g) * (1.0 - lam_init)
    y_lat = jnp.einsum('bte,ed->btd', jnp.concatenate([_merge_heads(o_a), _merge_heads(o_b)], axis=-1), w_out)
    if not with_ctx_out:
        return y_lat, None
    o_ac = _softmax_attend(_split_heads(q_ac, NA_HEADS), k_ac, v_ac, scale)
    o_bc = rmsnorm(_diff_attend(_split_pair_heads(q_bc), k12_ctx, v_ctx_b, lam, scale), subln_g) * (1.0 - lam_init)
    y_ctx = jnp.einsum('bte,ed->btd', jnp.concatenate([_merge_heads(o_ac), _merge_heads(o_bc)], axis=-1), w_out)
    return y_lat, y_ctx


def _hgrn_gates(fx, lb):
    lbb = lb[None, :, None, :]
    logf = jnp.logaddexp(jnp.log(lbb), jnp.log1p(-lbb) + jax.nn.log_sigmoid(fx))
    k = (1.0 - lbb) * jax.nn.sigmoid(-fx)
    return k, logf


def hgrn2_chunk_scan(k, v, logf, s0, q):
    b, h, t, dk = k.shape
    nc = t // REC_CHUNK
    tri = jnp.asarray(np.tril(np.ones((REC_CHUNK, REC_CHUNK), dtype=bool)))

    def chunks(a):
        return jnp.moveaxis(a.reshape(b, h, nc, REC_CHUNK, a.shape[-1]), 2, 0)

    def body(s, xs):
        kc, vc, gc = xs[0], xs[1], xs[2]
        bcum = jnp.cumsum(gc, axis=2)
        blast = bcum[:, :, -1:, :]
        s_new = jnp.exp(blast[:, :, 0])[..., None] * s + jnp.einsum('bhcd,bhce->bhde', kc * jnp.exp(blast - bcum), vc)
        if q is None:
            return s_new, None
        qc = xs[3]
        rel = jnp.where(tri[None, None, :, :, None], bcum[:, :, :, None, :] - bcum[:, :, None, :, :], -jnp.inf)
        scores = jnp.einsum('bhtsd,bhsd->bhts', qc[:, :, :, None, :] * jnp.exp(rel), kc)
        o = jnp.einsum('bhtd,bhde->bhte', qc * jnp.exp(bcum), s) + jnp.einsum('bhts,bhse->bhte', scores, vc)
        return s_new, o

    xs = (chunks(k), chunks(v), chunks(logf)) + ((chunks(q),) if q is not None else ())
    s_fin, o = lax.scan(body, s0, xs)
    if q is not None:
        o = jnp.moveaxis(o, 0, 2).reshape(b, h, t, -1)
    return o, s_fin


def _hgrn_out(o, g, gnorm_g, w_out, dtype):
    o = _merge_heads(rmsnorm(o, gnorm_g)).astype(dtype) * jax.nn.silu(g)
    return jnp.einsum('bte,ed->btd', o, w_out)


def odd_layer_mixer(h_lat, h_ctx, w_in, w_out, lb_fwd, lb_bwd, gnorm_g, with_ctx_out):
    f32 = jnp.float32
    heads = lambda a: _split_heads(a, REC_HEADS).astype(f32)
    flip = lambda a: jnp.flip(a, axis=2)
    lb_f = lb_fwd.reshape(REC_HEADS, REC_DK)
    lb_b = lb_bwd.reshape(REC_HEADS, REC_DK)
    q, ff, fb, inp, g = jnp.split(jnp.einsum('btd,de->bte', h_lat, w_in), 5, axis=-1)
    if with_ctx_out:
        qc, ffc, fbc, inpc, gc = jnp.split(jnp.einsum('btd,de->bte', h_ctx, w_in), 5, axis=-1)
        qc = jax.nn.silu(heads(qc))
    else:
        ffc, fbc, inpc = jnp.split(jnp.einsum('btd,de->bte', h_ctx, w_in[:, D_MODEL:4 * D_MODEL]), 3, axis=-1)
        qc = None
    vc = heads(inpc)
    kcf, gcf = _hgrn_gates(heads(ffc), lb_f)
    kcb, gcb = _hgrn_gates(heads(fbc), lb_b)
    s0 = jnp.zeros((h_ctx.shape[0], REC_HEADS, REC_DK, REC_DV), f32)
    o_cf, s_cf = hgrn2_chunk_scan(kcf, vc, gcf, s0, qc)
    o_cb, s_cb = hgrn2_chunk_scan(flip(kcb), flip(vc), flip(gcb), s0, None if qc is None else flip(qc))
    ql = jax.nn.silu(heads(q))
    vl = heads(inp)
    klf, glf = _hgrn_gates(heads(ff), lb_f)
    klb, glb = _hgrn_gates(heads(fb), lb_b)
    o_lf, _ = hgrn2_chunk_scan(klf, vl, glf, s_cf, ql)
    o_lb, _ = hgrn2_chunk_scan(flip(klb), flip(vl), flip(glb), s_cb, flip(ql))
    y_lat = _hgrn_out(o_lf + flip(o_lb), g, gnorm_g, w_out, h_lat.dtype)
    if not with_ctx_out:
        return y_lat, None
    y_ctx = _hgrn_out(o_cf + flip(o_cb), gc, gnorm_g, w_out, h_ctx.dtype)
    return y_lat, y_ctx


def expert_choice_ffn(h, w_router, w_gate, w_up, w_down):
    b, t, d = h.shape
    cap = EC_CAPACITY_FACTOR * t // N_EXPERTS
    aff = jax.nn.softmax(jnp.einsum('btd,de->bte', h, w_router).astype(jnp.float32), axis=-1)
    gate, idx = lax.top_k(jnp.swapaxes(aff, 1, 2), cap)
    x_sel = jax.vmap(lambda hb, ib: hb[ib])(h, idx)
    hid = jax.nn.silu(jnp.einsum('becd,edf->becf', x_sel, w_gate)) * jnp.einsum('becd,edf->becf', x_sel, w_up)
    y_sel = jnp.einsum('becf,efd->becd', hid, w_down) * gate[..., None].astype(h.dtype)
    return jax.vmap(lambda ib, yb: jnp.zeros((t, d), h.dtype).at[ib.reshape(-1)].add(yb.reshape(-1, d)))(idx, y_sel)


def setup_inputs(seed: int = 0) -> dict:
    key = jax.random.key(seed)
    ks = jax.random.split(key, 21)
    f32 = jnp.float32

    def nrm(k, shape, std):
        return jax.random.normal(k, shape, f32) * std

    def gain(k, shape):
        return 1.0 + 0.02 * jax.random.normal(k, shape, f32)

    return {
        'x': nrm(ks[0], (BATCH, SEQ, D_MODEL), 1.0),
        'c': nrm(ks[1], (BATCH, D_MODEL), 1.0),
        'ctx': nrm(ks[2], (BATCH, CTX_LEN, D_MODEL), 1.0),
        'c_ctx': nrm(ks[3], (D_MODEL,), 1.0),
        'w_mod': nrm(ks[4], (DEPTH, D_MODEL, 6 * D_MODEL), 0.5 * D_MODEL ** -0.5),
        'b_mod': nrm(ks[5], (DEPTH, 6 * D_MODEL), 0.02),
        'norm_g': gain(ks[6], (DEPTH, 2, D_MODEL)),
        'att_w_in': nrm(ks[7], (N_EVEN, D_MODEL, ATT_IN_W), D_MODEL ** -0.5),
        'att_w_out': nrm(ks[8], (N_EVEN, ATT_OUT_W, D_MODEL), ATT_OUT_W ** -0.5),
        'na_rpb': nrm(ks[9], (N_EVEN, NA_HEADS, 2 * NA_WIN_H - 1, 2 * NA_WIN_W - 1), 0.1),
        'diff_lambda': nrm(ks[10], (N_EVEN, 4, HEAD_DIM), 0.1),
        'diff_subln_g': gain(ks[11], (N_EVEN, DIFF_DV)),
        'rec_w_in': nrm(ks[12], (N_ODD, D_MODEL, REC_IN_W), D_MODEL ** -0.5),
        'rec_w_out': nrm(ks[13], (N_ODD, D_MODEL, D_MODEL), D_MODEL ** -0.5),
        'rec_lb_logits': 1.0 + 0.1 * jax.random.normal(ks[14], (2, DEPTH, D_MODEL), f32),
        'rec_gnorm_g': gain(ks[15], (N_ODD, REC_DV)),
        'moe_router': nrm(ks[16], (DEPTH, D_MODEL, N_EXPERTS), D_MODEL ** -0.5),
        'moe_w_gate': nrm(ks[17], (DEPTH, N_EXPERTS, D_MODEL, EXPERT_FF), D_MODEL ** -0.5),
        'moe_w_up': nrm(ks[18], (DEPTH, N_EXPERTS, D_MODEL, EXPERT_FF), D_MODEL ** -0.5),
        'moe_w_down': nrm(ks[19], (DEPTH, N_EXPERTS, EXPERT_FF, D_MODEL), EXPERT_FF ** -0.5),
        'final_g': gain(ks[20], (D_MODEL,)),
    }


def reference(x, c, ctx, c_ctx, w_mod, b_mod, norm_g, att_w_in, att_w_out, na_rpb, diff_lambda,
              diff_subln_g, rec_w_in, rec_w_out, rec_lb_logits, rec_gnorm_g, moe_router, moe_w_gate,
              moe_w_up, moe_w_down, final_g):
    sm = jax.nn.softmax(rec_lb_logits.astype(jnp.float32), axis=1)
    lower_bounds = jnp.cumsum(sm, axis=1) - sm[:, :1]
    for l in range(DEPTH):
        last = l == DEPTH - 1
        sh1, sc1, g1, sh2, sc2, g2 = [m[:, None, :] for m in modulation(c, w_mod[l], b_mod[l])]
        csh1, csc1, cg1, csh2, csc2, cg2 = modulation(c_ctx, w_mod[l], b_mod[l])
        h_lat = rmsnorm(x, norm_g[l, 0]) * (1.0 + sc1) + sh1
        h_ctx = rmsnorm(ctx, norm_g[l, 0]) * (1.0 + csc1) + csh1
        if l % 2 == 0:
            e = l // 2
            y_lat, y_ctx = even_layer_mixer(h_lat, h_ctx, att_w_in[e], att_w_out[e], na_rpb[e],
                                            diff_lambda[e], diff_subln_g[e], l, not last)
        else:
            o = l // 2
            y_lat, y_ctx = odd_layer_mixer(h_lat, h_ctx, rec_w_in[o], rec_w_out[o], lower_bounds[0, l],
                                           lower_bounds[1, l], rec_gnorm_g[o], not last)
        x = x + g1 * y_lat
        x = x + g2 * expert_choice_ffn(rmsnorm(x, norm_g[l, 1]) * (1.0 + sc2) + sh2,
                                       moe_router[l], moe_w_gate[l], moe_w_up[l], moe_w_down[l])
        if not last:
            ctx = ctx + cg1 * y_ctx
            ctx = ctx + cg2 * expert_choice_ffn(rmsnorm(ctx, norm_g[l, 1]) * (1.0 + csc2) + csh2,
                                                moe_router[l], moe_w_gate[l], moe_w_up[l], moe_w_down[l])
    return rmsnorm(x, final_g)
```

```python
import functools
import math

import numpy as np
import jax
import jax.numpy as jnp
from jax import lax
from jax.experimental import pallas as pl
from jax.experimental.pallas import tpu as pltpu

F32 = jnp.float32
BF16 = jnp.bfloat16

D_MODEL = 1024
DEPTH = 4
GRID_W = 64
HEAD_DIM = 64
NA_HEADS = 8
NA_WIN_H = 8
NA_WIN_W = 16
DIFF_HEADS = 4
A_W = NA_HEADS * HEAD_DIM
ATT_IN_W = 3072
REC_HEADS = 8
REC_D = 128
N_EXPERTS = 16
EXPERT_FF = 2 * D_MODEL
EC_CAPACITY_FACTOR = 2
ROPE_THETA = 10000.0
NORM_EPS = 1e-6
NEG_INF = -1e30

LANES = 128
V7X_VMEM_BYTES = 64 * 1024 * 1024
MIB = 1024 * 1024

NA_QROWS = 4
NA_KROWS = NA_QROWS + NA_WIN_H - 1
REC_CHUNK = 64
REC_SUB = 16
REC_NSUB = REC_CHUNK // REC_SUB
REC_EXP_CLAMP = 60.0
PREFIX_BLOCK = 256

NT_DIMS = (((1,), (1,)), ((), ()))
TN_DIMS = (((0,), (0,)), ((), ()))


def _cparams(semantics, vmem_mib):
    assert vmem_mib * MIB < V7X_VMEM_BYTES
    return pltpu.CompilerParams(dimension_semantics=semantics, vmem_limit_bytes=vmem_mib * MIB)


def _sigmoid(x):
    return 1.0 / (1.0 + jnp.exp(-x))


def _rms(x, eps=NORM_EPS):
    return x * lax.rsqrt(jnp.mean(x * x, axis=-1, keepdims=True) + eps)


def _mod_kernel(c_ref, w_ref, b_ref, o_ref):
    c = c_ref[...]
    s = (c * _sigmoid(c)).astype(BF16)
    o_ref[0] = jnp.dot(s, w_ref[0].astype(BF16), preferred_element_type=F32) + b_ref[0]


def modulation_all(c_all, w_mod, b_mod):
    depth, d, n = w_mod.shape
    r = c_all.shape[0]
    tn = 1536
    return pl.pallas_call(
        _mod_kernel,
        grid=(depth, n // tn),
        in_specs=[
            pl.BlockSpec((r, d), lambda l, j: (0, 0)),
            pl.BlockSpec((1, d, tn), lambda l, j: (l, 0, j)),
            pl.BlockSpec((1, 1, tn), lambda l, j: (l, 0, j)),
        ],
        out_specs=pl.BlockSpec((1, r, tn), lambda l, j: (l, 0, j)),
        out_shape=jax.ShapeDtypeStruct((depth, r, n), F32),
        compiler_params=_cparams(("parallel", "parallel"), 32),
        name="modulation",
    )(c_all, w_mod, b_mod.reshape(depth, 1, n))


def _norm_mod(x, g, sc, sh):
    return _rms(x) * g * (1.0 + sc) + sh


def _proj_kernel(x_ref, g_ref, sc_ref, sh_ref, w_ref, o_ref, h_scr, *, tn):
    h_scr[...] = _norm_mod(x_ref[0], g_ref[...], sc_ref[0], sh_ref[0]).astype(BF16)
    n = w_ref.shape[1]
    for j in range(n // tn):
        o_ref[0, :, j * tn:(j + 1) * tn] = jnp.dot(
            h_scr[...], w_ref[:, j * tn:(j + 1) * tn], preferred_element_type=F32).astype(o_ref.dtype)


def norm_mod_proj(x, g, sc, sh, w_bf16, out_dtype):
    b, t, d = x.shape
    n = w_bf16.shape[1]
    tm = min(t, 256)
    return pl.pallas_call(
        functools.partial(_proj_kernel, tn=512),
        grid=(b, t // tm),
        in_specs=[
            pl.BlockSpec((1, tm, d), lambda i, j: (i, j, 0)),
            pl.BlockSpec((1, d), lambda i, j: (0, 0)),
            pl.BlockSpec((1, 1, d), lambda i, j: (i, 0, 0)),
            pl.BlockSpec((1, 1, d), lambda i, j: (i, 0, 0)),
            pl.BlockSpec((d, n), lambda i, j: (0, 0)),
        ],
        out_specs=pl.BlockSpec((1, tm, n), lambda i, j: (i, j, 0)),
        out_shape=jax.ShapeDtypeStruct((b, t, n), out_dtype),
        scratch_shapes=[pltpu.VMEM((tm, d), BF16)],
        compiler_params=_cparams(("parallel", "parallel"), 48),
        name="norm_mod_proj",
    )(x, g.reshape(1, d), sc, sh, w_bf16)


def _outproj_kernel(*refs, n_in):
    o_refs, w_refs = refs[:n_in], refs[n_in:2 * n_in]
    x_ref, gate_ref, out_ref = refs[2 * n_in:]
    acc = jnp.dot(o_refs[0][0], w_refs[0][...], preferred_element_type=F32)
    for o_ref, w_ref in zip(o_refs[1:], w_refs[1:]):
        acc = acc + jnp.dot(o_ref[0], w_ref[...], preferred_element_type=F32)
    out_ref[0] = x_ref[0] + gate_ref[0] * acc


def outproj_residual(os_, ws, x, gate):
    b, t, d = x.shape
    tm = min(t, 512)
    n_in = len(os_)
    in_specs = [pl.BlockSpec((1, tm, o.shape[2]), lambda i, j: (i, j, 0)) for o in os_]
    in_specs += [pl.BlockSpec(w.shape, lambda i, j: (0, 0)) for w in ws]
    in_specs += [pl.BlockSpec((1, tm, d), lambda i, j: (i, j, 0)),
                 pl.BlockSpec((1, 1, d), lambda i, j: (i, 0, 0))]
    return pl.pallas_call(
        functools.partial(_outproj_kernel, n_in=n_in),
        grid=(b, t // tm),
        in_specs=in_specs,
        out_specs=pl.BlockSpec((1, tm, d), lambda i, j: (i, j, 0)),
        out_shape=jax.ShapeDtypeStruct((b, t, d), F32),
        compiler_params=_cparams(("parallel", "parallel"), 32),
        name="outproj_residual",
    )(*os_, *ws, x, gate)


def _na_block_tables(rows):
    kh = min(NA_WIN_H, rows)
    assert kh == NA_WIN_H and rows % NA_QROWS == 0 and rows >= NA_KROWS
    n_qb = rows // NA_QROWS
    starts = [min(max(qb * NA_QROWS - kh // 2, 0), rows - NA_KROWS) for qb in range(n_qb)]
    tbl_of, reps = [], []
    for qb in range(n_qb):
        r0 = qb * NA_QROWS
        interior = (r0 - kh // 2 >= 0) and (r0 + NA_QROWS - 1 - kh // 2 <= rows - kh) and starts[qb] == r0 - kh // 2
        key = "interior" if interior else qb
        if key not in reps:
            reps.append(key)
        tbl_of.append(reps.index(key))
    rep_qb = [next(qb for qb in range(n_qb) if tbl_of[qb] == i) for i in range(len(reps))]
    nq, nk = NA_QROWS * GRID_W, NA_KROWS * GRID_W
    dr = np.zeros((len(reps), nq, nk), np.int32)
    dc = np.zeros((len(reps), nq, nk), np.int32)
    ok = np.zeros((len(reps), nq, nk), bool)
    qi, c = np.divmod(np.arange(nq), GRID_W)
    kk, c2 = np.divmod(np.arange(nk), GRID_W)
    for i, qb in enumerate(rep_qb):
        r = qb * NA_QROWS + qi
        r2 = starts[qb] + kk
        rs = np.clip(r - kh // 2, 0, rows - kh)
        cs = np.clip(c - NA_WIN_W // 2, 0, GRID_W - NA_WIN_W)
        row_ok = (r2[None, :] >= rs[:, None]) & (r2[None, :] < rs[:, None] + kh)
        col_ok = (c2[None, :] >= cs[:, None]) & (c2[None, :] < cs[:, None] + NA_WIN_W)
        ok[i] = row_ok & col_ok
        dr[i] = np.clip(r2[None, :] - r[:, None] + NA_WIN_H - 1, 0, 2 * NA_WIN_H - 2)
        dc[i] = np.clip(c2[None, :] - c[:, None] + NA_WIN_W - 1, 0, 2 * NA_WIN_W - 2)
    return starts, tbl_of, dr, dc, ok


def na_bias_tables(rpb, rows, ctx_len):
    _, _, dr, dc, ok = _na_block_tables(rows)
    loc = jnp.where(jnp.asarray(ok)[None], rpb[:, jnp.asarray(dr), jnp.asarray(dc)].astype(F32), NEG_INF)
    return jnp.concatenate([loc, jnp.zeros(loc.shape[:3] + (ctx_len,), F32)], axis=-1)


def _softmax_pv(s, v):
    m = jnp.max(s, axis=-1, keepdims=True)
    p = jnp.exp(s - m)
    l = jnp.sum(p, axis=-1, keepdims=True)
    return jnp.dot(p.astype(BF16), v, preferred_element_type=F32) * (1.0 / l)


def _na_kernel(q_ref, k_ref, v_ref, qc_ref, kc_ref, vc_ref, bias_ref, o_ref, oc_ref, *, starts, tbl_of):
    scale = HEAD_DIM ** -0.5
    lane = lax.broadcasted_iota(jnp.int32, (1, LANES), 1)
    first = lane < HEAD_DIM
    head_masks = [jnp.where(first, 1.0, 0.0).astype(BF16), jnp.where(first, 0.0, 1.0).astype(BF16)]
    nq, nk = NA_QROWS * GRID_W, NA_KROWS * GRID_W
    kc, vc = kc_ref[0], vc_ref[0]
    for qb, (start, tbl) in enumerate(zip(starts, tbl_of)):
        q = q_ref[0, qb * nq:(qb + 1) * nq, :]
        k_all = jnp.concatenate([k_ref[0, start * GRID_W:start * GRID_W + nk, :], kc], axis=0)
        v_all = jnp.concatenate([v_ref[0, start * GRID_W:start * GRID_W + nk, :], vc], axis=0)
        outs = []
        for hh in range(2):
            s = lax.dot_general(q * head_masks[hh], k_all, NT_DIMS, preferred_element_type=F32)
            outs.append(_softmax_pv(s * scale + bias_ref[hh, tbl], v_all))
        o_ref[0, qb * nq:(qb + 1) * nq, :] = jnp.where(first, outs[0], outs[1]).astype(o_ref.dtype)
    qc = qc_ref[0]
    outs = []
    for hh in range(2):
        s = lax.dot_general(qc * head_masks[hh], kc, NT_DIMS, preferred_element_type=F32)
        outs.append(_softmax_pv(s * scale, vc))
    oc_ref[0] = jnp.where(first, outs[0], outs[1]).astype(oc_ref.dtype)


def na_attention(p_lat, p_ctx, bias):
    b, t, _ = p_lat.shape
    l = p_ctx.shape[1]
    rows = t // GRID_W
    starts, tbl_of, _, _, _ = _na_block_tables(rows)
    n_pairs = NA_HEADS // 2
    kb, vb = 2 * A_W // LANES, 3 * A_W // LANES
    lat = lambda off: pl.BlockSpec((1, t, LANES), lambda j, i: (i, 0, off + j))
    cx = lambda off: pl.BlockSpec((1, l, LANES), lambda j, i: (i, 0, off + j))
    n_tbl, nq, nkc = bias.shape[1:]
    return pl.pallas_call(
        functools.partial(_na_kernel, starts=tuple(starts), tbl_of=tuple(tbl_of)),
        grid=(n_pairs, b),
        in_specs=[lat(0), lat(kb), lat(vb), cx(0), cx(kb), cx(vb),
                  pl.BlockSpec((2, n_tbl, nq, nkc), lambda j, i: (j, 0, 0, 0))],
        out_specs=[pl.BlockSpec((1, t, LANES), lambda j, i: (i, 0, j)),
                   pl.BlockSpec((1, l, LANES), lambda j, i: (i, 0, j))],
        out_shape=[jax.ShapeDtypeStruct((b, t, A_W), BF16), jax.ShapeDtypeStruct((b, l, A_W), BF16)],
        compiler_params=_cparams(("parallel", "parallel"), 48),
        name="na_attention",
    )(p_lat, p_lat, p_lat, p_ctx, p_ctx, p_ctx, bias)


def rope_tables(n):
    t = np.arange(n)
    rows, cols = (t // GRID_W).astype(np.float64), (t % GRID_W).astype(np.float64)
    n_freq = HEAD_DIM // 4
    inv = ROPE_THETA ** (-np.arange(n_freq, dtype=np.float64) / n_freq)
    ang = np.concatenate([rows[:, None] * inv, cols[:, None] * inv], axis=-1)
    cos = np.tile(np.cos(ang), (1, 4))
    sin = np.tile(np.concatenate([-np.sin(ang), np.sin(ang)], axis=-1), (1, 2))
    return jnp.asarray(cos, F32), jnp.asarray(sin, F32)


def _rope(a, cos, sin_signed):
    lane = lax.broadcasted_iota(jnp.int32, (1, LANES), 1)
    low = (lane % HEAD_DIM) < HEAD_DIM // 2
    partner = jnp.where(low, pltpu.roll(a, LANES - HEAD_DIM // 2, 1), pltpu.roll(a, HEAD_DIM // 2, 1))
    return a * cos + partner * sin_signed


def _diff_kernel(q_ref, k_ref, v_ref, qc_ref, kc_ref, vc_ref, cos_ref, sin_ref, lam_ref, g_ref,
                 o_ref, oc_ref, q_scr, k_scr, v_scr, *, lam_init, tq):
    scale = HEAD_DIM ** -0.5
    t, l = q_ref.shape[1], kc_ref.shape[1]
    lane = lax.broadcasted_iota(jnp.int32, (1, LANES), 1)
    first = lane < HEAD_DIM
    m1 = jnp.where(first, 1.0, 0.0).astype(BF16)
    m2 = jnp.where(first, 0.0, 1.0).astype(BF16)
    lv = lam_ref[...]
    lam = (jnp.exp(jnp.sum(lv[0:1] * lv[1:2], axis=-1, keepdims=True))
           - jnp.exp(jnp.sum(lv[2:3] * lv[3:4], axis=-1, keepdims=True)) + lam_init)
    cos, sin = cos_ref[...], sin_ref[...]
    q_scr[...] = _rope(q_ref[0].astype(F32), cos, sin).astype(BF16)
    k_scr[0:t, :] = _rope(k_ref[0].astype(F32), cos, sin).astype(BF16)
    k_scr[t:t + l, :] = kc_ref[0]
    v_scr[0:t, :] = v_ref[0]
    v_scr[t:t + l, :] = vc_ref[0]
    gain = g_ref[...] * (1.0 - lam_init)

    def attend(q, k, v):
        def probs(mask):
            s = lax.dot_general(q * mask, k, NT_DIMS, preferred_element_type=F32) * scale
            p = jnp.exp(s - jnp.max(s, axis=-1, keepdims=True))
            return p * (1.0 / jnp.sum(p, axis=-1, keepdims=True))
        a = probs(m1) - lam * probs(m2)
        o = jnp.dot(a.astype(BF16), v, preferred_element_type=F32)
        return _rms(o) * gain

    def body(i, carry):
        r = pl.multiple_of(i * tq, tq)
        o_ref[0, pl.ds(r, tq), :] = attend(q_scr[pl.ds(r, tq), :], k_scr[...], v_scr[...]).astype(o_ref.dtype)
        return carry

    lax.fori_loop(0, t // tq, body, 0)
    oc_ref[0] = attend(qc_ref[0], kc_ref[0], vc_ref[0]).astype(oc_ref.dtype)


def diff_attention(p_lat, p_ctx, cos, sin, lam_vecs, subln_g, layer_idx):
    b, t, _ = p_lat.shape
    l = p_ctx.shape[1]
    lam_init = 0.8 - 0.6 * math.exp(-0.3 * layer_idx)
    qb, kb, vb = A_W // LANES, 4 * A_W // LANES, 5 * A_W // LANES
    lat = lambda off: pl.BlockSpec((1, t, LANES), lambda i, j: (i, 0, off + j))
    cx = lambda off: pl.BlockSpec((1, l, LANES), lambda i, j: (i, 0, off + j))
    const = lambda shape: pl.BlockSpec(shape, lambda i, j: (0, 0))
    return pl.pallas_call(
        functools.partial(_diff_kernel, lam_init=lam_init, tq=256),
        grid=(b, DIFF_HEADS),
        in_specs=[lat(qb), lat(kb), lat(vb), cx(qb), cx(kb), cx(vb),
                  const((t, LANES)), const((t, LANES)), const(lam_vecs.shape), const((1, LANES))],
        out_specs=[pl.BlockSpec((1, t, LANES), lambda i, j: (i, 0, j)),
                   pl.BlockSpec((1, l, LANES), lambda i, j: (i, 0, j))],
        out_shape=[jax.ShapeDtypeStruct((b, t, A_W), BF16), jax.ShapeDtypeStruct((b, l, A_W), BF16)],
        scratch_shapes=[pltpu.VMEM((t, LANES), BF16), pltpu.VMEM((t + l, LANES), BF16),
                        pltpu.VMEM((t + l, LANES), BF16)],
        compiler_params=_cparams(("parallel", "parallel"), 48),
        name="diff_attention",
    )(p_lat, p_lat, p_lat, p_ctx, p_ctx, p_ctx, cos, sin, lam_vecs, subln_g.reshape(1, LANES))


def _split3(x):
    a = x.astype(BF16)
    r = x - a.astype(F32)
    b = r.astype(BF16)
    c = (r - b.astype(F32)).astype(BF16)
    return a, b, c


def _rec_masks(forward):
    t = lax.broadcasted_iota(jnp.int32, (REC_CHUNK, REC_CHUNK), 0)
    s = lax.broadcasted_iota(jnp.int32, (REC_CHUNK, REC_CHUNK), 1)
    if forward:
        incl, excl = s <= t, (s // REC_SUB) < (t // REC_SUB)
    else:
        incl, excl = s >= t, (s // REC_SUB) > (t // REC_SUB)
    both = jnp.concatenate([jnp.where(incl, 1.0, 0.0), jnp.where(excl, 1.0, 0.0)], axis=0).astype(BF16)
    return both, incl


def _rec_chunk(q, fx, v, st, lb, forward):
    c = REC_CHUNK
    masks, incl = _rec_masks(forward)
    qs = q * _sigmoid(q)
    e = jnp.exp(-jnp.abs(fx))
    r = 1.0 / (1.0 + e)
    pos = fx >= 0
    f = lb + (1.0 - lb) * jnp.where(pos, r, e * r)
    kk = (1.0 - lb) * jnp.where(pos, e * r, r)
    g = jnp.log(f)
    g3 = jnp.concatenate(_split3(g), axis=1)
    cs = jnp.dot(masks, g3, preferred_element_type=F32)
    cs = cs[:, 0:LANES] + cs[:, LANES:2 * LANES] + cs[:, 2 * LANES:3 * LANES]
    bcum, bsub = cs[0:c], cs[c:2 * c]
    qhat = qs * jnp.exp(bcum - bsub)
    row_sub = lax.broadcasted_iota(jnp.int32, (c, 1), 0) // REC_SUB
    qcat = jnp.concatenate([jnp.where(row_sub == p, qhat, 0.0) for p in range(REC_NSUB)], axis=1)
    kcat = jnp.concatenate(
        [kk * jnp.exp(jnp.minimum(bsub[p * REC_SUB:p * REC_SUB + 1] - bcum, REC_EXP_CLAMP))
         for p in range(REC_NSUB)], axis=1)
    sc = lax.dot_general(qcat.astype(BF16), kcat.astype(BF16), NT_DIMS, preferred_element_type=F32)
    sc = jnp.where(incl, sc, 0.0)
    vb = v.astype(BF16)
    o = (jnp.dot(sc.astype(BF16), vb, preferred_element_type=F32)
         + lax.dot_general((qs * jnp.exp(bcum)).astype(BF16), st.astype(BF16), NT_DIMS,
                           preferred_element_type=F32))
    blast = bcum[c - 1:c] if forward else bcum[0:1]
    kdec = kk * jnp.exp(blast - bcum)
    st_new = st * jnp.exp(blast) + lax.dot_general(vb, kdec.astype(BF16), TN_DIMS, preferred_element_type=F32)
    return o, st_new


def _rec_kernel(q_ref, ff_ref, fb_ref, v_ref, g_ref, qc_ref, ffc_ref, fbc_ref, vc_ref, gc_ref,
                lb_ref, gn_ref, y_ref, yc_ref, of_scr, ob_scr, *, layer_idx):
    t, l = q_ref.shape[1], qc_ref.shape[1]
    c = REC_CHUNK

    def lower_bound(d):
        lg = lb_ref[d]
        ex = jnp.exp(lg - jnp.max(lg, axis=0, keepdims=True))
        return jnp.sum(ex[1:layer_idx + 1], axis=0, keepdims=True) / jnp.sum(ex, axis=0, keepdims=True)

    lb_f, lb_b = lower_bound(0), lower_bound(1)

    def scan(q_r, ff_r, fb_r, v_r, n, st_f, st_b):
        def body(i, carry):
            sf, sb = carry
            rf = pl.multiple_of(i * c, c)
            rb = pl.multiple_of((n - 1 - i) * c, c)
            o_f, sf = _rec_chunk(q_r[0, pl.ds(rf, c), :], ff_r[0, pl.ds(rf, c), :], v_r[0, pl.ds(rf, c), :],
                                 sf, lb_f, True)
            o_b, sb = _rec_chunk(q_r[0, pl.ds(rb, c), :], fb_r[0, pl.ds(rb, c), :], v_r[0, pl.ds(rb, c), :],
                                 sb, lb_b, False)
            of_scr[pl.ds(rf, c), :] = o_f
            ob_scr[pl.ds(rb, c), :] = o_b
            return sf, sb
        return lax.fori_loop(0, n, body, (st_f, st_b))

    def finish(n_tok, gate_ref, out_ref):
        o = of_scr[0:n_tok, :] + ob_scr[0:n_tok, :]
        gate = gate_ref[0]
        out_ref[0] = ((_rms(o) * gn_ref[...]) * (gate * _sigmoid(gate))).astype(out_ref.dtype)

    zero = jnp.zeros((REC_D, REC_D), F32)
    st_f, st_b = scan(qc_ref, ffc_ref, fbc_ref, vc_ref, l // c, zero, zero)
    finish(l, gc_ref, yc_ref)
    scan(q_ref, ff_ref, fb_ref, v_ref, t // c, st_f, st_b)
    finish(t, g_ref, y_ref)


def hgrn2_mixer(p_lat, p_ctx, lb_logits, gnorm_g, layer_idx):
    b, t, _ = p_lat.shape
    l = p_ctx.shape[1]
    lat = lambda k: pl.BlockSpec((1, t, REC_D), lambda i, j: (i, 0, k * REC_HEADS + j))
    cx = lambda k: pl.BlockSpec((1, l, REC_D), lambda i, j: (i, 0, k * REC_HEADS + j))
    return pl.pallas_call(
        functools.partial(_rec_kernel, layer_idx=layer_idx),
        grid=(b, REC_HEADS),
        in_specs=[lat(k) for k in range(5)] + [cx(k) for k in range(5)] + [
            pl.BlockSpec((2, DEPTH, REC_D), lambda i, j: (0, 0, j)),
            pl.BlockSpec((1, REC_D), lambda i, j: (0, 0))],
        out_specs=[pl.BlockSpec((1, t, REC_D), lambda i, j: (i, 0, j)),
                   pl.BlockSpec((1, l, REC_D), lambda i, j: (i, 0, j))],
        out_shape=[jax.ShapeDtypeStruct((b, t, D_MODEL), BF16), jax.ShapeDtypeStruct((b, l, D_MODEL), BF16)],
        scratch_shapes=[pltpu.VMEM((t, REC_D), F32), pltpu.VMEM((t, REC_D), F32)],
        compiler_params=_cparams(("parallel", "parallel"), 32),
        name="hgrn2_mixer",
    )(*([p_lat] * 5), *([p_ctx] * 5), lb_logits, gnorm_g.reshape(1, REC_D))


def _prefix_count(flags):
    e, t = flags.shape
    blk = min(PREFIX_BLOCK, t)
    ones = jnp.where(flags, 1.0, 0.0)
    s = lax.broadcasted_iota(jnp.int32, (blk, blk), 0)
    u = lax.broadcasted_iota(jnp.int32, (blk, blk), 1)
    upper = jnp.where(s < u, 1.0, 0.0).astype(BF16)
    carry = jnp.zeros((e, 1), F32)
    outs = []
    for i in range(t // blk):
        xs = ones[:, i * blk:(i + 1) * blk]
        outs.append(jnp.dot(xs.astype(BF16), upper, preferred_element_type=F32) + carry)
        carry = carry + jnp.sum(xs, axis=1, keepdims=True)
    return jnp.concatenate(outs, axis=1)


def _route_kernel(x_ref, g_ref, sc_ref, sh_ref, wr_ref, xsel_ref, slot_ref, gate_ref, hb_scr, slot_scr, *, cap):
    e = pl.program_id(1)
    t = x_ref.shape[1]

    @pl.when(e == 0)
    def _():
        h = _norm_mod(x_ref[0], g_ref[...], sc_ref[0], sh_ref[0])
        hb_scr[...] = h.astype(BF16)
        logits = lax.dot_general(wr_ref[...], h, NT_DIMS, precision=lax.Precision.HIGHEST,
                                 preferred_element_type=F32)
        ex = jnp.exp(logits - jnp.max(logits, axis=0, keepdims=True))
        aff = ex / jnp.sum(ex, axis=0, keepdims=True)
        bits = lax.bitcast_convert_type(aff, jnp.int32)
        capf = jnp.float32(cap)
        thr = jnp.zeros((N_EXPERTS, 1), jnp.int32)
        for bit in range(30, -1, -1):
            cand = thr | jnp.int32(1 << bit)
            cnt = jnp.sum(jnp.where(bits >= cand, 1.0, 0.0), axis=1, keepdims=True)
            thr = jnp.where(cnt >= capf, cand, thr)
        gt = bits > thr
        eq = bits == thr
        need = capf - jnp.sum(jnp.where(gt, 1.0, 0.0), axis=1, keepdims=True)
        sel = gt | (eq & (_prefix_count(eq) < need))
        slot = jnp.where(sel, _prefix_count(sel), -1.0)
        slot_scr[...] = slot
        slot_ref[0] = slot
        gate_ref[0] = aff

    srow = slot_scr[pl.ds(e, 1), :]
    onehot = lax.broadcasted_iota(jnp.int32, (cap, t), 0).astype(F32) == srow
    xsel_ref[0] = jnp.dot(jnp.where(onehot, 1.0, 0.0).astype(BF16), hb_scr[...],
                          preferred_element_type=F32).astype(xsel_ref.dtype)


def route_gather(x, g, sc, sh, w_router):
    b, t, d = x.shape
    cap = EC_CAPACITY_FACTOR * t // N_EXPERTS
    return pl.pallas_call(
        functools.partial(_route_kernel, cap=cap),
        grid=(b, N_EXPERTS),
        in_specs=[
            pl.BlockSpec((1, t, d), lambda i, e: (i, 0, 0)),
            pl.BlockSpec((1, d), lambda i, e: (0, 0)),
            pl.BlockSpec((1, 1, d), lambda i, e: (i, 0, 0)),
            pl.BlockSpec((1, 1, d), lambda i, e: (i, 0, 0)),
            pl.BlockSpec((N_EXPERTS, d), lambda i, e: (0, 0)),
        ],
        out_specs=[pl.BlockSpec((1, cap, d), lambda i, e: (e, i, 0)),
                   pl.BlockSpec((1, N_EXPERTS, t), lambda i, e: (i, 0, 0)),
                   pl.BlockSpec((1, N_EXPERTS, t), lambda i, e: (i, 0, 0))],
        out_shape=[jax.ShapeDtypeStruct((N_EXPERTS, b * cap, d), BF16),
                   jax.ShapeDtypeStruct((b, N_EXPERTS, t), F32),
                   jax.ShapeDtypeStruct((b, N_EXPERTS, t), F32)],
        scratch_shapes=[pltpu.VMEM((t, d), BF16), pltpu.VMEM((N_EXPERTS, t), F32)],
        compiler_params=_cparams(("parallel", "arbitrary"), 48),
        name="route_gather",
    )(x, g.reshape(1, d), sc, sh, w_router.T)


def _ffn_kernel(*refs, n_streams):
    x_refs = refs[:n_streams]
    wg_ref, wu_ref, wd_ref = refs[n_streams:n_streams + 3]
    y_refs = refs[n_streams + 3:2 * n_streams + 3]
    acc_refs = refs[2 * n_streams + 3:]
    f = pl.program_id(2)
    wg, wu, wd = wg_ref[0, 0].astype(BF16), wu_ref[0, 0].astype(BF16), wd_ref[0, 0].astype(BF16)
    for x_ref, y_ref, acc_ref in zip(x_refs, y_refs, acc_refs):
        x = x_ref[0]
        a = jnp.dot(x, wg, preferred_element_type=F32)
        u = jnp.dot(x, wu, preferred_element_type=F32)
        hid = ((a * _sigmoid(a)) * u).astype(BF16)
        part = jnp.dot(hid, wd, preferred_element_type=F32)

        @pl.when(f == 0)
        def _():
            acc_ref[...] = part

        @pl.when(f > 0)
        def _():
            acc_ref[...] += part

        @pl.when(f == pl.num_programs(2) - 1)
        def _():
            y_ref[0] = acc_ref[...].astype(y_ref.dtype)


def expert_ffn(xsels, w_gate, w_up, w_down, layer_idx):
    d, ff = w_gate.shape[2], w_gate.shape[3]
    tf, n_split = 256, 2
    n_streams = len(xsels)
    row_blocks = [x.shape[1] // n_split for x in xsels]
    xspec = lambda rb: pl.BlockSpec((1, rb, d), lambda e, h, f: (e, h, 0))
    return pl.pallas_call(
        functools.partial(_ffn_kernel, n_streams=n_streams),
        grid=(N_EXPERTS, n_split, ff // tf),
        in_specs=[xspec(rb) for rb in row_blocks] + [
            pl.BlockSpec((1, 1, d, tf), lambda e, h, f: (layer_idx, e, 0, f)),
            pl.BlockSpec((1, 1, d, tf), lambda e, h, f: (layer_idx, e, 0, f)),
            pl.BlockSpec((1, 1, tf, d), lambda e, h, f: (layer_idx, e, f, 0))],
        out_specs=[xspec(rb) for rb in row_blocks],
        out_shape=[jax.ShapeDtypeStruct(x.shape, BF16) for x in xsels],
        scratch_shapes=[pltpu.VMEM((rb, d), F32) for rb in row_blocks],
        compiler_params=_cparams(("parallel", "parallel", "arbitrary"), 56),
        name="expert_ffn",
    )(*xsels, w_gate, w_up, w_down)


def _combine_kernel(y_ref, slot_ref, gate_ref, x_ref, g2_ref, fg_ref, out_ref, *, cap, final_norm):
    e = pl.program_id(2)
    tt = x_ref.shape[1]
    srow = slot_ref[0, pl.ds(e, 1), :]
    grow = gate_ref[0, pl.ds(e, 1), :]
    onehot = lax.broadcasted_iota(jnp.int32, (cap, tt), 0).astype(F32) == srow
    pg = jnp.where(onehot, grow, 0.0).astype(BF16)
    contrib = lax.dot_general(pg, y_ref[0], TN_DIMS, preferred_element_type=F32)

    @pl.when(e == 0)
    def _():
        out_ref[0] = contrib

    @pl.when(e > 0)
    def _():
        out_ref[0] += contrib

    @pl.when(e == N_EXPERTS - 1)
    def _():
        xn = x_ref[0] + g2_ref[0] * out_ref[0]
        if final_norm:
            xn = _rms(xn) * fg_ref[...]
        out_ref[0] = xn


def combine_residual(y_sel, slot, gate, x, g2, final_g=None):
    b, t, d = x.shape
    cap = EC_CAPACITY_FACTOR * t // N_EXPERTS
    tt = min(t, 1024)
    final_norm = final_g is not None
    fg = (final_g if final_norm else jnp.ones((d,), F32)).reshape(1, d)
    return pl.pallas_call(
        functools.partial(_combine_kernel, cap=cap, final_norm=final_norm),
        grid=(b, t // tt, N_EXPERTS),
        in_specs=[
            pl.BlockSpec((1, cap, d), lambda i, j, e: (e, i, 0)),
            pl.BlockSpec((1, N_EXPERTS, tt), lambda i, j, e: (i, 0, j)),
            pl.BlockSpec((1, N_EXPERTS, tt), lambda i, j, e: (i, 0, j)),
            pl.BlockSpec((1, tt, d), lambda i, j, e: (i, j, 0)),
            pl.BlockSpec((1, 1, d), lambda i, j, e: (i, 0, 0)),
            pl.BlockSpec((1, d), lambda i, j, e: (0, 0)),
        ],
        out_specs=pl.BlockSpec((1, tt, d), lambda i, j, e: (i, j, 0)),
        out_shape=jax.ShapeDtypeStruct((b, t, d), F32),
        compiler_params=_cparams(("parallel", "parallel", "arbitrary"), 48),
        name="combine_residual",
    )(y_sel, slot, gate, x, g2, fg)


def kernel(x, c, ctx, c_ctx, w_mod, b_mod, norm_g, att_w_in, att_w_out, na_rpb, diff_lambda, diff_subln_g,
           rec_w_in, rec_w_out, rec_lb_logits, rec_gnorm_g, moe_router, moe_w_gate, moe_w_up, moe_w_down,
           final_g):
    b, t, d = x.shape
    l_ctx = ctx.shape[1]
    n_c = b + 1
    r_pad = -(-n_c // 8) * 8
    c_all = jnp.concatenate([c, c_ctx[None, :], jnp.zeros((r_pad - n_c, d), F32)], axis=0)
    mods = modulation_all(c_all, w_mod, b_mod)
    cos, sin = rope_tables(t)

    for l in range(DEPTH):
        last = l == DEPTH - 1
        lat_m = [mods[l, :b, i * d:(i + 1) * d][:, None, :] for i in range(6)]
        ctx_m = [jnp.broadcast_to(mods[l, b, i * d:(i + 1) * d][None, None, :], (b, 1, d)) for i in range(6)]
        sh1, sc1, g1, sh2, sc2, g2 = lat_m
        csh1, csc1, cg1, csh2, csc2, cg2 = ctx_m

        if l % 2 == 0:
            e = l // 2
            w_in = att_w_in[e].astype(BF16)
            w_out = att_w_out[e].astype(BF16)
            p_lat = norm_mod_proj(x, norm_g[l, 0], sc1, sh1, w_in, BF16)
            p_ctx = norm_mod_proj(ctx, norm_g[l, 0], csc1, csh1, w_in, BF16)
            bias = na_bias_tables(na_rpb[e], t // GRID_W, l_ctx)
            oa_lat, oa_ctx = na_attention(p_lat, p_ctx, bias)
            ob_lat, ob_ctx = diff_attention(p_lat, p_ctx, cos, sin, diff_lambda[e], diff_subln_g[e], l)
            ws = [w_out[:A_W], w_out[A_W:]]
            x = outproj_residual([oa_lat, ob_lat], ws, x, g1)
            if not last:
                ctx = outproj_residual([oa_ctx, ob_ctx], ws, ctx, cg1)
        else:
            o = l // 2
            w_in = rec_w_in[o].astype(BF16)
            w_out = rec_w_out[o].astype(BF16)
            p_lat = norm_mod_proj(x, norm_g[l, 0], sc1, sh1, w_in, F32)
            p_ctx = norm_mod_proj(ctx, norm_g[l, 0], csc1, csh1, w_in, F32)
            y_lat, y_ctx = hgrn2_mixer(p_lat, p_ctx, rec_lb_logits, rec_gnorm_g[o], l)
            x = outproj_residual([y_lat], [w_out], x, g1)
            if not last:
                ctx = outproj_residual([y_ctx], [w_out], ctx, cg1)

        xsel_l, slot_l, gate_l = route_gather(x, norm_g[l, 1], sc2, sh2, moe_router[l])
        if last:
            (y_l,) = expert_ffn([xsel_l], moe_w_gate, moe_w_up, moe_w_down, l)
            x = combine_residual(y_l, slot_l, gate_l, x, g2, final_g)
        else:
            xsel_c, slot_c, gate_c = route_gather(ctx, norm_g[l, 1], csc2, csh2, moe_router[l])
            y_l, y_c = expert_ffn([xsel_l, xsel_c], moe_w_gate, moe_w_up, moe_w_down, l)
            x = combine_residual(y_l, slot_l, gate_l, x, g2)
            ctx = combine_residual(y_c, slot_c, gate_c, ctx, cg2)
    return x
```

```python
import functools
import math

import numpy as np
import jax
import jax.numpy as jnp
from jax import lax
from jax.experimental import pallas as pl
from jax.experimental.pallas import tpu as pltpu

F32 = jnp.float32
BF16 = jnp.bfloat16

D_MODEL = 1024
DEPTH = 4
GRID_W = 64
HEAD_DIM = 64
NA_HEADS = 8
NA_WIN_H = 8
NA_WIN_W = 16
DIFF_HEADS = 4
A_W = NA_HEADS * HEAD_DIM
ATT_IN_W = 3072
REC_HEADS = 8
REC_D = 128
N_EXPERTS = 16
EXPERT_FF = 2 * D_MODEL
EC_CAPACITY_FACTOR = 2
ROPE_THETA = 10000.0
NORM_EPS = 1e-6
NEG_INF = -1e30

LANES = 128
V7X_VMEM_BYTES = 64 * 1024 * 1024
MIB = 1024 * 1024

NA_QROWS = 4
NA_KROWS = NA_QROWS + NA_WIN_H - 1
NA_BLOCKS_PER_STEP = 2
DIFF_BLOCKS_PER_STEP = 2
REC_CHUNK = 64
REC_SUB = 16
REC_NSUB = REC_CHUNK // REC_SUB
REC_EXP_CLAMP = 60.0
REC_GROUP = 4
PREFIX_BLOCK = 256

LOG2E = math.log2(math.e)
SCORE_MULT = HEAD_DIM ** -0.5 * LOG2E

NT_DIMS = (((1,), (1,)), ((), ()))
TN_DIMS = (((0,), (0,)), ((), ()))


def _cparams(semantics, vmem_mib):
    assert vmem_mib * MIB < V7X_VMEM_BYTES
    return pltpu.CompilerParams(dimension_semantics=semantics, vmem_limit_bytes=vmem_mib * MIB)


def _sigmoid(x):
    return 1.0 / (1.0 + jnp.exp(-x))


def _rms(x, eps=NORM_EPS):
    return x * lax.rsqrt(jnp.mean(x * x, axis=-1, keepdims=True) + eps)


def _mod_kernel(c_ref, w_ref, b_ref, o_ref):
    c = c_ref[...]
    s = (c * _sigmoid(c)).astype(BF16)
    o_ref[0] = jnp.dot(s, w_ref[0].astype(BF16), preferred_element_type=F32) + b_ref[0]


def modulation_all(c_all, w_mod, b_mod):
    depth, d, n = w_mod.shape
    r = c_all.shape[0]
    tn = 1536
    return pl.pallas_call(
        _mod_kernel,
        grid=(depth, n // tn),
        in_specs=[
            pl.BlockSpec((r, d), lambda l, j: (0, 0)),
            pl.BlockSpec((1, d, tn), lambda l, j: (l, 0, j)),
            pl.BlockSpec((1, 1, tn), lambda l, j: (l, 0, j)),
        ],
        out_specs=pl.BlockSpec((1, r, tn), lambda l, j: (l, 0, j)),
        out_shape=jax.ShapeDtypeStruct((depth, r, n), F32),
        compiler_params=_cparams(("parallel", "parallel"), 32),
        name="modulation",
    )(c_all, w_mod, b_mod.reshape(depth, 1, n))


def _norm_mod(x, g, sc, sh):
    return _rms(x) * g * (1.0 + sc) + sh


def _proj_kernel(x_ref, g_ref, sc_ref, sh_ref, w_ref, o_ref, h_scr, *, tn):
    h_scr[...] = _norm_mod(x_ref[0], g_ref[...], sc_ref[0], sh_ref[0]).astype(BF16)
    n = w_ref.shape[1]
    for j in range(n // tn):
        o_ref[0, :, j * tn:(j + 1) * tn] = jnp.dot(
            h_scr[...], w_ref[:, j * tn:(j + 1) * tn], preferred_element_type=F32).astype(o_ref.dtype)


def norm_mod_proj(x, g, sc, sh, w_bf16, out_dtype):
    b, t, d = x.shape
    n = w_bf16.shape[1]
    tm = min(t, 256)
    return pl.pallas_call(
        functools.partial(_proj_kernel, tn=512),
        grid=(b, t // tm),
        in_specs=[
            pl.BlockSpec((1, tm, d), lambda i, j: (i, j, 0)),
            pl.BlockSpec((1, d), lambda i, j: (0, 0)),
            pl.BlockSpec((1, 1, d), lambda i, j: (i, 0, 0)),
            pl.BlockSpec((1, 1, d), lambda i, j: (i, 0, 0)),
            pl.BlockSpec((d, n), lambda i, j: (0, 0)),
        ],
        out_specs=pl.BlockSpec((1, tm, n), lambda i, j: (i, j, 0)),
        out_shape=jax.ShapeDtypeStruct((b, t, n), out_dtype),
        scratch_shapes=[pltpu.VMEM((tm, d), BF16)],
        compiler_params=_cparams(("parallel", "parallel"), 48),
        name="norm_mod_proj",
    )(x, g.reshape(1, d), sc, sh, w_bf16)


def _outproj_kernel(*refs, n_in):
    o_refs, w_refs = refs[:n_in], refs[n_in:2 * n_in]
    x_ref, gate_ref, out_ref = refs[2 * n_in:]
    acc = jnp.dot(o_refs[0][0], w_refs[0][...], preferred_element_type=F32)
    for o_ref, w_ref in zip(o_refs[1:], w_refs[1:]):
        acc = acc + jnp.dot(o_ref[0], w_ref[...], preferred_element_type=F32)
    out_ref[0] = x_ref[0] + gate_ref[0] * acc


def outproj_residual(os_, ws, x, gate):
    b, t, d = x.shape
    tm = min(t, 512)
    n_in = len(os_)
    in_specs = [pl.BlockSpec((1, tm, o.shape[2]), lambda i, j: (i, j, 0)) for o in os_]
    in_specs += [pl.BlockSpec(w.shape, lambda i, j: (0, 0)) for w in ws]
    in_specs += [pl.BlockSpec((1, tm, d), lambda i, j: (i, j, 0)),
                 pl.BlockSpec((1, 1, d), lambda i, j: (i, 0, 0))]
    return pl.pallas_call(
        functools.partial(_outproj_kernel, n_in=n_in),
        grid=(b, t // tm),
        in_specs=in_specs,
        out_specs=pl.BlockSpec((1, tm, d), lambda i, j: (i, j, 0)),
        out_shape=jax.ShapeDtypeStruct((b, t, d), F32),
        compiler_params=_cparams(("parallel", "parallel"), 32),
        name="outproj_residual",
    )(*os_, *ws, x, gate)


def _na_block_tables(rows):
    kh = min(NA_WIN_H, rows)
    assert kh == NA_WIN_H and rows % NA_QROWS == 0 and rows >= NA_KROWS
    n_qb = rows // NA_QROWS
    starts = [min(max(qb * NA_QROWS - kh // 2, 0), rows - NA_KROWS) for qb in range(n_qb)]
    tbl_of, reps = [], []
    for qb in range(n_qb):
        r0 = qb * NA_QROWS
        interior = (r0 - kh // 2 >= 0) and (r0 + NA_QROWS - 1 - kh // 2 <= rows - kh) and starts[qb] == r0 - kh // 2
        key = "interior" if interior else qb
        if key not in reps:
            reps.append(key)
        tbl_of.append(reps.index(key))
    rep_qb = [next(qb for qb in range(n_qb) if tbl_of[qb] == i) for i in range(len(reps))]
    tiles = []
    for qb in rep_qb:
        tiles.append([])
        for qi in range(NA_QROWS):
            r = qb * NA_QROWS + qi
            rs = min(max(r - kh // 2, 0), rows - kh)
            tiles[-1].append([(starts[qb] + kk - r + NA_WIN_H - 1) if rs <= starts[qb] + kk < rs + kh else None
                              for kk in range(NA_KROWS)])
    return starts, tbl_of, tiles


def na_bias_tables(rpb, rows, ctx_len, mult):
    _, _, tiles = _na_block_tables(rows)
    c = np.arange(GRID_W)
    cs = np.clip(c - NA_WIN_W // 2, 0, GRID_W - NA_WIN_W)
    col_ok = (c[None, :] >= cs[:, None]) & (c[None, :] < cs[:, None] + NA_WIN_W)
    dcol = c[None, :] - c[:, None] + NA_WIN_W - 1
    pick = ((dcol[None] == np.arange(2 * NA_WIN_W - 1)[:, None, None]) & col_ok[None]).astype(np.float32)
    toe = jnp.einsum("hrd,dqk->hrqk", rpb.astype(F32), jnp.asarray(pick), precision=lax.Precision.HIGHEST)
    toe = jnp.where(jnp.asarray(col_ok), toe * mult, NEG_INF)
    neg = jnp.full((rpb.shape[0], GRID_W, GRID_W), NEG_INF, F32)
    tables = [jnp.concatenate([jnp.concatenate([neg if d is None else toe[:, d] for d in row], axis=-1)
                               for row in tbl], axis=-2) for tbl in tiles]
    loc = jnp.stack(tables, axis=1)
    return jnp.concatenate([loc, jnp.zeros(loc.shape[:3] + (ctx_len,), F32)], axis=-1)


def _attend_many(qs, ks, vs, biases=None):
    s2 = [lax.dot_general(q, k, NT_DIMS, preferred_element_type=F32) for q, k in zip(qs, ks)]
    if biases is not None:
        s2 = [s + b for s, b in zip(s2, biases)]
    ps = [jnp.exp2(s - jnp.max(s, axis=-1, keepdims=True)) for s in s2]
    inv = [1.0 / jnp.sum(p, axis=-1, keepdims=True) for p in ps]
    return [jnp.dot(p.astype(BF16), v, preferred_element_type=F32) * r for p, v, r in zip(ps, vs, inv)]


def _head_half_scales(mult):
    first = lax.broadcasted_iota(jnp.int32, (1, LANES), 1) < HEAD_DIM
    return first, jnp.where(first, mult, 0.0), jnp.where(first, 0.0, mult)


def _na_kernel(q_ref, k_ref, v_ref, qc_ref, kc_ref, vc_ref, bias_ref, o_ref, oc_ref, *, starts, tbl_of):
    first, *halves = _head_half_scales(SCORE_MULT)
    nq, nk = NA_QROWS * GRID_W, NA_KROWS * GRID_W
    kc, vc = kc_ref[0], vc_ref[0]
    n_qb = len(starts)
    assert n_qb % NA_BLOCKS_PER_STEP == 0
    for qb0 in range(0, n_qb, NA_BLOCKS_PER_STEP):
        blocks = range(qb0, qb0 + NA_BLOCKS_PER_STEP)
        qs, ks, vs, bs = [], [], [], []
        for qb in blocks:
            start, tbl = starts[qb], tbl_of[qb]
            q = q_ref[0, qb * nq:(qb + 1) * nq, :].astype(F32)
            k_all = jnp.concatenate([k_ref[0, start * GRID_W:start * GRID_W + nk, :], kc], axis=0)
            v_all = jnp.concatenate([v_ref[0, start * GRID_W:start * GRID_W + nk, :], vc], axis=0)
            for hh in range(2):
                qs.append((q * halves[hh]).astype(BF16))
                ks.append(k_all)
                vs.append(v_all)
                bs.append(bias_ref[hh, tbl])
        outs = _attend_many(qs, ks, vs, bs)
        for i, qb in enumerate(blocks):
            o_ref[0, qb * nq:(qb + 1) * nq, :] = jnp.where(first, outs[2 * i], outs[2 * i + 1]).astype(o_ref.dtype)
    qc = qc_ref[0].astype(F32)
    outs = _attend_many([(qc * h).astype(BF16) for h in halves], [kc, kc], [vc, vc])
    oc_ref[0] = jnp.where(first, outs[0], outs[1]).astype(oc_ref.dtype)


def na_attention(p_lat, p_ctx, bias):
    b, t, _ = p_lat.shape
    l = p_ctx.shape[1]
    rows = t // GRID_W
    starts, tbl_of, _ = _na_block_tables(rows)
    n_pairs = NA_HEADS // 2
    kb, vb = 2 * A_W // LANES, 3 * A_W // LANES
    lat = lambda off: pl.BlockSpec((1, t, LANES), lambda j, i: (i, 0, off + j))
    cx = lambda off: pl.BlockSpec((1, l, LANES), lambda j, i: (i, 0, off + j))
    n_tbl, nq, nkc = bias.shape[1:]
    return pl.pallas_call(
        functools.partial(_na_kernel, starts=tuple(starts), tbl_of=tuple(tbl_of)),
        grid=(n_pairs, b),
        in_specs=[lat(0), lat(kb), lat(vb), cx(0), cx(kb), cx(vb),
                  pl.BlockSpec((2, n_tbl, nq, nkc), lambda j, i: (j, 0, 0, 0))],
        out_specs=[pl.BlockSpec((1, t, LANES), lambda j, i: (i, 0, j)),
                   pl.BlockSpec((1, l, LANES), lambda j, i: (i, 0, j))],
        out_shape=[jax.ShapeDtypeStruct((b, t, A_W), BF16), jax.ShapeDtypeStruct((b, l, A_W), BF16)],
        compiler_params=_cparams(("parallel", "parallel"), 48),
        name="na_attention",
    )(p_lat, p_lat, p_lat, p_ctx, p_ctx, p_ctx, bias)


def rope_tables(n):
    t = np.arange(n)
    rows, cols = (t // GRID_W).astype(np.float64), (t % GRID_W).astype(np.float64)
    n_freq = HEAD_DIM // 4
    inv = ROPE_THETA ** (-np.arange(n_freq, dtype=np.float64) / n_freq)
    ang = np.concatenate([rows[:, None] * inv, cols[:, None] * inv], axis=-1)
    cos = np.tile(np.cos(ang), (1, 4))
    sin = np.tile(np.concatenate([-np.sin(ang), np.sin(ang)], axis=-1), (1, 2))
    return jnp.asarray(cos, F32), jnp.asarray(sin, F32)


def _rope(a, cos, sin_signed):
    lane = lax.broadcasted_iota(jnp.int32, (1, LANES), 1)
    low = (lane % HEAD_DIM) < HEAD_DIM // 2
    partner = jnp.where(low, pltpu.roll(a, LANES - HEAD_DIM // 2, 1), pltpu.roll(a, HEAD_DIM // 2, 1))
    return a * cos + partner * sin_signed


def _diff_kernel(q_ref, k_ref, v_ref, qc_ref, kc_ref, vc_ref, cos_ref, sin_ref, lam_ref, g_ref,
                 o_ref, oc_ref, q1_scr, q2_scr, k_scr, v_scr, *, lam_init, tq):
    t, l = q_ref.shape[1], kc_ref.shape[1]
    _, half1, half2 = _head_half_scales(SCORE_MULT)
    lv = lam_ref[...]
    lam = (jnp.exp(jnp.sum(lv[0:1] * lv[1:2], axis=-1, keepdims=True))
           - jnp.exp(jnp.sum(lv[2:3] * lv[3:4], axis=-1, keepdims=True)) + lam_init)
    cos, sin = cos_ref[...], sin_ref[...]
    q = _rope(q_ref[0].astype(F32), cos, sin)
    q1_scr[...] = (q * half1).astype(BF16)
    q2_scr[...] = (q * half2).astype(BF16)
    k_scr[0:t, :] = _rope(k_ref[0].astype(F32), cos, sin).astype(BF16)
    k_scr[t:t + l, :] = kc_ref[0]
    v_scr[0:t, :] = v_ref[0]
    v_scr[t:t + l, :] = vc_ref[0]
    gain = g_ref[...] * (1.0 - lam_init)

    def attend(q_pairs, k, v):
        flat = [q for pair in q_pairs for q in pair]
        heads = _attend_many(flat, [k] * len(flat), [v] * len(flat))
        return [_rms(heads[2 * i] - lam * heads[2 * i + 1]) * gain for i in range(len(q_pairs))]

    def body(i, carry):
        rs = [pl.multiple_of((i * DIFF_BLOCKS_PER_STEP + j) * tq, tq) for j in range(DIFF_BLOCKS_PER_STEP)]
        outs = attend([(q1_scr[pl.ds(r, tq), :], q2_scr[pl.ds(r, tq), :]) for r in rs], k_scr[...], v_scr[...])
        for r, o in zip(rs, outs):
            o_ref[0, pl.ds(r, tq), :] = o.astype(o_ref.dtype)
        return carry

    lax.fori_loop(0, t // (tq * DIFF_BLOCKS_PER_STEP), body, 0)
    qc = qc_ref[0].astype(F32)
    (oc,) = attend([((qc * half1).astype(BF16), (qc * half2).astype(BF16))], kc_ref[0], vc_ref[0])
    oc_ref[0] = oc.astype(oc_ref.dtype)


def diff_attention(p_lat, p_ctx, cos, sin, lam_vecs, subln_g, layer_idx):
    b, t, _ = p_lat.shape
    l = p_ctx.shape[1]
    lam_init = 0.8 - 0.6 * math.exp(-0.3 * layer_idx)
    qb, kb, vb = A_W // LANES, 4 * A_W // LANES, 5 * A_W // LANES
    lat = lambda off: pl.BlockSpec((1, t, LANES), lambda i, j: (i, 0, off + j))
    cx = lambda off: pl.BlockSpec((1, l, LANES), lambda i, j: (i, 0, off + j))
    const = lambda shape: pl.BlockSpec(shape, lambda i, j: (0, 0))
    tq = 256
    assert t % (tq * DIFF_BLOCKS_PER_STEP) == 0
    return pl.pallas_call(
        functools.partial(_diff_kernel, lam_init=lam_init, tq=tq),
        grid=(b, DIFF_HEADS),
        in_specs=[lat(qb), lat(kb), lat(vb), cx(qb), cx(kb), cx(vb),
                  const((t, LANES)), const((t, LANES)), const(lam_vecs.shape), const((1, LANES))],
        out_specs=[pl.BlockSpec((1, t, LANES), lambda i, j: (i, 0, j)),
                   pl.BlockSpec((1, l, LANES), lambda i, j: (i, 0, j))],
        out_shape=[jax.ShapeDtypeStruct((b, t, A_W), BF16), jax.ShapeDtypeStruct((b, l, A_W), BF16)],
        scratch_shapes=[pltpu.VMEM((t, LANES), BF16), pltpu.VMEM((t, LANES), BF16),
                        pltpu.VMEM((t + l, LANES), BF16), pltpu.VMEM((t + l, LANES), BF16)],
        compiler_params=_cparams(("parallel", "parallel"), 48),
        name="diff_attention",
    )(p_lat, p_lat, p_lat, p_ctx, p_ctx, p_ctx, cos, sin, lam_vecs, subln_g.reshape(1, LANES))


def _split3(x):
    a = x.astype(BF16)
    r = x - a.astype(F32)
    b = r.astype(BF16)
    c = (r - b.astype(F32)).astype(BF16)
    return a, b, c


def _rec_masks(forward):
    t = lax.broadcasted_iota(jnp.int32, (REC_CHUNK, REC_CHUNK), 0)
    s = lax.broadcasted_iota(jnp.int32, (REC_CHUNK, REC_CHUNK), 1)
    if forward:
        incl, excl = s <= t, (s // REC_SUB) < (t // REC_SUB)
    else:
        incl, excl = s >= t, (s // REC_SUB) > (t // REC_SUB)
    both = jnp.concatenate([jnp.where(incl, 1.0, 0.0), jnp.where(excl, 1.0, 0.0)], axis=0).astype(BF16)
    return both, incl


def _rec_group_local(qs, fxs, vb, lbs):
    c = REC_CHUNK
    items = [(d, j) for j in range(REC_GROUP) for d in range(2)]
    rows = lambda a, j: a[j * c:(j + 1) * c]
    masks = [_rec_masks(True), _rec_masks(False)]
    kks, g3s = [], []
    for fx, lb in zip(fxs, lbs):
        e = jnp.exp(-jnp.abs(fx))
        r = 1.0 / (1.0 + e)
        pos = fx >= 0
        f = lb + (1.0 - lb) * jnp.where(pos, r, e * r)
        kks.append((1.0 - lb) * jnp.where(pos, e * r, r))
        g3s.append(jnp.concatenate(_split3(jnp.log(f)), axis=1))
    cs = {(d, j): jnp.dot(masks[d][0], rows(g3s[d], j), preferred_element_type=F32) for d, j in items}
    row_sub = lax.broadcasted_iota(jnp.int32, (c, 1), 0) // REC_SUB
    qcat, kcat, qtil, kdec, dec = {}, {}, {}, {}, {}
    for d, j in items:
        x = cs[d, j]
        x = x[:, 0:LANES] + x[:, LANES:2 * LANES] + x[:, 2 * LANES:3 * LANES]
        bcum, bsub = x[0:c], x[c:2 * c]
        q, kk = rows(qs, j), rows(kks[d], j)
        qhat = q * jnp.exp(bcum - bsub)
        qcat[d, j] = jnp.concatenate([jnp.where(row_sub == p, qhat, 0.0) for p in range(REC_NSUB)],
                                     axis=1).astype(BF16)
        kcat[d, j] = jnp.concatenate(
            [kk * jnp.exp(jnp.minimum(bsub[p * REC_SUB:p * REC_SUB + 1] - bcum, REC_EXP_CLAMP))
             for p in range(REC_NSUB)], axis=1).astype(BF16)
        qtil[d, j] = (q * jnp.exp(bcum)).astype(BF16)
        blast = bcum[c - 1:c] if d == 0 else bcum[0:1]
        kdec[d, j] = (kk * jnp.exp(blast - bcum)).astype(BF16)
        dec[d, j] = jnp.exp(blast)
    sc = {k: lax.dot_general(qcat[k], kcat[k], NT_DIMS, preferred_element_type=F32) for k in items}
    u = {(d, j): lax.dot_general(rows(vb, j), kdec[d, j], TN_DIMS, preferred_element_type=F32)
         for d, j in items}
    scb = {(d, j): jnp.where(masks[d][1], sc[d, j], 0.0).astype(BF16) for d, j in items}
    o_intra = {(d, j): jnp.dot(scb[d, j], rows(vb, j), preferred_element_type=F32) for d, j in items}
    return {k: (o_intra[k], qtil[k], u[k], dec[k]) for k in items}


def _rec_kernel(q_ref, ff_ref, fb_ref, v_ref, g_ref, qc_ref, ffc_ref, fbc_ref, vc_ref, gc_ref,
                lb_ref, gn_ref, y_ref, yc_ref, o_scr, qt_scr, u_scr, dec_scr, *, layer_idx):
    t, l = q_ref.shape[1], qc_ref.shape[1]
    c, grp = REC_CHUNK, REC_GROUP

    def lower_bound(d):
        lg = lb_ref[d]
        ex = jnp.exp(lg - jnp.max(lg, axis=0, keepdims=True))
        return jnp.sum(ex[1:layer_idx + 1], axis=0, keepdims=True) / jnp.sum(ex, axis=0, keepdims=True)

    lbs = (lower_bound(0), lower_bound(1))

    def local_pass(q_r, f_refs, v_r, n_tok, tok0):
        def body(i, carry):
            r0 = pl.multiple_of(i * (grp * c), grp * c)
            q = q_r[0, pl.ds(r0, grp * c), :]
            qs = q * _sigmoid(q)
            vb = v_r[0, pl.ds(r0, grp * c), :].astype(BF16)
            fxs = [f_r[0, pl.ds(r0, grp * c), :] for f_r in f_refs]
            local = _rec_group_local(qs, fxs, vb, lbs)
            for j in range(grp):
                row = pl.multiple_of(tok0 + r0 + j * c, c)
                ci = tok0 // c + i * grp + j
                for d in range(2):
                    o_intra, qtil, u, dec = local[d, j]
                    o_scr[d, pl.ds(row, c), :] = o_intra
                    qt_scr[d, pl.ds(row, c), :] = qtil
                    u_scr[d, ci] = u
                    dec_scr[d, pl.ds(ci, 1), :] = dec
            return carry
        lax.fori_loop(0, n_tok // (grp * c), body, 0)

    def state_pass(c_lo, n, states):
        def body(i, carry):
            sts = list(carry)
            for k in range(grp):
                step = i * grp + k
                for d, ci in ((0, c_lo + step), (1, c_lo + n - 1 - step)):
                    row = pl.multiple_of(ci * c, c)
                    o_scr[d, pl.ds(row, c), :] += lax.dot_general(
                        qt_scr[d, pl.ds(row, c), :], sts[d].astype(BF16), NT_DIMS, preferred_element_type=F32)
                    sts[d] = sts[d] * dec_scr[d, pl.ds(ci, 1), :] + u_scr[d, ci]
            return tuple(sts)
        return lax.fori_loop(0, n // grp, body, states)

    def finish(tok0, n_tok, gate_ref, out_ref):
        o = o_scr[0, tok0:tok0 + n_tok, :] + o_scr[1, tok0:tok0 + n_tok, :]
        gate = gate_ref[0]
        out_ref[0] = ((_rms(o) * gn_ref[...]) * (gate * _sigmoid(gate))).astype(out_ref.dtype)

    local_pass(qc_ref, (ffc_ref, fbc_ref), vc_ref, l, 0)
    local_pass(q_ref, (ff_ref, fb_ref), v_ref, t, l)
    zero = jnp.zeros((REC_D, REC_D), F32)
    states = state_pass(0, l // c, (zero, zero))
    state_pass(l // c, t // c, states)
    finish(0, l, gc_ref, yc_ref)
    finish(l, t, g_ref, y_ref)


def hgrn2_mixer(p_lat, p_ctx, lb_logits, gnorm_g, layer_idx):
    b, t, _ = p_lat.shape
    l = p_ctx.shape[1]
    assert t % (REC_GROUP * REC_CHUNK) == 0 and l % (REC_GROUP * REC_CHUNK) == 0
    n_chunks = (t + l) // REC_CHUNK
    lat = lambda k: pl.BlockSpec((1, t, REC_D), lambda i, j: (i, 0, k * REC_HEADS + j))
    cx = lambda k: pl.BlockSpec((1, l, REC_D), lambda i, j: (i, 0, k * REC_HEADS + j))
    return pl.pallas_call(
        functools.partial(_rec_kernel, layer_idx=layer_idx),
        grid=(b, REC_HEADS),
        in_specs=[lat(k) for k in range(5)] + [cx(k) for k in range(5)] + [
            pl.BlockSpec((2, DEPTH, REC_D), lambda i, j: (0, 0, j)),
            pl.BlockSpec((1, REC_D), lambda i, j: (0, 0))],
        out_specs=[pl.BlockSpec((1, t, REC_D), lambda i, j: (i, 0, j)),
                   pl.BlockSpec((1, l, REC_D), lambda i, j: (i, 0, j))],
        out_shape=[jax.ShapeDtypeStruct((b, t, D_MODEL), BF16), jax.ShapeDtypeStruct((b, l, D_MODEL), BF16)],
        scratch_shapes=[pltpu.VMEM((2, t + l, REC_D), F32), pltpu.VMEM((2, t + l, REC_D), BF16),
                        pltpu.VMEM((2, n_chunks, REC_D, REC_D), F32), pltpu.VMEM((2, n_chunks, REC_D), F32)],
        compiler_params=_cparams(("parallel", "parallel"), 40),
        name="hgrn2_mixer",
    )(*([p_lat] * 5), *([p_ctx] * 5), lb_logits, gnorm_g.reshape(1, REC_D))


def _prefix_count(flags):
    e, t = flags.shape
    blk = min(PREFIX_BLOCK, t)
    ones = jnp.where(flags, 1.0, 0.0)
    s = lax.broadcasted_iota(jnp.int32, (blk, blk), 0)
    u = lax.broadcasted_iota(jnp.int32, (blk, blk), 1)
    upper = jnp.where(s < u, 1.0, 0.0).astype(BF16)
    carry = jnp.zeros((e, 1), F32)
    outs = []
    for i in range(t // blk):
        xs = ones[:, i * blk:(i + 1) * blk]
        outs.append(jnp.dot(xs.astype(BF16), upper, preferred_element_type=F32) + carry)
        carry = carry + jnp.sum(xs, axis=1, keepdims=True)
    return jnp.concatenate(outs, axis=1)


def _route_kernel(x_ref, g_ref, sc_ref, sh_ref, wr_ref, xsel_ref, slot_ref, gate_ref, hb_scr, slot_scr, *, cap):
    e = pl.program_id(1)
    t = x_ref.shape[1]

    @pl.when(e == 0)
    def _():
        h = _norm_mod(x_ref[0], g_ref[...], sc_ref[0], sh_ref[0])
        hb_scr[...] = h.astype(BF16)
        logits = lax.dot_general(wr_ref[...], h, NT_DIMS, precision=lax.Precision.HIGHEST,
                                 preferred_element_type=F32)
        ex = jnp.exp(logits - jnp.max(logits, axis=0, keepdims=True))
        aff = ex / jnp.sum(ex, axis=0, keepdims=True)
        bits = lax.bitcast_convert_type(aff, jnp.int32)
        capf = jnp.float32(cap)
        thr = jnp.zeros((N_EXPERTS, 1), jnp.int32)
        for bit in range(30, -1, -1):
            cand = thr | jnp.int32(1 << bit)
            cnt = jnp.sum(jnp.where(bits >= cand, 1.0, 0.0), axis=1, keepdims=True)
            thr = jnp.where(cnt >= capf, cand, thr)
        gt = bits > thr
        eq = bits == thr
        need = capf - jnp.sum(jnp.where(gt, 1.0, 0.0), axis=1, keepdims=True)
        sel = gt | (eq & (_prefix_count(eq) < need))
        slot = jnp.where(sel, _prefix_count(sel), -1.0)
        slot_scr[...] = slot
        slot_ref[0] = slot
        gate_ref[0] = aff

    srow = slot_scr[pl.ds(e, 1), :]
    onehot = lax.broadcasted_iota(jnp.int32, (cap, t), 0).astype(F32) == srow
    xsel_ref[0] = jnp.dot(jnp.where(onehot, 1.0, 0.0).astype(BF16), hb_scr[...],
                          preferred_element_type=F32).astype(xsel_ref.dtype)


def route_gather(x, g, sc, sh, w_router):
    b, t, d = x.shape
    cap = EC_CAPACITY_FACTOR * t // N_EXPERTS
    return pl.pallas_call(
        functools.partial(_route_kernel, cap=cap),
        grid=(b, N_EXPERTS),
        in_specs=[
            pl.BlockSpec((1, t, d), lambda i, e: (i, 0, 0)),
            pl.BlockSpec((1, d), lambda i, e: (0, 0)),
            pl.BlockSpec((1, 1, d), lambda i, e: (i, 0, 0)),
            pl.BlockSpec((1, 1, d), lambda i, e: (i, 0, 0)),
            pl.BlockSpec((N_EXPERTS, d), lambda i, e: (0, 0)),
        ],
        out_specs=[pl.BlockSpec((1, cap, d), lambda i, e: (e, i, 0)),
                   pl.BlockSpec((1, N_EXPERTS, t), lambda i, e: (i, 0, 0)),
                   pl.BlockSpec((1, N_EXPERTS, t), lambda i, e: (i, 0, 0))],
        out_shape=[jax.ShapeDtypeStruct((N_EXPERTS, b * cap, d), BF16),
                   jax.ShapeDtypeStruct((b, N_EXPERTS, t), F32),
                   jax.ShapeDtypeStruct((b, N_EXPERTS, t), F32)],
        scratch_shapes=[pltpu.VMEM((t, d), BF16), pltpu.VMEM((N_EXPERTS, t), F32)],
        compiler_params=_cparams(("parallel", "arbitrary"), 48),
        name="route_gather",
    )(x, g.reshape(1, d), sc, sh, w_router.T)


def _ffn_kernel(*refs, n_streams):
    x_refs = refs[:n_streams]
    wg_ref, wu_ref, wd_ref = refs[n_streams:n_streams + 3]
    y_refs = refs[n_streams + 3:2 * n_streams + 3]
    acc_refs = refs[2 * n_streams + 3:]
    f = pl.program_id(2)
    wg, wu, wd = wg_ref[0, 0].astype(BF16), wu_ref[0, 0].astype(BF16), wd_ref[0, 0].astype(BF16)

    @pl.when(f == 0)
    def _():
        for acc_ref in acc_refs:
            acc_ref[...] = jnp.zeros_like(acc_ref)

    xs = [x_ref[0] for x_ref in x_refs]
    gates = [jnp.dot(x, wg, preferred_element_type=F32) for x in xs]
    ups = [jnp.dot(x, wu, preferred_element_type=F32) for x in xs]
    hids = [((a * _sigmoid(a)) * u).astype(BF16) for a, u in zip(gates, ups)]
    for acc_ref, hid in zip(acc_refs, hids):
        acc_ref[...] += jnp.dot(hid, wd, preferred_element_type=F32)

    @pl.when(f == pl.num_programs(2) - 1)
    def _():
        for y_ref, acc_ref in zip(y_refs, acc_refs):
            y_ref[0] = acc_ref[...].astype(y_ref.dtype)


def expert_ffn(xsels, w_gate, w_up, w_down, layer_idx):
    d, ff = w_gate.shape[2], w_gate.shape[3]
    tf, n_split = 256, 2
    n_streams = len(xsels)
    row_blocks = [x.shape[1] // n_split for x in xsels]
    xspec = lambda rb: pl.BlockSpec((1, rb, d), lambda e, h, f: (e, h, 0))
    return pl.pallas_call(
        functools.partial(_ffn_kernel, n_streams=n_streams),
        grid=(N_EXPERTS, n_split, ff // tf),
        in_specs=[xspec(rb) for rb in row_blocks] + [
            pl.BlockSpec((1, 1, d, tf), lambda e, h, f: (layer_idx, e, 0, f)),
            pl.BlockSpec((1, 1, d, tf), lambda e, h, f: (layer_idx, e, 0, f)),
            pl.BlockSpec((1, 1, tf, d), lambda e, h, f: (layer_idx, e, f, 0))],
        out_specs=[xspec(rb) for rb in row_blocks],
        out_shape=[jax.ShapeDtypeStruct(x.shape, BF16) for x in xsels],
        scratch_shapes=[pltpu.VMEM((rb, d), F32) for rb in row_blocks],
        compiler_params=_cparams(("parallel", "parallel", "arbitrary"), 56),
        name="expert_ffn",
    )(*xsels, w_gate, w_up, w_down)


def _combine_kernel(y_ref, slot_ref, gate_ref, x_ref, g2_ref, fg_ref, out_ref, *, cap, final_norm):
    e = pl.program_id(2)
    tt = x_ref.shape[1]
    srow = slot_ref[0, pl.ds(e, 1), :]
    grow = gate_ref[0, pl.ds(e, 1), :]
    onehot = lax.broadcasted_iota(jnp.int32, (cap, tt), 0).astype(F32) == srow
    pg = jnp.where(onehot, grow, 0.0).astype(BF16)

    @pl.when(e == 0)
    def _():
        out_ref[...] = jnp.zeros_like(out_ref)

    out_ref[0] += lax.dot_general(pg, y_ref[0], TN_DIMS, preferred_element_type=F32)

    @pl.when(e == N_EXPERTS - 1)
    def _():
        xn = x_ref[0] + g2_ref[0] * out_ref[0]
        if final_norm:
            xn = _rms(xn) * fg_ref[...]
        out_ref[0] = xn


def combine_residual(y_sel, slot, gate, x, g2, final_g=None):
    b, t, d = x.shape
    cap = EC_CAPACITY_FACTOR * t // N_EXPERTS
    tt = min(t, 1024)
    final_norm = final_g is not None
    fg = (final_g if final_norm else jnp.ones((d,), F32)).reshape(1, d)
    return pl.pallas_call(
        functools.partial(_combine_kernel, cap=cap, final_norm=final_norm),
        grid=(b, t // tt, N_EXPERTS),
        in_specs=[
            pl.BlockSpec((1, cap, d), lambda i, j, e: (e, i, 0)),
            pl.BlockSpec((1, N_EXPERTS, tt), lambda i, j, e: (i, 0, j)),
            pl.BlockSpec((1, N_EXPERTS, tt), lambda i, j, e: (i, 0, j)),
            pl.BlockSpec((1, tt, d), lambda i, j, e: (i, j, 0)),
            pl.BlockSpec((1, 1, d), lambda i, j, e: (i, 0, 0)),
            pl.BlockSpec((1, d), lambda i, j, e: (0, 0)),
        ],
        out_specs=pl.BlockSpec((1, tt, d), lambda i, j, e: (i, j, 0)),
        out_shape=jax.ShapeDtypeStruct((b, t, d), F32),
        compiler_params=_cparams(("parallel", "parallel", "arbitrary"), 48),
        name="combine_residual",
    )(y_sel, slot, gate, x, g2, fg)


def kernel(x, c, ctx, c_ctx, w_mod, b_mod, norm_g, att_w_in, att_w_out, na_rpb, diff_lambda, diff_subln_g,
           rec_w_in, rec_w_out, rec_lb_logits, rec_gnorm_g, moe_router, moe_w_gate, moe_w_up, moe_w_down,
           final_g):
    b, t, d = x.shape
    l_ctx = ctx.shape[1]
    n_c = b + 1
    r_pad = -(-n_c // 8) * 8
    c_all = jnp.concatenate([c, c_ctx[None, :], jnp.zeros((r_pad - n_c, d), F32)], axis=0)
    mods = modulation_all(c_all, w_mod, b_mod)
    cos, sin = rope_tables(t)

    for l in range(DEPTH):
        last = l == DEPTH - 1
        lat_m = [mods[l, :b, i * d:(i + 1) * d][:, None, :] for i in range(6)]
        ctx_m = [jnp.broadcast_to(mods[l, b, i * d:(i + 1) * d][None, None, :], (b, 1, d)) for i in range(6)]
        sh1, sc1, g1, sh2, sc2, g2 = lat_m
        csh1, csc1, cg1, csh2, csc2, cg2 = ctx_m

        if l % 2 == 0:
            e = l // 2
            w_in = att_w_in[e].astype(BF16)
            w_out = att_w_out[e].astype(BF16)
            p_lat = norm_mod_proj(x, norm_g[l, 0], sc1, sh1, w_in, BF16)
            p_ctx = norm_mod_proj(ctx, norm_g[l, 0], csc1, csh1, w_in, BF16)
            bias = na_bias_tables(na_rpb[e], t // GRID_W, l_ctx, LOG2E)
            oa_lat, oa_ctx = na_attention(p_lat, p_ctx, bias)
            ob_lat, ob_ctx = diff_attention(p_lat, p_ctx, cos, sin, diff_lambda[e], diff_subln_g[e], l)
            ws = [w_out[:A_W], w_out[A_W:]]
            x = outproj_residual([oa_lat, ob_lat], ws, x, g1)
            if not last:
                ctx = outproj_residual([oa_ctx, ob_ctx], ws, ctx, cg1)
        else:
            o = l // 2
            w_in = rec_w_in[o].astype(BF16)
            w_out = rec_w_out[o].astype(BF16)
            p_lat = norm_mod_proj(x, norm_g[l, 0], sc1, sh1, w_in, F32)
            p_ctx = norm_mod_proj(ctx, norm_g[l, 0], csc1, csh1, w_in, F32)
            y_lat, y_ctx = hgrn2_mixer(p_lat, p_ctx, rec_lb_logits, rec_gnorm_g[o], l)
            x = outproj_residual([y_lat], [w_out], x, g1)
            if not last:
                ctx = outproj_residual([y_ctx], [w_out], ctx, cg1)

        xsel_l, slot_l, gate_l = route_gather(x, norm_g[l, 1], sc2, sh2, moe_router[l])
        if last:
            (y_l,) = expert_ffn([xsel_l], moe_w_gate, moe_w_up, moe_w_down, l)
            x = combine_residual(y_l, slot_l, gate_l, x, g2, final_g)
        else:
            xsel_c, slot_c, gate_c = route_gather(ctx, norm_g[l, 1], csc2, csh2, moe_router[l])
            y_l, y_c = expert_ffn([xsel_l, xsel_c], moe_w_gate, moe_w_up, moe_w_down, l)
            x = combine_residual(y_l, slot_l, gate_l, x, g2)
            ctx = combine_residual(y_c, slot_c, gate_c, ctx, cg2)
    return x
```

```python
import functools
import math

import numpy as np
import jax
import jax.numpy as jnp
from jax import lax
from jax.experimental import pallas as pl
from jax.experimental.pallas import tpu as pltpu

F32 = jnp.float32
BF16 = jnp.bfloat16

D_MODEL = 1024
DEPTH = 4
GRID_W = 64
HEAD_DIM = 64
NA_HEADS = 8
NA_WIN_H = 8
NA_WIN_W = 16
DIFF_HEADS = 4
A_W = NA_HEADS * HEAD_DIM
ATT_IN_W = 3072
REC_HEADS = 8
REC_D = 128
N_EXPERTS = 16
EXPERT_FF = 2 * D_MODEL
EC_CAPACITY_FACTOR = 2
ROPE_THETA = 10000.0
NORM_EPS = 1e-6
NEG_INF = -1e30

LANES = 128
V7X_VMEM_BYTES = 64 * 1024 * 1024
MIB = 1024 * 1024

NA_QROWS = 4
NA_KROWS = NA_QROWS + NA_WIN_H - 1
NA_BLOCKS_PER_STEP = 2
DIFF_BLOCKS_PER_STEP = 2
REC_CHUNK = 64
REC_SUB = 16
REC_NSUB = REC_CHUNK // REC_SUB
REC_EXP_CLAMP = 60.0
REC_GROUP = 4
PREFIX_BLOCK = 256
THRESHOLD_RADIX_BITS = 3

LOG2E = math.log2(math.e)
SCORE_MULT = HEAD_DIM ** -0.5 * LOG2E

NT_DIMS = (((1,), (1,)), ((), ()))
TN_DIMS = (((0,), (0,)), ((), ()))


def _cparams(semantics, vmem_mib):
    assert vmem_mib * MIB < V7X_VMEM_BYTES
    return pltpu.CompilerParams(dimension_semantics=semantics, vmem_limit_bytes=vmem_mib * MIB)


def _sigmoid(x):
    return 1.0 / (1.0 + jnp.exp(-x))


def _rms(x, eps=NORM_EPS):
    return x * lax.rsqrt(jnp.mean(x * x, axis=-1, keepdims=True) + eps)


def _mod_kernel(c_ref, w_ref, b_ref, o_ref):
    c = c_ref[...]
    s = (c * _sigmoid(c)).astype(BF16)
    o_ref[0] = jnp.dot(s, w_ref[0].astype(BF16), preferred_element_type=F32) + b_ref[0]


def modulation_all(c_all, w_mod, b_mod):
    depth, d, n = w_mod.shape
    r = c_all.shape[0]
    tn = 1536
    return pl.pallas_call(
        _mod_kernel,
        grid=(depth, n // tn),
        in_specs=[
            pl.BlockSpec((r, d), lambda l, j: (0, 0)),
            pl.BlockSpec((1, d, tn), lambda l, j: (l, 0, j)),
            pl.BlockSpec((1, 1, tn), lambda l, j: (l, 0, j)),
        ],
        out_specs=pl.BlockSpec((1, r, tn), lambda l, j: (l, 0, j)),
        out_shape=jax.ShapeDtypeStruct((depth, r, n), F32),
        compiler_params=_cparams(("parallel", "parallel"), 32),
        name="modulation",
    )(c_all, w_mod, b_mod.reshape(depth, 1, n))


def _norm_mod(x, g, sc, sh):
    return _rms(x) * g * (1.0 + sc) + sh


def _proj_kernel(x_ref, g_ref, sc_ref, sh_ref, w_ref, o_ref, h_scr, *, tn):
    h_scr[...] = _norm_mod(x_ref[0], g_ref[...], sc_ref[0], sh_ref[0]).astype(BF16)
    n = w_ref.shape[1]
    for j in range(n // tn):
        o_ref[0, :, j * tn:(j + 1) * tn] = jnp.dot(
            h_scr[...], w_ref[:, j * tn:(j + 1) * tn], preferred_element_type=F32).astype(o_ref.dtype)


def norm_mod_proj(x, g, sc, sh, w_bf16, out_dtype):
    b, t, d = x.shape
    n = w_bf16.shape[1]
    tm = min(t, 256)
    return pl.pallas_call(
        functools.partial(_proj_kernel, tn=512),
        grid=(b, t // tm),
        in_specs=[
            pl.BlockSpec((1, tm, d), lambda i, j: (i, j, 0)),
            pl.BlockSpec((1, d), lambda i, j: (0, 0)),
            pl.BlockSpec((1, 1, d), lambda i, j: (i, 0, 0)),
            pl.BlockSpec((1, 1, d), lambda i, j: (i, 0, 0)),
            pl.BlockSpec((d, n), lambda i, j: (0, 0)),
        ],
        out_specs=pl.BlockSpec((1, tm, n), lambda i, j: (i, j, 0)),
        out_shape=jax.ShapeDtypeStruct((b, t, n), out_dtype),
        scratch_shapes=[pltpu.VMEM((tm, d), BF16)],
        compiler_params=_cparams(("parallel", "parallel"), 48),
        name="norm_mod_proj",
    )(x, g.reshape(1, d), sc, sh, w_bf16)


def _outproj_kernel(*refs, n_in):
    o_refs, w_refs = refs[:n_in], refs[n_in:2 * n_in]
    x_ref, gate_ref, out_ref = refs[2 * n_in:]
    acc = jnp.dot(o_refs[0][0], w_refs[0][...], preferred_element_type=F32)
    for o_ref, w_ref in zip(o_refs[1:], w_refs[1:]):
        acc = acc + jnp.dot(o_ref[0], w_ref[...], preferred_element_type=F32)
    out_ref[0] = x_ref[0] + gate_ref[0] * acc


def outproj_residual(os_, ws, x, gate):
    b, t, d = x.shape
    tm = min(t, 512)
    n_in = len(os_)
    in_specs = [pl.BlockSpec((1, tm, o.shape[2]), lambda i, j: (i, j, 0)) for o in os_]
    in_specs += [pl.BlockSpec(w.shape, lambda i, j: (0, 0)) for w in ws]
    in_specs += [pl.BlockSpec((1, tm, d), lambda i, j: (i, j, 0)),
                 pl.BlockSpec((1, 1, d), lambda i, j: (i, 0, 0))]
    return pl.pallas_call(
        functools.partial(_outproj_kernel, n_in=n_in),
        grid=(b, t // tm),
        in_specs=in_specs,
        out_specs=pl.BlockSpec((1, tm, d), lambda i, j: (i, j, 0)),
        out_shape=jax.ShapeDtypeStruct((b, t, d), F32),
        compiler_params=_cparams(("parallel", "parallel"), 32),
        name="outproj_residual",
    )(*os_, *ws, x, gate)


def _na_block_tables(rows):
    kh = min(NA_WIN_H, rows)
    assert kh == NA_WIN_H and rows % NA_QROWS == 0 and rows >= NA_KROWS
    n_qb = rows // NA_QROWS
    starts = [min(max(qb * NA_QROWS - kh // 2, 0), rows - NA_KROWS) for qb in range(n_qb)]
    tbl_of, reps = [], []
    for qb in range(n_qb):
        r0 = qb * NA_QROWS
        interior = (r0 - kh // 2 >= 0) and (r0 + NA_QROWS - 1 - kh // 2 <= rows - kh) and starts[qb] == r0 - kh // 2
        key = "interior" if interior else qb
        if key not in reps:
            reps.append(key)
        tbl_of.append(reps.index(key))
    rep_qb = [next(qb for qb in range(n_qb) if tbl_of[qb] == i) for i in range(len(reps))]
    tiles = []
    for qb in rep_qb:
        tiles.append([])
        for qi in range(NA_QROWS):
            r = qb * NA_QROWS + qi
            rs = min(max(r - kh // 2, 0), rows - kh)
            tiles[-1].append([(starts[qb] + kk - r + NA_WIN_H - 1) if rs <= starts[qb] + kk < rs + kh else None
                              for kk in range(NA_KROWS)])
    return starts, tbl_of, tiles


def na_bias_tables(rpb, rows, ctx_len, mult):
    _, _, tiles = _na_block_tables(rows)
    c = np.arange(GRID_W)
    cs = np.clip(c - NA_WIN_W // 2, 0, GRID_W - NA_WIN_W)
    col_ok = (c[None, :] >= cs[:, None]) & (c[None, :] < cs[:, None] + NA_WIN_W)
    dcol = c[None, :] - c[:, None] + NA_WIN_W - 1
    pick = ((dcol[None] == np.arange(2 * NA_WIN_W - 1)[:, None, None]) & col_ok[None]).astype(np.float32)
    toe = jnp.einsum("hrd,dqk->hrqk", rpb.astype(F32), jnp.asarray(pick), precision=lax.Precision.HIGHEST)
    toe = jnp.where(jnp.asarray(col_ok), toe * mult, NEG_INF)
    neg = jnp.full((rpb.shape[0], GRID_W, GRID_W), NEG_INF, F32)
    tables = [jnp.concatenate([jnp.concatenate([neg if d is None else toe[:, d] for d in row], axis=-1)
                               for row in tbl], axis=-2) for tbl in tiles]
    loc = jnp.stack(tables, axis=1)
    return jnp.concatenate([loc, jnp.zeros(loc.shape[:3] + (ctx_len,), F32)], axis=-1)


def _attend_many(qs, ks, vs, biases=None):
    s2 = [lax.dot_general(q, k, NT_DIMS, preferred_element_type=F32) for q, k in zip(qs, ks)]
    if biases is not None:
        s2 = [s + b for s, b in zip(s2, biases)]
    ps = [jnp.exp2(s - jnp.max(s, axis=-1, keepdims=True)) for s in s2]
    inv = [1.0 / jnp.sum(p, axis=-1, keepdims=True) for p in ps]
    return [jnp.dot(p.astype(BF16), v, preferred_element_type=F32) * r for p, v, r in zip(ps, vs, inv)]


def _head_half_scales(mult):
    first = lax.broadcasted_iota(jnp.int32, (1, LANES), 1) < HEAD_DIM
    return first, jnp.where(first, mult, 0.0), jnp.where(first, 0.0, mult)


def _na_kernel(q_ref, k_ref, v_ref, qc_ref, kc_ref, vc_ref, bias_ref, o_ref, oc_ref, *, starts, tbl_of):
    first, *halves = _head_half_scales(SCORE_MULT)
    nq, nk = NA_QROWS * GRID_W, NA_KROWS * GRID_W
    kc, vc = kc_ref[0], vc_ref[0]
    n_qb = len(starts)
    assert n_qb % NA_BLOCKS_PER_STEP == 0
    for qb0 in range(0, n_qb, NA_BLOCKS_PER_STEP):
        blocks = range(qb0, qb0 + NA_BLOCKS_PER_STEP)
        qs, ks, vs, bs = [], [], [], []
        for qb in blocks:
            start, tbl = starts[qb], tbl_of[qb]
            q = q_ref[0, qb * nq:(qb + 1) * nq, :].astype(F32)
            k_all = jnp.concatenate([k_ref[0, start * GRID_W:start * GRID_W + nk, :], kc], axis=0)
            v_all = jnp.concatenate([v_ref[0, start * GRID_W:start * GRID_W + nk, :], vc], axis=0)
            for hh in range(2):
                qs.append((q * halves[hh]).astype(BF16))
                ks.append(k_all)
                vs.append(v_all)
                bs.append(bias_ref[hh, tbl])
        outs = _attend_many(qs, ks, vs, bs)
        for i, qb in enumerate(blocks):
            o_ref[0, qb * nq:(qb + 1) * nq, :] = jnp.where(first, outs[2 * i], outs[2 * i + 1]).astype(o_ref.dtype)
    qc = qc_ref[0].astype(F32)
    outs = _attend_many([(qc * h).astype(BF16) for h in halves], [kc, kc], [vc, vc])
    oc_ref[0] = jnp.where(first, outs[0], outs[1]).astype(oc_ref.dtype)


def na_attention(p_lat, p_ctx, bias):
    b, t, _ = p_lat.shape
    l = p_ctx.shape[1]
    rows = t // GRID_W
    starts, tbl_of, _ = _na_block_tables(rows)
    n_pairs = NA_HEADS // 2
    kb, vb = 2 * A_W // LANES, 3 * A_W // LANES
    lat = lambda off: pl.BlockSpec((1, t, LANES), lambda j, i: (i, 0, off + j))
    cx = lambda off: pl.BlockSpec((1, l, LANES), lambda j, i: (i, 0, off + j))
    n_tbl, nq, nkc = bias.shape[1:]
    return pl.pallas_call(
        functools.partial(_na_kernel, starts=tuple(starts), tbl_of=tuple(tbl_of)),
        grid=(n_pairs, b),
        in_specs=[lat(0), lat(kb), lat(vb), cx(0), cx(kb), cx(vb),
                  pl.BlockSpec((2, n_tbl, nq, nkc), lambda j, i: (j, 0, 0, 0))],
        out_specs=[pl.BlockSpec((1, t, LANES), lambda j, i: (i, 0, j)),
                   pl.BlockSpec((1, l, LANES), lambda j, i: (i, 0, j))],
        out_shape=[jax.ShapeDtypeStruct((b, t, A_W), BF16), jax.ShapeDtypeStruct((b, l, A_W), BF16)],
        compiler_params=_cparams(("parallel", "parallel"), 48),
        name="na_attention",
    )(p_lat, p_lat, p_lat, p_ctx, p_ctx, p_ctx, bias)


def rope_tables(n):
    t = np.arange(n)
    rows, cols = (t // GRID_W).astype(np.float64), (t % GRID_W).astype(np.float64)
    n_freq = HEAD_DIM // 4
    inv = ROPE_THETA ** (-np.arange(n_freq, dtype=np.float64) / n_freq)
    ang = np.concatenate([rows[:, None] * inv, cols[:, None] * inv], axis=-1)
    cos = np.tile(np.cos(ang), (1, 4))
    sin = np.tile(np.concatenate([-np.sin(ang), np.sin(ang)], axis=-1), (1, 2))
    return jnp.asarray(cos, F32), jnp.asarray(sin, F32)


def _rope(a, cos, sin_signed):
    lane = lax.broadcasted_iota(jnp.int32, (1, LANES), 1)
    low = (lane % HEAD_DIM) < HEAD_DIM // 2
    partner = jnp.where(low, pltpu.roll(a, LANES - HEAD_DIM // 2, 1), pltpu.roll(a, HEAD_DIM // 2, 1))
    return a * cos + partner * sin_signed


def _diff_kernel(q_ref, k_ref, v_ref, qc_ref, kc_ref, vc_ref, cos_ref, sin_ref, lam_ref, g_ref,
                 o_ref, oc_ref, q1_scr, q2_scr, k_scr, v_scr, *, lam_init, tq):
    t, l = q_ref.shape[1], kc_ref.shape[1]
    _, half1, half2 = _head_half_scales(SCORE_MULT)
    lv = lam_ref[...]
    lam = (jnp.exp(jnp.sum(lv[0:1] * lv[1:2], axis=-1, keepdims=True))
           - jnp.exp(jnp.sum(lv[2:3] * lv[3:4], axis=-1, keepdims=True)) + lam_init)
    cos, sin = cos_ref[...], sin_ref[...]
    q = _rope(q_ref[0].astype(F32), cos, sin)
    q1_scr[...] = (q * half1).astype(BF16)
    q2_scr[...] = (q * half2).astype(BF16)
    k_scr[0:t, :] = _rope(k_ref[0].astype(F32), cos, sin).astype(BF16)
    k_scr[t:t + l, :] = kc_ref[0]
    v_scr[0:t, :] = v_ref[0]
    v_scr[t:t + l, :] = vc_ref[0]
    gain = g_ref[...] * (1.0 - lam_init)

    def attend(q_pairs, k, v):
        flat = [q for pair in q_pairs for q in pair]
        heads = _attend_many(flat, [k] * len(flat), [v] * len(flat))
        return [_rms(heads[2 * i] - lam * heads[2 * i + 1]) * gain for i in range(len(q_pairs))]

    def body(i, carry):
        rs = [pl.multiple_of((i * DIFF_BLOCKS_PER_STEP + j) * tq, tq) for j in range(DIFF_BLOCKS_PER_STEP)]
        outs = attend([(q1_scr[pl.ds(r, tq), :], q2_scr[pl.ds(r, tq), :]) for r in rs], k_scr[...], v_scr[...])
        for r, o in zip(rs, outs):
            o_ref[0, pl.ds(r, tq), :] = o.astype(o_ref.dtype)
        return carry

    lax.fori_loop(0, t // (tq * DIFF_BLOCKS_PER_STEP), body, 0)
    qc = qc_ref[0].astype(F32)
    (oc,) = attend([((qc * half1).astype(BF16), (qc * half2).astype(BF16))], kc_ref[0], vc_ref[0])
    oc_ref[0] = oc.astype(oc_ref.dtype)


def diff_attention(p_lat, p_ctx, cos, sin, lam_vecs, subln_g, layer_idx):
    b, t, _ = p_lat.shape
    l = p_ctx.shape[1]
    lam_init = 0.8 - 0.6 * math.exp(-0.3 * layer_idx)
    qb, kb, vb = A_W // LANES, 4 * A_W // LANES, 5 * A_W // LANES
    lat = lambda off: pl.BlockSpec((1, t, LANES), lambda i, j: (i, 0, off + j))
    cx = lambda off: pl.BlockSpec((1, l, LANES), lambda i, j: (i, 0, off + j))
    const = lambda shape: pl.BlockSpec(shape, lambda i, j: (0, 0))
    tq = 256
    assert t % (tq * DIFF_BLOCKS_PER_STEP) == 0
    return pl.pallas_call(
        functools.partial(_diff_kernel, lam_init=lam_init, tq=tq),
        grid=(b, DIFF_HEADS),
        in_specs=[lat(qb), lat(kb), lat(vb), cx(qb), cx(kb), cx(vb),
                  const((t, LANES)), const((t, LANES)), const(lam_vecs.shape), const((1, LANES))],
        out_specs=[pl.BlockSpec((1, t, LANES), lambda i, j: (i, 0, j)),
                   pl.BlockSpec((1, l, LANES), lambda i, j: (i, 0, j))],
        out_shape=[jax.ShapeDtypeStruct((b, t, A_W), BF16), jax.ShapeDtypeStruct((b, l, A_W), BF16)],
        scratch_shapes=[pltpu.VMEM((t, LANES), BF16), pltpu.VMEM((t, LANES), BF16),
                        pltpu.VMEM((t + l, LANES), BF16), pltpu.VMEM((t + l, LANES), BF16)],
        compiler_params=_cparams(("parallel", "parallel"), 48),
        name="diff_attention",
    )(p_lat, p_lat, p_lat, p_ctx, p_ctx, p_ctx, cos, sin, lam_vecs, subln_g.reshape(1, LANES))


def _split2(x):
    a = x.astype(BF16)
    return a, (x - a.astype(F32)).astype(BF16)


def _rec_masks(forward):
    t = lax.broadcasted_iota(jnp.int32, (REC_CHUNK, REC_CHUNK), 0)
    s = lax.broadcasted_iota(jnp.int32, (REC_CHUNK, REC_CHUNK), 1)
    incl = (s <= t) if forward else (s >= t)
    return jnp.where(incl, 1.0, 0.0).astype(BF16), incl


def _rec_group_local(qs, fxs, vb, lbs):
    c = REC_CHUNK
    items = [(d, j) for j in range(REC_GROUP) for d in range(2)]
    rows = lambda a, j: a[j * c:(j + 1) * c]
    masks = [_rec_masks(True), _rec_masks(False)]
    kks, g3s = [], []
    for fx, lb in zip(fxs, lbs):
        e = jnp.exp(-jnp.abs(fx))
        r = 1.0 / (1.0 + e)
        pos = fx >= 0
        f = lb + (1.0 - lb) * jnp.where(pos, r, e * r)
        kks.append((1.0 - lb) * jnp.where(pos, e * r, r))
        g3s.append(jnp.concatenate(_split2(jnp.log(f)), axis=1))
    cs = {(d, j): jnp.dot(masks[d][0], rows(g3s[d], j), preferred_element_type=F32) for d, j in items}
    row_sub = lax.broadcasted_iota(jnp.int32, (c, 1), 0) // REC_SUB
    band = lambda a, p: a[p * REC_SUB:(p + 1) * REC_SUB]
    subs = range(REC_NSUB)
    qcat, kcat, qtil, kdec, dec = {}, {}, {}, {}, {}
    for d, j in items:
        x = cs[d, j]
        x = x[:, 0:LANES] + x[:, LANES:2 * LANES]
        bcum = x
        q, kk = rows(qs, j), rows(kks[d], j)
        if d == 0:
            ends = [bcum[(p + 1) * REC_SUB - 1:(p + 1) * REC_SUB] for p in subs]
            starts = [jnp.zeros((1, LANES), F32)] + ends[:-1]
            blast = ends[-1]
        else:
            ends = [bcum[p * REC_SUB:p * REC_SUB + 1] for p in subs]
            starts = ends[1:] + [jnp.zeros((1, LANES), F32)]
            blast = ends[0]
        ends_mat = jnp.concatenate(ends, axis=0)
        per_row = lambda bounds: jnp.concatenate([jnp.broadcast_to(b_, (REC_SUB, LANES)) for b_ in bounds], axis=0)
        qhat = q * jnp.exp(bcum - per_row(starts))
        bend = per_row(ends)
        ke = kk * jnp.exp(bend - bcum)
        qcat[d, j] = jnp.concatenate([jnp.where(row_sub == p, qhat, 0.0) for p in subs], axis=1).astype(BF16)
        grow = [jnp.exp(jnp.minimum(starts[p] - ends_mat, REC_EXP_CLAMP)) for p in subs]
        kcat[d, j] = jnp.concatenate(
            [jnp.concatenate([band(ke, s) * grow[p][s:s + 1] for s in subs], axis=0) for p in subs],
            axis=1).astype(BF16)
        from_start = jnp.exp(jnp.concatenate(starts, axis=0))
        to_last = jnp.exp(blast - ends_mat)
        qtil[d, j] = jnp.concatenate([band(qhat, p) * from_start[p:p + 1] for p in subs], axis=0).astype(BF16)
        kdec[d, j] = jnp.concatenate([band(ke, p) * to_last[p:p + 1] for p in subs], axis=0).astype(BF16)
        dec[d, j] = jnp.exp(blast)
    sc = {k: lax.dot_general(qcat[k], kcat[k], NT_DIMS, preferred_element_type=F32) for k in items}
    u = {(d, j): lax.dot_general(rows(vb, j), kdec[d, j], TN_DIMS, preferred_element_type=F32)
         for d, j in items}
    scb = {(d, j): jnp.where(masks[d][1], sc[d, j], 0.0).astype(BF16) for d, j in items}
    o_intra = {(d, j): jnp.dot(scb[d, j], rows(vb, j), preferred_element_type=F32) for d, j in items}
    return {k: (o_intra[k], qtil[k], u[k], dec[k]) for k in items}


def _rec_kernel(q_ref, ff_ref, fb_ref, v_ref, g_ref, qc_ref, ffc_ref, fbc_ref, vc_ref, gc_ref,
                lb_ref, gn_ref, y_ref, yc_ref, o_scr, qt_scr, u_scr, dec_scr, *, layer_idx):
    t, l = q_ref.shape[1], qc_ref.shape[1]
    c, grp = REC_CHUNK, REC_GROUP

    def lower_bound(d):
        lg = lb_ref[d]
        ex = jnp.exp(lg - jnp.max(lg, axis=0, keepdims=True))
        return jnp.sum(ex[1:layer_idx + 1], axis=0, keepdims=True) / jnp.sum(ex, axis=0, keepdims=True)

    lbs = (lower_bound(0), lower_bound(1))

    def local_pass(q_r, f_refs, v_r, n_tok, tok0):
        def body(i, carry):
            r0 = pl.multiple_of(i * (grp * c), grp * c)
            q = q_r[0, pl.ds(r0, grp * c), :]
            qs = q * _sigmoid(q)
            vb = v_r[0, pl.ds(r0, grp * c), :].astype(BF16)
            fxs = [f_r[0, pl.ds(r0, grp * c), :] for f_r in f_refs]
            local = _rec_group_local(qs, fxs, vb, lbs)
            for j in range(grp):
                row = pl.multiple_of(tok0 + r0 + j * c, c)
                ci = tok0 // c + i * grp + j
                for d in range(2):
                    o_intra, qtil, u, dec = local[d, j]
                    o_scr[d, pl.ds(row, c), :] = o_intra
                    qt_scr[d, pl.ds(row, c), :] = qtil
                    u_scr[d, ci] = u
                    dec_scr[d, pl.ds(ci, 1), :] = dec
            return carry
        lax.fori_loop(0, n_tok // (grp * c), body, 0)

    def state_pass(c_lo, n, states):
        def body(i, carry):
            sts = list(carry)
            for k in range(grp):
                step = i * grp + k
                for d, ci in ((0, c_lo + step), (1, c_lo + n - 1 - step)):
                    row = pl.multiple_of(ci * c, c)
                    o_scr[d, pl.ds(row, c), :] += lax.dot_general(
                        qt_scr[d, pl.ds(row, c), :], sts[d].astype(BF16), NT_DIMS, preferred_element_type=F32)
                    sts[d] = sts[d] * dec_scr[d, pl.ds(ci, 1), :] + u_scr[d, ci]
            return tuple(sts)
        return lax.fori_loop(0, n // grp, body, states)

    def finish(tok0, n_tok, gate_ref, out_ref):
        o = o_scr[0, tok0:tok0 + n_tok, :] + o_scr[1, tok0:tok0 + n_tok, :]
        gate = gate_ref[0]
        out_ref[0] = ((_rms(o) * gn_ref[...]) * (gate * _sigmoid(gate))).astype(out_ref.dtype)

    local_pass(qc_ref, (ffc_ref, fbc_ref), vc_ref, l, 0)
    local_pass(q_ref, (ff_ref, fb_ref), v_ref, t, l)
    zero = jnp.zeros((REC_D, REC_D), F32)
    states = state_pass(0, l // c, (zero, zero))
    state_pass(l // c, t // c, states)
    finish(0, l, gc_ref, yc_ref)
    finish(l, t, g_ref, y_ref)


def hgrn2_mixer(p_lat, p_ctx, lb_logits, gnorm_g, layer_idx):
    b, t, _ = p_lat.shape
    l = p_ctx.shape[1]
    assert t % (REC_GROUP * REC_CHUNK) == 0 and l % (REC_GROUP * REC_CHUNK) == 0
    n_chunks = (t + l) // REC_CHUNK
    lat = lambda k: pl.BlockSpec((1, t, REC_D), lambda i, j: (i, 0, k * REC_HEADS + j))
    cx = lambda k: pl.BlockSpec((1, l, REC_D), lambda i, j: (i, 0, k * REC_HEADS + j))
    return pl.pallas_call(
        functools.partial(_rec_kernel, layer_idx=layer_idx),
        grid=(b, REC_HEADS),
        in_specs=[lat(k) for k in range(5)] + [cx(k) for k in range(5)] + [
            pl.BlockSpec((2, DEPTH, REC_D), lambda i, j: (0, 0, j)),
            pl.BlockSpec((1, REC_D), lambda i, j: (0, 0))],
        out_specs=[pl.BlockSpec((1, t, REC_D), lambda i, j: (i, 0, j)),
                   pl.BlockSpec((1, l, REC_D), lambda i, j: (i, 0, j))],
        out_shape=[jax.ShapeDtypeStruct((b, t, D_MODEL), BF16), jax.ShapeDtypeStruct((b, l, D_MODEL), BF16)],
        scratch_shapes=[pltpu.VMEM((2, t + l, REC_D), F32), pltpu.VMEM((2, t + l, REC_D), BF16),
                        pltpu.VMEM((2, n_chunks, REC_D, REC_D), F32), pltpu.VMEM((2, n_chunks, REC_D), F32)],
        compiler_params=_cparams(("parallel", "parallel"), 40),
        name="hgrn2_mixer",
    )(*([p_lat] * 5), *([p_ctx] * 5), lb_logits, gnorm_g.reshape(1, REC_D))


def _prefix_count(flags):
    e, t = flags.shape
    blk = min(PREFIX_BLOCK, t)
    ones = jnp.where(flags, 1.0, 0.0)
    s = lax.broadcasted_iota(jnp.int32, (blk, blk), 0)
    u = lax.broadcasted_iota(jnp.int32, (blk, blk), 1)
    upper = jnp.where(s < u, 1.0, 0.0).astype(BF16)
    carry = jnp.zeros((e, 1), F32)
    outs = []
    for i in range(t // blk):
        xs = ones[:, i * blk:(i + 1) * blk]
        outs.append(jnp.dot(xs.astype(BF16), upper, preferred_element_type=F32) + carry)
        carry = carry + jnp.sum(xs, axis=1, keepdims=True)
    return jnp.concatenate(outs, axis=1)


def _route_kernel(x_ref, g_ref, sc_ref, sh_ref, wr_ref, xsel_ref, slot_ref, gate_ref, hb_scr, slot_scr, *, cap):
    e = pl.program_id(1)
    t = x_ref.shape[1]

    @pl.when(e == 0)
    def _():
        h = _norm_mod(x_ref[0], g_ref[...], sc_ref[0], sh_ref[0])
        hb_scr[...] = h.astype(BF16)
        logits = lax.dot_general(wr_ref[...], h, NT_DIMS, precision=lax.Precision.HIGHEST,
                                 preferred_element_type=F32)
        ex = jnp.exp(logits - jnp.max(logits, axis=0, keepdims=True))
        aff = ex / jnp.sum(ex, axis=0, keepdims=True)
        bits = lax.bitcast_convert_type(aff, jnp.int32)
        capf = jnp.float32(cap)
        thr = jnp.zeros((N_EXPERTS, 1), jnp.int32)
        for hi in range(30, -1, -THRESHOLD_RADIX_BITS):
            lo = max(hi - THRESHOLD_RADIX_BITS + 1, 0)
            cands = [thr | jnp.int32(m << lo) for m in range(1, 1 << (hi - lo + 1))]
            cnts = [jnp.sum(jnp.where(bits >= cand, 1.0, 0.0), axis=1, keepdims=True) for cand in cands]
            for cand, cnt in zip(cands, cnts):
                thr = jnp.where(cnt >= capf, cand, thr)
        gt = bits > thr
        eq = bits == thr
        need = capf - jnp.sum(jnp.where(gt, 1.0, 0.0), axis=1, keepdims=True)
        sel = gt | (eq & (_prefix_count(eq) < need))
        slot = jnp.where(sel, _prefix_count(sel), -1.0)
        slot_scr[...] = slot
        slot_ref[0] = slot
        gate_ref[0] = aff

    srow = slot_scr[pl.ds(e, 1), :]
    onehot = lax.broadcasted_iota(jnp.int32, (cap, t), 0).astype(F32) == srow
    xsel_ref[0] = jnp.dot(jnp.where(onehot, 1.0, 0.0).astype(BF16), hb_scr[...],
                          preferred_element_type=F32).astype(xsel_ref.dtype)


def route_gather(x, g, sc, sh, w_router):
    b, t, d = x.shape
    cap = EC_CAPACITY_FACTOR * t // N_EXPERTS
    return pl.pallas_call(
        functools.partial(_route_kernel, cap=cap),
        grid=(b, N_EXPERTS),
        in_specs=[
            pl.BlockSpec((1, t, d), lambda i, e: (i, 0, 0)),
            pl.BlockSpec((1, d), lambda i, e: (0, 0)),
            pl.BlockSpec((1, 1, d), lambda i, e: (i, 0, 0)),
            pl.BlockSpec((1, 1, d), lambda i, e: (i, 0, 0)),
            pl.BlockSpec((N_EXPERTS, d), lambda i, e: (0, 0)),
        ],
        out_specs=[pl.BlockSpec((1, cap, d), lambda i, e: (e, i, 0)),
                   pl.BlockSpec((1, N_EXPERTS, t), lambda i, e: (i, 0, 0)),
                   pl.BlockSpec((1, N_EXPERTS, t), lambda i, e: (i, 0, 0))],
        out_shape=[jax.ShapeDtypeStruct((N_EXPERTS, b * cap, d), BF16),
                   jax.ShapeDtypeStruct((b, N_EXPERTS, t), F32),
                   jax.ShapeDtypeStruct((b, N_EXPERTS, t), F32)],
        scratch_shapes=[pltpu.VMEM((t, d), BF16), pltpu.VMEM((N_EXPERTS, t), F32)],
        compiler_params=_cparams(("parallel", "arbitrary"), 48),
        name="route_gather",
    )(x, g.reshape(1, d), sc, sh, w_router.T)


def _ffn_kernel(*refs, n_streams):
    x_refs = refs[:n_streams]
    wg_ref, wu_ref, wd_ref = refs[n_streams:n_streams + 3]
    y_refs = refs[n_streams + 3:2 * n_streams + 3]
    acc_refs = refs[2 * n_streams + 3:]
    f = pl.program_id(2)
    wg, wu, wd = wg_ref[0, 0].astype(BF16), wu_ref[0, 0].astype(BF16), wd_ref[0, 0].astype(BF16)

    @pl.when(f == 0)
    def _():
        for acc_ref in acc_refs:
            acc_ref[...] = jnp.zeros_like(acc_ref)

    xs = [x_ref[0] for x_ref in x_refs]
    gates = [jnp.dot(x, wg, preferred_element_type=F32) for x in xs]
    ups = [jnp.dot(x, wu, preferred_element_type=F32) for x in xs]
    hids = [((a * _sigmoid(a)) * u).astype(BF16) for a, u in zip(gates, ups)]
    for acc_ref, hid in zip(acc_refs, hids):
        acc_ref[...] += jnp.dot(hid, wd, preferred_element_type=F32)

    @pl.when(f == pl.num_programs(2) - 1)
    def _():
        for y_ref, acc_ref in zip(y_refs, acc_refs):
            y_ref[0] = acc_ref[...].astype(y_ref.dtype)


def expert_ffn(xsels, w_gate, w_up, w_down, layer_idx):
    d, ff = w_gate.shape[2], w_gate.shape[3]
    tf, n_split = 512, 4
    n_streams = len(xsels)
    row_blocks = [x.shape[1] // n_split for x in xsels]
    xspec = lambda rb: pl.BlockSpec((1, rb, d), lambda e, h, f: (e, h, 0))
    return pl.pallas_call(
        functools.partial(_ffn_kernel, n_streams=n_streams),
        grid=(N_EXPERTS, n_split, ff // tf),
        in_specs=[xspec(rb) for rb in row_blocks] + [
            pl.BlockSpec((1, 1, d, tf), lambda e, h, f: (layer_idx, e, 0, f)),
            pl.BlockSpec((1, 1, d, tf), lambda e, h, f: (layer_idx, e, 0, f)),
            pl.BlockSpec((1, 1, tf, d), lambda e, h, f: (layer_idx, e, f, 0))],
        out_specs=[xspec(rb) for rb in row_blocks],
        out_shape=[jax.ShapeDtypeStruct(x.shape, BF16) for x in xsels],
        scratch_shapes=[pltpu.VMEM((rb, d), F32) for rb in row_blocks],
        compiler_params=_cparams(("parallel", "parallel", "arbitrary"), 56),
        name="expert_ffn",
    )(*xsels, w_gate, w_up, w_down)


def _combine_kernel(y_ref, slot_ref, gate_ref, x_ref, g2_ref, fg_ref, out_ref, *, final_norm):
    n_e, cap, d = y_ref.shape
    tt = x_ref.shape[1]
    slot, gate = slot_ref[0], gate_ref[0]
    row = lax.broadcasted_iota(jnp.int32, (cap, tt), 0).astype(F32)
    pg = jnp.concatenate([jnp.where(row == slot[e:e + 1], gate[e:e + 1], 0.0).astype(BF16) for e in range(n_e)],
                         axis=0)
    moe = lax.dot_general(pg, y_ref[...].reshape(n_e * cap, d), TN_DIMS, preferred_element_type=F32)
    xn = x_ref[0] + g2_ref[0] * moe
    if final_norm:
        xn = _rms(xn) * fg_ref[...]
    out_ref[0] = xn


def combine_residual(y_sel, slot, gate, x, g2, final_g=None):
    b, t, d = x.shape
    cap = EC_CAPACITY_FACTOR * t // N_EXPERTS
    tt = min(t, 512)
    final_norm = final_g is not None
    fg = (final_g if final_norm else jnp.ones((d,), F32)).reshape(1, d)
    return pl.pallas_call(
        functools.partial(_combine_kernel, final_norm=final_norm),
        grid=(b, t // tt),
        in_specs=[
            pl.BlockSpec((N_EXPERTS, cap, d), lambda i, j: (0, i, 0)),
            pl.BlockSpec((1, N_EXPERTS, tt), lambda i, j: (i, 0, j)),
            pl.BlockSpec((1, N_EXPERTS, tt), lambda i, j: (i, 0, j)),
            pl.BlockSpec((1, tt, d), lambda i, j: (i, j, 0)),
            pl.BlockSpec((1, 1, d), lambda i, j: (i, 0, 0)),
            pl.BlockSpec((1, d), lambda i, j: (0, 0)),
        ],
        out_specs=pl.BlockSpec((1, tt, d), lambda i, j: (i, j, 0)),
        out_shape=jax.ShapeDtypeStruct((b, t, d), F32),
        compiler_params=_cparams(("parallel", "parallel"), 48),
        name="combine_residual",
    )(y_sel, slot, gate, x, g2, fg)


def kernel(x, c, ctx, c_ctx, w_mod, b_mod, norm_g, att_w_in, att_w_out, na_rpb, diff_lambda, diff_subln_g,
           rec_w_in, rec_w_out, rec_lb_logits, rec_gnorm_g, moe_router, moe_w_gate, moe_w_up, moe_w_down,
           final_g):
    b, t, d = x.shape
    l_ctx = ctx.shape[1]
    n_c = b + 1
    r_pad = -(-n_c // 8) * 8
    c_all = jnp.concatenate([c, c_ctx[None, :], jnp.zeros((r_pad - n_c, d), F32)], axis=0)
    mods = modulation_all(c_all, w_mod, b_mod)
    cos, sin = rope_tables(t)

    for l in range(DEPTH):
        last = l == DEPTH - 1
        lat_m = [mods[l, :b, i * d:(i + 1) * d][:, None, :] for i in range(6)]
        ctx_m = [jnp.broadcast_to(mods[l, b, i * d:(i + 1) * d][None, None, :], (b, 1, d)) for i in range(6)]
        sh1, sc1, g1, sh2, sc2, g2 = lat_m
        csh1, csc1, cg1, csh2, csc2, cg2 = ctx_m

        if l % 2 == 0:
            e = l // 2
            w_in = att_w_in[e].astype(BF16)
            w_out = att_w_out[e].astype(BF16)
            p_lat = norm_mod_proj(x, norm_g[l, 0], sc1, sh1, w_in, BF16)
            p_ctx = norm_mod_proj(ctx, norm_g[l, 0], csc1, csh1, w_in, BF16)
            bias = na_bias_tables(na_rpb[e], t // GRID_W, l_ctx, LOG2E)
            oa_lat, oa_ctx = na_attention(p_lat, p_ctx, bias)
            ob_lat, ob_ctx = diff_attention(p_lat, p_ctx, cos, sin, diff_lambda[e], diff_subln_g[e], l)
            ws = [w_out[:A_W], w_out[A_W:]]
            x = outproj_residual([oa_lat, ob_lat], ws, x, g1)
            if not last:
                ctx = outproj_residual([oa_ctx, ob_ctx], ws, ctx, cg1)
        else:
            o = l // 2
            w_in = rec_w_in[o].astype(BF16)
            w_out = rec_w_out[o].astype(BF16)
            p_lat = norm_mod_proj(x, norm_g[l, 0], sc1, sh1, w_in, F32)
            p_ctx = norm_mod_proj(ctx, norm_g[l, 0], csc1, csh1, w_in, F32)
            y_lat, y_ctx = hgrn2_mixer(p_lat, p_ctx, rec_lb_logits, rec_gnorm_g[o], l)
            x = outproj_residual([y_lat], [w_out], x, g1)
            if not last:
                ctx = outproj_residual([y_ctx], [w_out], ctx, cg1)

        xsel_l, slot_l, gate_l = route_gather(x, norm_g[l, 1], sc2, sh2, moe_router[l])
        if last:
            (y_l,) = expert_ffn([xsel_l], moe_w_gate, moe_w_up, moe_w_down, l)
            x = combine_residual(y_l, slot_l, gate_l, x, g2, final_g)
        else:
            xsel_c, slot_c, gate_c = route_gather(ctx, norm_g[l, 1], csc2, csh2, moe_router[l])
            y_l, y_c = expert_ffn([xsel_l, xsel_c], moe_w_gate, moe_w_up, moe_w_down, l)
            x = combine_residual(y_l, slot_l, gate_l, x, g2)
            ctx = combine_residual(y_c, slot_c, gate_c, ctx, cg2)
    return x
```

```python
import functools
import math

import numpy as np
import jax
import jax.numpy as jnp
from jax import lax
from jax.experimental import pallas as pl
from jax.experimental.pallas import tpu as pltpu

F32 = jnp.float32
BF16 = jnp.bfloat16

D_MODEL = 1024
DEPTH = 4
GRID_W = 64
HEAD_DIM = 64
NA_HEADS = 8
NA_WIN_H = 8
NA_WIN_W = 16
DIFF_HEADS = 4
A_W = NA_HEADS * HEAD_DIM
ATT_IN_W = 3072
REC_HEADS = 8
REC_D = 128
N_EXPERTS = 16
EXPERT_FF = 2 * D_MODEL
EC_CAPACITY_FACTOR = 2
ROPE_THETA = 10000.0
NORM_EPS = 1e-6
NEG_INF = -1e30

LANES = 128
V7X_VMEM_BYTES = 64 * 1024 * 1024
MIB = 1024 * 1024

NA_QROWS = 4
NA_KROWS = NA_QROWS + NA_WIN_H - 1
NA_BLOCKS_PER_STEP = 2
DIFF_BLOCKS_PER_STEP = 2
REC_CHUNK = 64
REC_SUB = 16
REC_NSUB = REC_CHUNK // REC_SUB
REC_EXP_CLAMP = 60.0
REC_GROUP = 4
PREFIX_BLOCK = 256
THRESHOLD_RADIX_BITS = 3
FFN_ROW_TILE = 512
FFN_HALVES = 2

LOG2E = math.log2(math.e)
SCORE_MULT = HEAD_DIM ** -0.5 * LOG2E

NT_DIMS = (((1,), (1,)), ((), ()))
TN_DIMS = (((0,), (0,)), ((), ()))


def _cparams(semantics, vmem_mib):
    assert vmem_mib * MIB < V7X_VMEM_BYTES
    return pltpu.CompilerParams(dimension_semantics=semantics, vmem_limit_bytes=vmem_mib * MIB)


def _sigmoid(x):
    return 1.0 / (1.0 + jnp.exp(-x))


def _rms(x, eps=NORM_EPS):
    return x * lax.rsqrt(jnp.mean(x * x, axis=-1, keepdims=True) + eps)


def _mod_kernel(c_ref, w_ref, b_ref, o_ref):
    c = c_ref[...]
    s = (c * _sigmoid(c)).astype(BF16)
    o_ref[0] = jnp.dot(s, w_ref[0].astype(BF16), preferred_element_type=F32) + b_ref[0]


def modulation_all(c_all, w_mod, b_mod):
    depth, d, n = w_mod.shape
    r = c_all.shape[0]
    tn = 1536
    return pl.pallas_call(
        _mod_kernel,
        grid=(depth, n // tn),
        in_specs=[
            pl.BlockSpec((r, d), lambda l, j: (0, 0)),
            pl.BlockSpec((1, d, tn), lambda l, j: (l, 0, j)),
            pl.BlockSpec((1, 1, tn), lambda l, j: (l, 0, j)),
        ],
        out_specs=pl.BlockSpec((1, r, tn), lambda l, j: (l, 0, j)),
        out_shape=jax.ShapeDtypeStruct((depth, r, n), F32),
        compiler_params=_cparams(("parallel", "parallel"), 32),
        name="modulation",
    )(c_all, w_mod, b_mod.reshape(depth, 1, n))


def _norm_mod(x, g, sc, sh):
    return _rms(x) * g * (1.0 + sc) + sh


def _proj_kernel(x_ref, g_ref, sc_ref, sh_ref, w_ref, o_ref, h_scr, *, tn):
    h_scr[...] = _norm_mod(x_ref[0], g_ref[...], sc_ref[0], sh_ref[0]).astype(BF16)
    n = w_ref.shape[1]
    for j in range(n // tn):
        o_ref[0, :, j * tn:(j + 1) * tn] = jnp.dot(
            h_scr[...], w_ref[:, j * tn:(j + 1) * tn], preferred_element_type=F32).astype(o_ref.dtype)


def norm_mod_proj(x, g, sc, sh, w_bf16, out_dtype):
    b, t, d = x.shape
    n = w_bf16.shape[1]
    tm = min(t, 256)
    return pl.pallas_call(
        functools.partial(_proj_kernel, tn=512),
        grid=(b, t // tm),
        in_specs=[
            pl.BlockSpec((1, tm, d), lambda i, j: (i, j, 0)),
            pl.BlockSpec((1, d), lambda i, j: (0, 0)),
            pl.BlockSpec((1, 1, d), lambda i, j: (i, 0, 0)),
            pl.BlockSpec((1, 1, d), lambda i, j: (i, 0, 0)),
            pl.BlockSpec((d, n), lambda i, j: (0, 0)),
        ],
        out_specs=pl.BlockSpec((1, tm, n), lambda i, j: (i, j, 0)),
        out_shape=jax.ShapeDtypeStruct((b, t, n), out_dtype),
        scratch_shapes=[pltpu.VMEM((tm, d), BF16)],
        compiler_params=_cparams(("parallel", "parallel"), 48),
        name="norm_mod_proj",
    )(x, g.reshape(1, d), sc, sh, w_bf16)


def _outproj_kernel(*refs, n_in):
    o_refs, w_refs = refs[:n_in], refs[n_in:2 * n_in]
    x_ref, gate_ref, out_ref = refs[2 * n_in:]
    acc = jnp.dot(o_refs[0][0], w_refs[0][...], preferred_element_type=F32)
    for o_ref, w_ref in zip(o_refs[1:], w_refs[1:]):
        acc = acc + jnp.dot(o_ref[0], w_ref[...], preferred_element_type=F32)
    out_ref[0] = x_ref[0] + gate_ref[0] * acc


def outproj_residual(os_, ws, x, gate):
    b, t, d = x.shape
    tm = min(t, 512)
    n_in = len(os_)
    in_specs = [pl.BlockSpec((1, tm, o.shape[2]), lambda i, j: (i, j, 0)) for o in os_]
    in_specs += [pl.BlockSpec(w.shape, lambda i, j: (0, 0)) for w in ws]
    in_specs += [pl.BlockSpec((1, tm, d), lambda i, j: (i, j, 0)),
                 pl.BlockSpec((1, 1, d), lambda i, j: (i, 0, 0))]
    return pl.pallas_call(
        functools.partial(_outproj_kernel, n_in=n_in),
        grid=(b, t // tm),
        in_specs=in_specs,
        out_specs=pl.BlockSpec((1, tm, d), lambda i, j: (i, j, 0)),
        out_shape=jax.ShapeDtypeStruct((b, t, d), F32),
        compiler_params=_cparams(("parallel", "parallel"), 32),
        name="outproj_residual",
    )(*os_, *ws, x, gate)


def _na_block_tables(rows):
    kh = min(NA_WIN_H, rows)
    assert kh == NA_WIN_H and rows % NA_QROWS == 0 and rows >= NA_KROWS
    n_qb = rows // NA_QROWS
    starts = [min(max(qb * NA_QROWS - kh // 2, 0), rows - NA_KROWS) for qb in range(n_qb)]
    tbl_of, reps = [], []
    for qb in range(n_qb):
        r0 = qb * NA_QROWS
        interior = (r0 - kh // 2 >= 0) and (r0 + NA_QROWS - 1 - kh // 2 <= rows - kh) and starts[qb] == r0 - kh // 2
        key = "interior" if interior else qb
        if key not in reps:
            reps.append(key)
        tbl_of.append(reps.index(key))
    rep_qb = [next(qb for qb in range(n_qb) if tbl_of[qb] == i) for i in range(len(reps))]
    tiles = []
    for qb in rep_qb:
        tiles.append([])
        for qi in range(NA_QROWS):
            r = qb * NA_QROWS + qi
            rs = min(max(r - kh // 2, 0), rows - kh)
            tiles[-1].append([(starts[qb] + kk - r + NA_WIN_H - 1) if rs <= starts[qb] + kk < rs + kh else None
                              for kk in range(NA_KROWS)])
    return starts, tbl_of, tiles


def na_bias_tables(rpb, rows, ctx_len, mult):
    _, _, tiles = _na_block_tables(rows)
    c = np.arange(GRID_W)
    cs = np.clip(c - NA_WIN_W // 2, 0, GRID_W - NA_WIN_W)
    col_ok = (c[None, :] >= cs[:, None]) & (c[None, :] < cs[:, None] + NA_WIN_W)
    dcol = c[None, :] - c[:, None] + NA_WIN_W - 1
    pick = ((dcol[None] == np.arange(2 * NA_WIN_W - 1)[:, None, None]) & col_ok[None]).astype(np.float32)
    toe = jnp.einsum("hrd,dqk->hrqk", rpb.astype(F32), jnp.asarray(pick), precision=lax.Precision.HIGHEST)
    toe = jnp.where(jnp.asarray(col_ok), toe * mult, NEG_INF)
    neg = jnp.full((rpb.shape[0], GRID_W, GRID_W), NEG_INF, F32)
    tables = [jnp.concatenate([jnp.concatenate([neg if d is None else toe[:, d] for d in row], axis=-1)
                               for row in tbl], axis=-2) for tbl in tiles]
    loc = jnp.stack(tables, axis=1)
    return jnp.concatenate([loc, jnp.zeros(loc.shape[:3] + (ctx_len,), F32)], axis=-1)


def _attend_many(qs, ks, vs, biases=None):
    s2 = [lax.dot_general(q, k, NT_DIMS, preferred_element_type=F32) for q, k in zip(qs, ks)]
    if biases is not None:
        s2 = [s + b for s, b in zip(s2, biases)]
    ps = [jnp.exp2(s - jnp.max(s, axis=-1, keepdims=True)) for s in s2]
    inv = [1.0 / jnp.sum(p, axis=-1, keepdims=True) for p in ps]
    return [jnp.dot(p.astype(BF16), v, preferred_element_type=F32) * r for p, v, r in zip(ps, vs, inv)]


def _head_half_scales(mult):
    first = lax.broadcasted_iota(jnp.int32, (1, LANES), 1) < HEAD_DIM
    return first, jnp.where(first, mult, 0.0), jnp.where(first, 0.0, mult)


def _na_kernel(q_ref, k_ref, v_ref, qc_ref, kc_ref, vc_ref, bias_ref, o_ref, oc_ref, *, starts, tbl_of):
    first, *halves = _head_half_scales(SCORE_MULT)
    nq, nk = NA_QROWS * GRID_W, NA_KROWS * GRID_W
    kc, vc = kc_ref[0], vc_ref[0]
    n_qb = len(starts)
    assert n_qb % NA_BLOCKS_PER_STEP == 0
    for qb0 in range(0, n_qb, NA_BLOCKS_PER_STEP):
        blocks = range(qb0, qb0 + NA_BLOCKS_PER_STEP)
        qs, ks, vs, bs = [], [], [], []
        for qb in blocks:
            start, tbl = starts[qb], tbl_of[qb]
            q = q_ref[0, qb * nq:(qb + 1) * nq, :].astype(F32)
            k_all = jnp.concatenate([k_ref[0, start * GRID_W:start * GRID_W + nk, :], kc], axis=0)
            v_all = jnp.concatenate([v_ref[0, start * GRID_W:start * GRID_W + nk, :], vc], axis=0)
            for hh in range(2):
                qs.append((q * halves[hh]).astype(BF16))
                ks.append(k_all)
                vs.append(v_all)
                bs.append(bias_ref[hh, tbl])
        outs = _attend_many(qs, ks, vs, bs)
        for i, qb in enumerate(blocks):
            o_ref[0, qb * nq:(qb + 1) * nq, :] = jnp.where(first, outs[2 * i], outs[2 * i + 1]).astype(o_ref.dtype)
    qc = qc_ref[0].astype(F32)
    outs = _attend_many([(qc * h).astype(BF16) for h in halves], [kc, kc], [vc, vc])
    oc_ref[0] = jnp.where(first, outs[0], outs[1]).astype(oc_ref.dtype)


def na_attention(p_lat, p_ctx, bias):
    b, t, _ = p_lat.shape
    l = p_ctx.shape[1]
    rows = t // GRID_W
    starts, tbl_of, _ = _na_block_tables(rows)
    n_pairs = NA_HEADS // 2
    kb, vb = 2 * A_W // LANES, 3 * A_W // LANES
    lat = lambda off: pl.BlockSpec((1, t, LANES), lambda j, i: (i, 0, off + j))
    cx = lambda off: pl.BlockSpec((1, l, LANES), lambda j, i: (i, 0, off + j))
    n_tbl, nq, nkc = bias.shape[1:]
    return pl.pallas_call(
        functools.partial(_na_kernel, starts=tuple(starts), tbl_of=tuple(tbl_of)),
        grid=(n_pairs, b),
        in_specs=[lat(0), lat(kb), lat(vb), cx(0), cx(kb), cx(vb),
                  pl.BlockSpec((2, n_tbl, nq, nkc), lambda j, i: (j, 0, 0, 0))],
        out_specs=[pl.BlockSpec((1, t, LANES), lambda j, i: (i, 0, j)),
                   pl.BlockSpec((1, l, LANES), lambda j, i: (i, 0, j))],
        out_shape=[jax.ShapeDtypeStruct((b, t, A_W), BF16), jax.ShapeDtypeStruct((b, l, A_W), BF16)],
        compiler_params=_cparams(("parallel", "parallel"), 48),
        name="na_attention",
    )(p_lat, p_lat, p_lat, p_ctx, p_ctx, p_ctx, bias)


def rope_tables(n):
    t = np.arange(n)
    rows, cols = (t // GRID_W).astype(np.float64), (t % GRID_W).astype(np.float64)
    n_freq = HEAD_DIM // 4
    inv = ROPE_THETA ** (-np.arange(n_freq, dtype=np.float64) / n_freq)
    ang = np.concatenate([rows[:, None] * inv, cols[:, None] * inv], axis=-1)
    cos = np.tile(np.cos(ang), (1, 4))
    sin = np.tile(np.concatenate([-np.sin(ang), np.sin(ang)], axis=-1), (1, 2))
    return jnp.asarray(cos, F32), jnp.asarray(sin, F32)


def _rope(a, cos, sin_signed):
    lane = lax.broadcasted_iota(jnp.int32, (1, LANES), 1)
    low = (lane % HEAD_DIM) < HEAD_DIM // 2
    partner = jnp.where(low, pltpu.roll(a, LANES - HEAD_DIM // 2, 1), pltpu.roll(a, HEAD_DIM // 2, 1))
    return a * cos + partner * sin_signed


def _diff_kernel(q_ref, k_ref, v_ref, qc_ref, kc_ref, vc_ref, cos_ref, sin_ref, lam_ref, g_ref,
                 o_ref, oc_ref, q1_scr, q2_scr, k_scr, v_scr, *, lam_init, tq):
    t, l = q_ref.shape[1], kc_ref.shape[1]
    _, half1, half2 = _head_half_scales(SCORE_MULT)
    lv = lam_ref[...]
    lam = (jnp.exp(jnp.sum(lv[0:1] * lv[1:2], axis=-1, keepdims=True))
           - jnp.exp(jnp.sum(lv[2:3] * lv[3:4], axis=-1, keepdims=True)) + lam_init)
    cos, sin = cos_ref[...], sin_ref[...]
    q = _rope(q_ref[0].astype(F32), cos, sin)
    q1_scr[...] = (q * half1).astype(BF16)
    q2_scr[...] = (q * half2).astype(BF16)
    k_scr[0:t, :] = _rope(k_ref[0].astype(F32), cos, sin).astype(BF16)
    k_scr[t:t + l, :] = kc_ref[0]
    v_scr[0:t, :] = v_ref[0]
    v_scr[t:t + l, :] = vc_ref[0]
    gain = g_ref[...] * (1.0 - lam_init)

    def attend(q_pairs, k, v):
        flat = [q for pair in q_pairs for q in pair]
        heads = _attend_many(flat, [k] * len(flat), [v] * len(flat))
        return [_rms(heads[2 * i] - lam * heads[2 * i + 1]) * gain for i in range(len(q_pairs))]

    def body(i, carry):
        rs = [pl.multiple_of((i * DIFF_BLOCKS_PER_STEP + j) * tq, tq) for j in range(DIFF_BLOCKS_PER_STEP)]
        outs = attend([(q1_scr[pl.ds(r, tq), :], q2_scr[pl.ds(r, tq), :]) for r in rs], k_scr[...], v_scr[...])
        for r, o in zip(rs, outs):
            o_ref[0, pl.ds(r, tq), :] = o.astype(o_ref.dtype)
        return carry

    lax.fori_loop(0, t // (tq * DIFF_BLOCKS_PER_STEP), body, 0)
    qc = qc_ref[0].astype(F32)
    (oc,) = attend([((qc * half1).astype(BF16), (qc * half2).astype(BF16))], kc_ref[0], vc_ref[0])
    oc_ref[0] = oc.astype(oc_ref.dtype)


def diff_attention(p_lat, p_ctx, cos, sin, lam_vecs, subln_g, layer_idx):
    b, t, _ = p_lat.shape
    l = p_ctx.shape[1]
    lam_init = 0.8 - 0.6 * math.exp(-0.3 * layer_idx)
    qb, kb, vb = A_W // LANES, 4 * A_W // LANES, 5 * A_W // LANES
    lat = lambda off: pl.BlockSpec((1, t, LANES), lambda i, j: (i, 0, off + j))
    cx = lambda off: pl.BlockSpec((1, l, LANES), lambda i, j: (i, 0, off + j))
    const = lambda shape: pl.BlockSpec(shape, lambda i, j: (0, 0))
    tq = 256
    assert t % (tq * DIFF_BLOCKS_PER_STEP) == 0
    return pl.pallas_call(
        functools.partial(_diff_kernel, lam_init=lam_init, tq=tq),
        grid=(b, DIFF_HEADS),
        in_specs=[lat(qb), lat(kb), lat(vb), cx(qb), cx(kb), cx(vb),
                  const((t, LANES)), const((t, LANES)), const(lam_vecs.shape), const((1, LANES))],
        out_specs=[pl.BlockSpec((1, t, LANES), lambda i, j: (i, 0, j)),
                   pl.BlockSpec((1, l, LANES), lambda i, j: (i, 0, j))],
        out_shape=[jax.ShapeDtypeStruct((b, t, A_W), BF16), jax.ShapeDtypeStruct((b, l, A_W), BF16)],
        scratch_shapes=[pltpu.VMEM((t, LANES), BF16), pltpu.VMEM((t, LANES), BF16),
                        pltpu.VMEM((t + l, LANES), BF16), pltpu.VMEM((t + l, LANES), BF16)],
        compiler_params=_cparams(("parallel", "parallel"), 48),
        name="diff_attention",
    )(p_lat, p_lat, p_lat, p_ctx, p_ctx, p_ctx, cos, sin, lam_vecs, subln_g.reshape(1, LANES))


def _split2(x):
    a = x.astype(BF16)
    return a, (x - a.astype(F32)).astype(BF16)


def _rec_masks(forward):
    t = lax.broadcasted_iota(jnp.int32, (REC_CHUNK, REC_CHUNK), 0)
    s = lax.broadcasted_iota(jnp.int32, (REC_CHUNK, REC_CHUNK), 1)
    incl = (s <= t) if forward else (s >= t)
    return jnp.where(incl, 1.0, 0.0).astype(BF16), incl


def _rec_group_local(qs, fxs, vb, lbs):
    c = REC_CHUNK
    items = [(d, j) for j in range(REC_GROUP) for d in range(2)]
    rows = lambda a, j: a[j * c:(j + 1) * c]
    masks = [_rec_masks(True), _rec_masks(False)]
    kks, g3s = [], []
    for fx, lb in zip(fxs, lbs):
        e = jnp.exp(-jnp.abs(fx))
        r = 1.0 / (1.0 + e)
        pos = fx >= 0
        f = lb + (1.0 - lb) * jnp.where(pos, r, e * r)
        kks.append((1.0 - lb) * jnp.where(pos, e * r, r))
        g3s.append(jnp.concatenate(_split2(jnp.log(f)), axis=1))
    cs = {(d, j): jnp.dot(masks[d][0], rows(g3s[d], j), preferred_element_type=F32) for d, j in items}
    row_sub = lax.broadcasted_iota(jnp.int32, (c, 1), 0) // REC_SUB
    band = lambda a, p: a[p * REC_SUB:(p + 1) * REC_SUB]
    subs = range(REC_NSUB)
    qcat, kcat, qtil, kdec, dec = {}, {}, {}, {}, {}
    for d, j in items:
        x = cs[d, j]
        x = x[:, 0:LANES] + x[:, LANES:2 * LANES]
        bcum = x
        q, kk = rows(qs, j), rows(kks[d], j)
        if d == 0:
            ends = [bcum[(p + 1) * REC_SUB - 1:(p + 1) * REC_SUB] for p in subs]
            starts = [jnp.zeros((1, LANES), F32)] + ends[:-1]
            blast = ends[-1]
        else:
            ends = [bcum[p * REC_SUB:p * REC_SUB + 1] for p in subs]
            starts = ends[1:] + [jnp.zeros((1, LANES), F32)]
            blast = ends[0]
        ends_mat = jnp.concatenate(ends, axis=0)
        per_row = lambda bounds: jnp.concatenate([jnp.broadcast_to(b_, (REC_SUB, LANES)) for b_ in bounds], axis=0)
        qhat = q * jnp.exp(bcum - per_row(starts))
        bend = per_row(ends)
        ke = kk * jnp.exp(bend - bcum)
        qcat[d, j] = jnp.concatenate([jnp.where(row_sub == p, qhat, 0.0) for p in subs], axis=1).astype(BF16)
        grow = [jnp.exp(jnp.minimum(starts[p] - ends_mat, REC_EXP_CLAMP)) for p in subs]
        kcat[d, j] = jnp.concatenate(
            [jnp.concatenate([band(ke, s) * grow[p][s:s + 1] for s in subs], axis=0) for p in subs],
            axis=1).astype(BF16)
        from_start = jnp.exp(jnp.concatenate(starts, axis=0))
        to_last = jnp.exp(blast - ends_mat)
        qtil[d, j] = jnp.concatenate([band(qhat, p) * from_start[p:p + 1] for p in subs], axis=0).astype(BF16)
        kdec[d, j] = jnp.concatenate([band(ke, p) * to_last[p:p + 1] for p in subs], axis=0).astype(BF16)
        dec[d, j] = jnp.exp(blast)
    sc = {k: lax.dot_general(qcat[k], kcat[k], NT_DIMS, preferred_element_type=F32) for k in items}
    u = {(d, j): lax.dot_general(rows(vb, j), kdec[d, j], TN_DIMS, preferred_element_type=F32)
         for d, j in items}
    scb = {(d, j): jnp.where(masks[d][1], sc[d, j], 0.0).astype(BF16) for d, j in items}
    o_intra = {(d, j): jnp.dot(scb[d, j], rows(vb, j), preferred_element_type=F32) for d, j in items}
    return {k: (o_intra[k], qtil[k], u[k], dec[k]) for k in items}


def _rec_kernel(q_ref, ff_ref, fb_ref, v_ref, g_ref, qc_ref, ffc_ref, fbc_ref, vc_ref, gc_ref,
                lb_ref, gn_ref, y_ref, yc_ref, o_scr, qt_scr, u_scr, dec_scr, *, layer_idx):
    t, l = q_ref.shape[1], qc_ref.shape[1]
    c, grp = REC_CHUNK, REC_GROUP

    def lower_bound(d):
        lg = lb_ref[d]
        ex = jnp.exp(lg - jnp.max(lg, axis=0, keepdims=True))
        return jnp.sum(ex[1:layer_idx + 1], axis=0, keepdims=True) / jnp.sum(ex, axis=0, keepdims=True)

    lbs = (lower_bound(0), lower_bound(1))

    def local_pass(q_r, f_refs, v_r, n_tok, tok0):
        def body(i, carry):
            r0 = pl.multiple_of(i * (grp * c), grp * c)
            q = q_r[0, pl.ds(r0, grp * c), :]
            qs = q * _sigmoid(q)
            vb = v_r[0, pl.ds(r0, grp * c), :].astype(BF16)
            fxs = [f_r[0, pl.ds(r0, grp * c), :] for f_r in f_refs]
            local = _rec_group_local(qs, fxs, vb, lbs)
            for j in range(grp):
                row = pl.multiple_of(tok0 + r0 + j * c, c)
                ci = tok0 // c + i * grp + j
                for d in range(2):
                    o_intra, qtil, u, dec = local[d, j]
                    o_scr[d, pl.ds(row, c), :] = o_intra
                    qt_scr[d, pl.ds(row, c), :] = qtil
                    u_scr[d, ci] = u
                    dec_scr[d, pl.ds(ci, 1), :] = dec
            return carry
        lax.fori_loop(0, n_tok // (grp * c), body, 0)

    def state_pass(c_lo, n, states):
        def body(i, carry):
            sts = list(carry)
            for k in range(grp):
                step = i * grp + k
                for d, ci in ((0, c_lo + step), (1, c_lo + n - 1 - step)):
                    row = pl.multiple_of(ci * c, c)
                    o_scr[d, pl.ds(row, c), :] += lax.dot_general(
                        qt_scr[d, pl.ds(row, c), :], sts[d].astype(BF16), NT_DIMS, preferred_element_type=F32)
                    sts[d] = sts[d] * dec_scr[d, pl.ds(ci, 1), :] + u_scr[d, ci]
            return tuple(sts)
        return lax.fori_loop(0, n // grp, body, states)

    def finish(tok0, n_tok, gate_ref, out_ref):
        o = o_scr[0, tok0:tok0 + n_tok, :] + o_scr[1, tok0:tok0 + n_tok, :]
        gate = gate_ref[0]
        out_ref[0] = ((_rms(o) * gn_ref[...]) * (gate * _sigmoid(gate))).astype(out_ref.dtype)

    local_pass(qc_ref, (ffc_ref, fbc_ref), vc_ref, l, 0)
    local_pass(q_ref, (ff_ref, fb_ref), v_ref, t, l)
    zero = jnp.zeros((REC_D, REC_D), F32)
    states = state_pass(0, l // c, (zero, zero))
    state_pass(l // c, t // c, states)
    finish(0, l, gc_ref, yc_ref)
    finish(l, t, g_ref, y_ref)


def hgrn2_mixer(p_lat, p_ctx, lb_logits, gnorm_g, layer_idx):
    b, t, _ = p_lat.shape
    l = p_ctx.shape[1]
    assert t % (REC_GROUP * REC_CHUNK) == 0 and l % (REC_GROUP * REC_CHUNK) == 0
    n_chunks = (t + l) // REC_CHUNK
    lat = lambda k: pl.BlockSpec((1, t, REC_D), lambda i, j: (i, 0, k * REC_HEADS + j))
    cx = lambda k: pl.BlockSpec((1, l, REC_D), lambda i, j: (i, 0, k * REC_HEADS + j))
    return pl.pallas_call(
        functools.partial(_rec_kernel, layer_idx=layer_idx),
        grid=(b, REC_HEADS),
        in_specs=[lat(k) for k in range(5)] + [cx(k) for k in range(5)] + [
            pl.BlockSpec((2, DEPTH, REC_D), lambda i, j: (0, 0, j)),
            pl.BlockSpec((1, REC_D), lambda i, j: (0, 0))],
        out_specs=[pl.BlockSpec((1, t, REC_D), lambda i, j: (i, 0, j)),
                   pl.BlockSpec((1, l, REC_D), lambda i, j: (i, 0, j))],
        out_shape=[jax.ShapeDtypeStruct((b, t, D_MODEL), BF16), jax.ShapeDtypeStruct((b, l, D_MODEL), BF16)],
        scratch_shapes=[pltpu.VMEM((2, t + l, REC_D), F32), pltpu.VMEM((2, t + l, REC_D), BF16),
                        pltpu.VMEM((2, n_chunks, REC_D, REC_D), F32), pltpu.VMEM((2, n_chunks, REC_D), F32)],
        compiler_params=_cparams(("parallel", "parallel"), 40),
        name="hgrn2_mixer",
    )(*([p_lat] * 5), *([p_ctx] * 5), lb_logits, gnorm_g.reshape(1, REC_D))


def _prefix_count(flags):
    e, t = flags.shape
    blk = min(PREFIX_BLOCK, t)
    ones = jnp.where(flags, 1.0, 0.0)
    s = lax.broadcasted_iota(jnp.int32, (blk, blk), 0)
    u = lax.broadcasted_iota(jnp.int32, (blk, blk), 1)
    upper = jnp.where(s < u, 1.0, 0.0).astype(BF16)
    carry = jnp.zeros((e, 1), F32)
    outs = []
    for i in range(t // blk):
        xs = ones[:, i * blk:(i + 1) * blk]
        outs.append(jnp.dot(xs.astype(BF16), upper, preferred_element_type=F32) + carry)
        carry = carry + jnp.sum(xs, axis=1, keepdims=True)
    return jnp.concatenate(outs, axis=1)


def _route_kernel(x_ref, g_ref, sc_ref, sh_ref, wr_ref, xsel_ref, slot_ref, gate_ref, hb_scr, slot_scr, *, cap):
    e = pl.program_id(1)
    t = x_ref.shape[1]

    @pl.when(e == 0)
    def _():
        h = _norm_mod(x_ref[0], g_ref[...], sc_ref[0], sh_ref[0])
        hb_scr[...] = h.astype(BF16)
        logits = lax.dot_general(wr_ref[...], h, NT_DIMS, precision=lax.Precision.HIGHEST,
                                 preferred_element_type=F32)
        ex = jnp.exp(logits - jnp.max(logits, axis=0, keepdims=True))
        aff = ex / jnp.sum(ex, axis=0, keepdims=True)
        bits = lax.bitcast_convert_type(aff, jnp.int32)
        capf = jnp.float32(cap)
        thr = jnp.zeros((N_EXPERTS, 1), jnp.int32)
        for hi in range(30, -1, -THRESHOLD_RADIX_BITS):
            lo = max(hi - THRESHOLD_RADIX_BITS + 1, 0)
            cands = [thr | jnp.int32(m << lo) for m in range(1, 1 << (hi - lo + 1))]
            cnts = [jnp.sum(jnp.where(bits >= cand, 1.0, 0.0), axis=1, keepdims=True) for cand in cands]
            for cand, cnt in zip(cands, cnts):
                thr = jnp.where(cnt >= capf, cand, thr)
        gt = bits > thr
        eq = bits == thr
        need = capf - jnp.sum(jnp.where(gt, 1.0, 0.0), axis=1, keepdims=True)
        sel = gt | (eq & (_prefix_count(eq) < need))
        slot = jnp.where(sel, _prefix_count(sel), -1.0)
        slot_scr[...] = slot
        slot_ref[0] = slot
        gate_ref[0] = aff

    srow = slot_scr[pl.ds(e, 1), :]
    onehot = lax.broadcasted_iota(jnp.int32, (cap, t), 0).astype(F32) == srow
    xsel_ref[0] = jnp.dot(jnp.where(onehot, 1.0, 0.0).astype(BF16), hb_scr[...],
                          preferred_element_type=F32).astype(xsel_ref.dtype)


def route_gather(x, g, sc, sh, w_router):
    b, t, d = x.shape
    cap = EC_CAPACITY_FACTOR * t // N_EXPERTS
    return pl.pallas_call(
        functools.partial(_route_kernel, cap=cap),
        grid=(b, N_EXPERTS),
        in_specs=[
            pl.BlockSpec((1, t, d), lambda i, e: (i, 0, 0)),
            pl.BlockSpec((1, d), lambda i, e: (0, 0)),
            pl.BlockSpec((1, 1, d), lambda i, e: (i, 0, 0)),
            pl.BlockSpec((1, 1, d), lambda i, e: (i, 0, 0)),
            pl.BlockSpec((N_EXPERTS, d), lambda i, e: (0, 0)),
        ],
        out_specs=[pl.BlockSpec((1, cap, d), lambda i, e: (e, i, 0)),
                   pl.BlockSpec((1, N_EXPERTS, t), lambda i, e: (i, 0, 0)),
                   pl.BlockSpec((1, N_EXPERTS, t), lambda i, e: (i, 0, 0))],
        out_shape=[jax.ShapeDtypeStruct((N_EXPERTS, b * cap, d), BF16),
                   jax.ShapeDtypeStruct((b, N_EXPERTS, t), F32),
                   jax.ShapeDtypeStruct((b, N_EXPERTS, t), F32)],
        scratch_shapes=[pltpu.VMEM((t, d), BF16), pltpu.VMEM((N_EXPERTS, t), F32)],
        compiler_params=_cparams(("parallel", "arbitrary"), 48),
        name="route_gather",
    )(x, g.reshape(1, d), sc, sh, w_router.T)


def _ffn_kernel(*refs, n_lat_tiles, has_ctx, layer_idx):
    if has_ctx:
        xl_ref, xc_ref, wg_hbm, wu_hbm, wd_hbm, yl_ref, yc_ref, wg_s, wu_s, wd_s, stage, sem = refs
    else:
        xl_ref, wg_hbm, wu_hbm, wd_hbm, yl_ref, wg_s, wu_s, wd_s, stage, sem = refs
    e, r = pl.program_id(0), pl.program_id(1)
    half = stage.shape[0]
    w_hbm, w_s = (wg_hbm, wu_hbm, wd_hbm), (wg_s, wu_s, wd_s)

    def chunk_copy(expert, c):
        m, h = divmod(c, FFN_HALVES)
        if m < 2:
            src = w_hbm[m].at[layer_idx, expert, :, pl.ds(h * half, half)]
        else:
            src = w_hbm[m].at[layer_idx, expert, pl.ds(h * half, half), :]
        return pltpu.make_async_copy(src, stage, sem.at[0])

    def chunk_store(slot, c):
        m, h = divmod(c, FFN_HALVES)
        if m < 2:
            w_s[m][slot, :, h * half:(h + 1) * half] = stage[...].astype(BF16)
        else:
            w_s[m][slot, h * half:(h + 1) * half, :] = stage[...].astype(BF16)

    n_chunks = 3 * FFN_HALVES

    @pl.when((e == 0) & (r == 0))
    def _():
        for c in range(n_chunks):
            cp = chunk_copy(0, c)
            cp.start()
            cp.wait()
            chunk_store(0, c)

    nxt = e + 1
    prefetch = nxt < pl.num_programs(0)
    for c in range(n_chunks):
        @pl.when(prefetch & (r == c + 1))
        def _():
            chunk_copy(nxt, c).wait()
            chunk_store(nxt % 2, c)

        @pl.when(prefetch & (r == c))
        def _():
            chunk_copy(nxt, c).start()

    slot = e % 2

    def swiglu(x_ref, y_ref):
        x = x_ref[0]
        a = jnp.dot(x, wg_s[slot], preferred_element_type=F32)
        u = jnp.dot(x, wu_s[slot], preferred_element_type=F32)
        hid = ((a * _sigmoid(a)) * u).astype(BF16)
        y_ref[0] = jnp.dot(hid, wd_s[slot], preferred_element_type=F32).astype(y_ref.dtype)

    if has_ctx:
        @pl.when(r < n_lat_tiles)
        def _():
            swiglu(xl_ref, yl_ref)

        @pl.when(r == n_lat_tiles)
        def _():
            swiglu(xc_ref, yc_ref)
    else:
        swiglu(xl_ref, yl_ref)


def expert_ffn(xsels, w_gate, w_up, w_down, layer_idx):
    d, ff = w_gate.shape[2], w_gate.shape[3]
    has_ctx = len(xsels) == 2
    tm = min(FFN_ROW_TILE, xsels[1].shape[1]) if has_ctx else FFN_ROW_TILE
    while xsels[0].shape[1] // tm + has_ctx <= 3 * FFN_HALVES:
        tm //= 2
    n_lat_tiles = xsels[0].shape[1] // tm
    n_tiles = n_lat_tiles + (1 if has_ctx else 0)
    assert xsels[0].shape[1] % tm == 0 and ff == FFN_HALVES * d and n_tiles > 3 * FFN_HALVES
    if has_ctx:
        assert xsels[1].shape[1] == tm
    lat_spec = pl.BlockSpec((1, tm, d), lambda e, r: (e, jnp.minimum(r, n_lat_tiles - 1), 0))
    ctx_spec = pl.BlockSpec((1, tm, d), lambda e, r: (e, 0, 0))
    io_specs = [lat_spec, ctx_spec] if has_ctx else [lat_spec]
    hbm = pl.BlockSpec(memory_space=pl.ANY)
    return pl.pallas_call(
        functools.partial(_ffn_kernel, n_lat_tiles=n_lat_tiles, has_ctx=has_ctx, layer_idx=layer_idx),
        grid=(N_EXPERTS, n_tiles),
        in_specs=io_specs + [hbm, hbm, hbm],
        out_specs=io_specs,
        out_shape=[jax.ShapeDtypeStruct(x.shape, BF16) for x in xsels],
        scratch_shapes=[pltpu.VMEM((2, d, ff), BF16), pltpu.VMEM((2, d, ff), BF16), pltpu.VMEM((2, ff, d), BF16),
                        pltpu.VMEM((d, d), F32), pltpu.SemaphoreType.DMA((1,))],
        compiler_params=_cparams(("arbitrary", "arbitrary"), 56),
        name="expert_ffn",
    )(*xsels, w_gate, w_up, w_down)


def _combine_kernel(y_ref, slot_ref, gate_ref, x_ref, g2_ref, fg_ref, out_ref, *, final_norm):
    n_e, cap, d = y_ref.shape
    tt = x_ref.shape[1]
    slot, gate = slot_ref[0], gate_ref[0]
    row = lax.broadcasted_iota(jnp.int32, (cap, tt), 0).astype(F32)
    pg = jnp.concatenate([jnp.where(row == slot[e:e + 1], gate[e:e + 1], 0.0).astype(BF16) for e in range(n_e)],
                         axis=0)
    moe = lax.dot_general(pg, y_ref[...].reshape(n_e * cap, d), TN_DIMS, preferred_element_type=F32)
    xn = x_ref[0] + g2_ref[0] * moe
    if final_norm:
        xn = _rms(xn) * fg_ref[...]
    out_ref[0] = xn


def combine_residual(y_sel, slot, gate, x, g2, final_g=None):
    b, t, d = x.shape
    cap = EC_CAPACITY_FACTOR * t // N_EXPERTS
    tt = min(t, 512)
    final_norm = final_g is not None
    fg = (final_g if final_norm else jnp.ones((d,), F32)).reshape(1, d)
    return pl.pallas_call(
        functools.partial(_combine_kernel, final_norm=final_norm),
        grid=(b, t // tt),
        in_specs=[
            pl.BlockSpec((N_EXPERTS, cap, d), lambda i, j: (0, i, 0)),
            pl.BlockSpec((1, N_EXPERTS, tt), lambda i, j: (i, 0, j)),
            pl.BlockSpec((1, N_EXPERTS, tt), lambda i, j: (i, 0, j)),
            pl.BlockSpec((1, tt, d), lambda i, j: (i, j, 0)),
            pl.BlockSpec((1, 1, d), lambda i, j: (i, 0, 0)),
            pl.BlockSpec((1, d), lambda i, j: (0, 0)),
        ],
        out_specs=pl.BlockSpec((1, tt, d), lambda i, j: (i, j, 0)),
        out_shape=jax.ShapeDtypeStruct((b, t, d), F32),
        compiler_params=_cparams(("parallel", "parallel"), 48),
        name="combine_residual",
    )(y_sel, slot, gate, x, g2, fg)


def kernel(x, c, ctx, c_ctx, w_mod, b_mod, norm_g, att_w_in, att_w_out, na_rpb, diff_lambda, diff_subln_g,
           rec_w_in, rec_w_out, rec_lb_logits, rec_gnorm_g, moe_router, moe_w_gate, moe_w_up, moe_w_down,
           final_g):
    b, t, d = x.shape
    l_ctx = ctx.shape[1]
    n_c = b + 1
    r_pad = -(-n_c // 8) * 8
    c_all = jnp.concatenate([c, c_ctx[None, :], jnp.zeros((r_pad - n_c, d), F32)], axis=0)
    mods = modulation_all(c_all, w_mod, b_mod)
    cos, sin = rope_tables(t)

    for l in range(DEPTH):
        last = l == DEPTH - 1
        lat_m = [mods[l, :b, i * d:(i + 1) * d][:, None, :] for i in range(6)]
        ctx_m = [jnp.broadcast_to(mods[l, b, i * d:(i + 1) * d][None, None, :], (b, 1, d)) for i in range(6)]
        sh1, sc1, g1, sh2, sc2, g2 = lat_m
        csh1, csc1, cg1, csh2, csc2, cg2 = ctx_m

        if l % 2 == 0:
            e = l // 2
            w_in = att_w_in[e].astype(BF16)
            w_out = att_w_out[e].astype(BF16)
            p_lat = norm_mod_proj(x, norm_g[l, 0], sc1, sh1, w_in, BF16)
            p_ctx = norm_mod_proj(ctx, norm_g[l, 0], csc1, csh1, w_in, BF16)
            bias = na_bias_tables(na_rpb[e], t // GRID_W, l_ctx, LOG2E)
            oa_lat, oa_ctx = na_attention(p_lat, p_ctx, bias)
            ob_lat, ob_ctx = diff_attention(p_lat, p_ctx, cos, sin, diff_lambda[e], diff_subln_g[e], l)
            ws = [w_out[:A_W], w_out[A_W:]]
            x = outproj_residual([oa_lat, ob_lat], ws, x, g1)
            if not last:
                ctx = outproj_residual([oa_ctx, ob_ctx], ws, ctx, cg1)
        else:
            o = l // 2
            w_in = rec_w_in[o].astype(BF16)
            w_out = rec_w_out[o].astype(BF16)
            p_lat = norm_mod_proj(x, norm_g[l, 0], sc1, sh1, w_in, F32)
            p_ctx = norm_mod_proj(ctx, norm_g[l, 0], csc1, csh1, w_in, F32)
            y_lat, y_ctx = hgrn2_mixer(p_lat, p_ctx, rec_lb_logits, rec_gnorm_g[o], l)
            x = outproj_residual([y_lat], [w_out], x, g1)
            if not last:
                ctx = outproj_residual([y_ctx], [w_out], ctx, cg1)

        xsel_l, slot_l, gate_l = route_gather(x, norm_g[l, 1], sc2, sh2, moe_router[l])
        if last:
            (y_l,) = expert_ffn([xsel_l], moe_w_gate, moe_w_up, moe_w_down, l)
            x = combine_residual(y_l, slot_l, gate_l, x, g2, final_g)
        else:
            xsel_c, slot_c, gate_c = route_gather(ctx, norm_g[l, 1], csc2, csh2, moe_router[l])
            y_l, y_c = expert_ffn([xsel_l, xsel_c], moe_w_gate, moe_w_up, moe_w_down, l)
            x = combine_residual(y_l, slot_l, gate_l, x, g2)
            ctx = combine_residual(y_c, slot_c, gate_c, ctx, cg2)
    return x
```

```python
import functools
import math

import numpy as np
import jax
import jax.numpy as jnp
from jax import lax
from jax.experimental import pallas as pl
from jax.experimental.pallas import tpu as pltpu

F32 = jnp.float32
BF16 = jnp.bfloat16

D_MODEL = 1024
DEPTH = 4
GRID_W = 64
HEAD_DIM = 64
NA_HEADS = 8
NA_WIN_H = 8
NA_WIN_W = 16
DIFF_HEADS = 4
A_W = NA_HEADS * HEAD_DIM
ATT_IN_W = 3072
REC_HEADS = 8
REC_D = 128
N_EXPERTS = 16
EXPERT_FF = 2 * D_MODEL
EC_CAPACITY_FACTOR = 2
ROPE_THETA = 10000.0
NORM_EPS = 1e-6
NEG_INF = -1e30

LANES = 128
V7X_VMEM_BYTES = 64 * 1024 * 1024
MIB = 1024 * 1024

NA_QROWS = 4
NA_KROWS = NA_QROWS + NA_WIN_H - 1
NA_BLOCKS_PER_STEP = 2
DIFF_BLOCKS_PER_STEP = 2
REC_CHUNK = 64
REC_SUB = 16
REC_NSUB = REC_CHUNK // REC_SUB
REC_EXP_CLAMP = 60.0
REC_GROUP = 4
PREFIX_BLOCK = 256
THRESHOLD_RADIX_BITS = 3
FFN_ROW_TILE = 512
FFN_HALVES = 2

LOG2E = math.log2(math.e)
SCORE_MULT = HEAD_DIM ** -0.5 * LOG2E

NT_DIMS = (((1,), (1,)), ((), ()))
TN_DIMS = (((0,), (0,)), ((), ()))


def _cparams(semantics, vmem_mib):
    assert vmem_mib * MIB < V7X_VMEM_BYTES
    return pltpu.CompilerParams(dimension_semantics=semantics, vmem_limit_bytes=vmem_mib * MIB)


def _sigmoid(x):
    return 1.0 / (1.0 + jnp.exp(-x))


def _rms(x, eps=NORM_EPS):
    return x * lax.rsqrt(jnp.mean(x * x, axis=-1, keepdims=True) + eps)


def _mod_kernel(c_ref, w_ref, b_ref, o_ref):
    c = c_ref[...]
    s = (c * _sigmoid(c)).astype(BF16)
    o_ref[0] = jnp.dot(s, w_ref[0].astype(BF16), preferred_element_type=F32) + b_ref[0]


def modulation_all(c_all, w_mod, b_mod):
    depth, d, n = w_mod.shape
    r = c_all.shape[0]
    tn = 1536
    return pl.pallas_call(
        _mod_kernel,
        grid=(depth, n // tn),
        in_specs=[
            pl.BlockSpec((r, d), lambda l, j: (0, 0)),
            pl.BlockSpec((1, d, tn), lambda l, j: (l, 0, j)),
            pl.BlockSpec((1, 1, tn), lambda l, j: (l, 0, j)),
        ],
        out_specs=pl.BlockSpec((1, r, tn), lambda l, j: (l, 0, j)),
        out_shape=jax.ShapeDtypeStruct((depth, r, n), F32),
        compiler_params=_cparams(("parallel", "parallel"), 32),
        name="modulation",
    )(c_all, w_mod, b_mod.reshape(depth, 1, n))


def _norm_mod(x, g, sc, sh):
    return _rms(x) * g * (1.0 + sc) + sh


def _proj_kernel(x_ref, g_ref, sc_ref, sh_ref, w_ref, o_ref, h_scr, *, tn):
    h_scr[...] = _norm_mod(x_ref[0], g_ref[...], sc_ref[0], sh_ref[0]).astype(BF16)
    n = w_ref.shape[1]
    for j in range(n // tn):
        o_ref[0, :, j * tn:(j + 1) * tn] = jnp.dot(
            h_scr[...], w_ref[:, j * tn:(j + 1) * tn], preferred_element_type=F32).astype(o_ref.dtype)


def norm_mod_proj(x, g, sc, sh, w_bf16, out_dtype):
    b, t, d = x.shape
    n = w_bf16.shape[1]
    tm = min(t, 256)
    return pl.pallas_call(
        functools.partial(_proj_kernel, tn=512),
        grid=(b, t // tm),
        in_specs=[
            pl.BlockSpec((1, tm, d), lambda i, j: (i, j, 0)),
            pl.BlockSpec((1, d), lambda i, j: (0, 0)),
            pl.BlockSpec((1, 1, d), lambda i, j: (i, 0, 0)),
            pl.BlockSpec((1, 1, d), lambda i, j: (i, 0, 0)),
            pl.BlockSpec((d, n), lambda i, j: (0, 0)),
        ],
        out_specs=pl.BlockSpec((1, tm, n), lambda i, j: (i, j, 0)),
        out_shape=jax.ShapeDtypeStruct((b, t, n), out_dtype),
        scratch_shapes=[pltpu.VMEM((tm, d), BF16)],
        compiler_params=_cparams(("parallel", "parallel"), 48),
        name="norm_mod_proj",
    )(x, g.reshape(1, d), sc, sh, w_bf16)


def _outproj_kernel(*refs, n_in):
    o_refs, w_refs = refs[:n_in], refs[n_in:2 * n_in]
    x_ref, gate_ref, out_ref = refs[2 * n_in:]
    acc = jnp.dot(o_refs[0][0], w_refs[0][...], preferred_element_type=F32)
    for o_ref, w_ref in zip(o_refs[1:], w_refs[1:]):
        acc = acc + jnp.dot(o_ref[0], w_ref[...], preferred_element_type=F32)
    out_ref[0] = x_ref[0] + gate_ref[0] * acc


def outproj_residual(os_, ws, x, gate):
    b, t, d = x.shape
    tm = min(t, 512)
    n_in = len(os_)
    in_specs = [pl.BlockSpec((1, tm, o.shape[2]), lambda i, j: (i, j, 0)) for o in os_]
    in_specs += [pl.BlockSpec(w.shape, lambda i, j: (0, 0)) for w in ws]
    in_specs += [pl.BlockSpec((1, tm, d), lambda i, j: (i, j, 0)),
                 pl.BlockSpec((1, 1, d), lambda i, j: (i, 0, 0))]
    return pl.pallas_call(
        functools.partial(_outproj_kernel, n_in=n_in),
        grid=(b, t // tm),
        in_specs=in_specs,
        out_specs=pl.BlockSpec((1, tm, d), lambda i, j: (i, j, 0)),
        out_shape=jax.ShapeDtypeStruct((b, t, d), F32),
        compiler_params=_cparams(("parallel", "parallel"), 32),
        name="outproj_residual",
    )(*os_, *ws, x, gate)


def _na_block_tables(rows):
    kh = min(NA_WIN_H, rows)
    assert kh == NA_WIN_H and rows % NA_QROWS == 0 and rows >= NA_KROWS
    n_qb = rows // NA_QROWS
    starts = [min(max(qb * NA_QROWS - kh // 2, 0), rows - NA_KROWS) for qb in range(n_qb)]
    tbl_of, reps = [], []
    for qb in range(n_qb):
        r0 = qb * NA_QROWS
        interior = (r0 - kh // 2 >= 0) and (r0 + NA_QROWS - 1 - kh // 2 <= rows - kh) and starts[qb] == r0 - kh // 2
        key = "interior" if interior else qb
        if key not in reps:
            reps.append(key)
        tbl_of.append(reps.index(key))
    rep_qb = [next(qb for qb in range(n_qb) if tbl_of[qb] == i) for i in range(len(reps))]
    tiles = []
    for qb in rep_qb:
        tiles.append([])
        for qi in range(NA_QROWS):
            r = qb * NA_QROWS + qi
            rs = min(max(r - kh // 2, 0), rows - kh)
            tiles[-1].append([(starts[qb] + kk - r + NA_WIN_H - 1) if rs <= starts[qb] + kk < rs + kh else None
                              for kk in range(NA_KROWS)])
    return starts, tbl_of, tiles


def na_bias_tables(rpb, rows, ctx_len, mult):
    _, _, tiles = _na_block_tables(rows)
    c = np.arange(GRID_W)
    cs = np.clip(c - NA_WIN_W // 2, 0, GRID_W - NA_WIN_W)
    col_ok = (c[None, :] >= cs[:, None]) & (c[None, :] < cs[:, None] + NA_WIN_W)
    dcol = c[None, :] - c[:, None] + NA_WIN_W - 1
    pick = ((dcol[None] == np.arange(2 * NA_WIN_W - 1)[:, None, None]) & col_ok[None]).astype(np.float32)
    toe = jnp.einsum("hrd,dqk->hrqk", rpb.astype(F32), jnp.asarray(pick), precision=lax.Precision.HIGHEST)
    toe = jnp.where(jnp.asarray(col_ok), toe * mult, NEG_INF)
    neg = jnp.full((rpb.shape[0], GRID_W, GRID_W), NEG_INF, F32)
    tables = [jnp.concatenate([jnp.concatenate([neg if d is None else toe[:, d] for d in row], axis=-1)
                               for row in tbl], axis=-2) for tbl in tiles]
    loc = jnp.stack(tables, axis=1)
    return jnp.concatenate([loc, jnp.zeros(loc.shape[:3] + (ctx_len,), F32)], axis=-1)


def _scores_many(qs, ks):
    return [lax.dot_general(q, k, NT_DIMS, preferred_element_type=F32) for q, k in zip(qs, ks)]


def _softmax_pv_many(s2, vs, biases=None):
    if biases is not None:
        s2 = [s + b for s, b in zip(s2, biases)]
    ps = [jnp.exp2(s - jnp.max(s, axis=-1, keepdims=True)) for s in s2]
    inv = [1.0 / jnp.sum(p, axis=-1, keepdims=True) for p in ps]
    return [jnp.dot(p.astype(BF16), v, preferred_element_type=F32) * r for p, v, r in zip(ps, vs, inv)]


def _head_half_scales(mult):
    first = lax.broadcasted_iota(jnp.int32, (1, LANES), 1) < HEAD_DIM
    return first, jnp.where(first, mult, 0.0), jnp.where(first, 0.0, mult)


def _na_kernel(q_ref, k_ref, v_ref, qc_ref, kc_ref, vc_ref, bias_ref, o_ref, oc_ref, *, starts, tbl_of):
    first, *halves = _head_half_scales(SCORE_MULT)
    nq, nk = NA_QROWS * GRID_W, NA_KROWS * GRID_W
    kc, vc = kc_ref[0], vc_ref[0]
    n_qb = len(starts)
    assert n_qb % NA_BLOCKS_PER_STEP == 0

    def group_scores(qb0):
        qs, ks = [], []
        for qb in range(qb0, qb0 + NA_BLOCKS_PER_STEP):
            q = q_ref[0, qb * nq:(qb + 1) * nq, :].astype(F32)
            k_all = jnp.concatenate([k_ref[0, starts[qb] * GRID_W:starts[qb] * GRID_W + nk, :], kc], axis=0)
            for hh in range(2):
                qs.append((q * halves[hh]).astype(BF16))
                ks.append(k_all)
        return _scores_many(qs, ks)

    s_next = group_scores(0)
    for qb0 in range(0, n_qb, NA_BLOCKS_PER_STEP):
        blocks = range(qb0, qb0 + NA_BLOCKS_PER_STEP)
        s_cur = s_next
        if qb0 + NA_BLOCKS_PER_STEP < n_qb:
            s_next = group_scores(qb0 + NA_BLOCKS_PER_STEP)
        vs, bs = [], []
        for qb in blocks:
            v_all = jnp.concatenate([v_ref[0, starts[qb] * GRID_W:starts[qb] * GRID_W + nk, :], vc], axis=0)
            for hh in range(2):
                vs.append(v_all)
                bs.append(bias_ref[hh, tbl_of[qb]])
        outs = _softmax_pv_many(s_cur, vs, bs)
        for i, qb in enumerate(blocks):
            o_ref[0, qb * nq:(qb + 1) * nq, :] = jnp.where(first, outs[2 * i], outs[2 * i + 1]).astype(o_ref.dtype)
    qc = qc_ref[0].astype(F32)
    outs = _softmax_pv_many(_scores_many([(qc * h).astype(BF16) for h in halves], [kc, kc]), [vc, vc])
    oc_ref[0] = jnp.where(first, outs[0], outs[1]).astype(oc_ref.dtype)


def na_attention(p_lat, p_ctx, bias):
    b, t, _ = p_lat.shape
    l = p_ctx.shape[1]
    rows = t // GRID_W
    starts, tbl_of, _ = _na_block_tables(rows)
    n_pairs = NA_HEADS // 2
    kb, vb = 2 * A_W // LANES, 3 * A_W // LANES
    lat = lambda off: pl.BlockSpec((1, t, LANES), lambda j, i: (i, 0, off + j))
    cx = lambda off: pl.BlockSpec((1, l, LANES), lambda j, i: (i, 0, off + j))
    n_tbl, nq, nkc = bias.shape[1:]
    return pl.pallas_call(
        functools.partial(_na_kernel, starts=tuple(starts), tbl_of=tuple(tbl_of)),
        grid=(n_pairs, b),
        in_specs=[lat(0), lat(kb), lat(vb), cx(0), cx(kb), cx(vb),
                  pl.BlockSpec((2, n_tbl, nq, nkc), lambda j, i: (j, 0, 0, 0))],
        out_specs=[pl.BlockSpec((1, t, LANES), lambda j, i: (i, 0, j)),
                   pl.BlockSpec((1, l, LANES), lambda j, i: (i, 0, j))],
        out_shape=[jax.ShapeDtypeStruct((b, t, A_W), BF16), jax.ShapeDtypeStruct((b, l, A_W), BF16)],
        compiler_params=_cparams(("parallel", "parallel"), 48),
        name="na_attention",
    )(p_lat, p_lat, p_lat, p_ctx, p_ctx, p_ctx, bias)


def rope_tables(n):
    t = np.arange(n)
    rows, cols = (t // GRID_W).astype(np.float64), (t % GRID_W).astype(np.float64)
    n_freq = HEAD_DIM // 4
    inv = ROPE_THETA ** (-np.arange(n_freq, dtype=np.float64) / n_freq)
    ang = np.concatenate([rows[:, None] * inv, cols[:, None] * inv], axis=-1)
    cos = np.tile(np.cos(ang), (1, 4))
    sin = np.tile(np.concatenate([-np.sin(ang), np.sin(ang)], axis=-1), (1, 2))
    return jnp.asarray(cos, F32), jnp.asarray(sin, F32)


def _rope(a, cos, sin_signed):
    lane = lax.broadcasted_iota(jnp.int32, (1, LANES), 1)
    low = (lane % HEAD_DIM) < HEAD_DIM // 2
    partner = jnp.where(low, pltpu.roll(a, LANES - HEAD_DIM // 2, 1), pltpu.roll(a, HEAD_DIM // 2, 1))
    return a * cos + partner * sin_signed


def _diff_kernel(q_ref, k_ref, v_ref, qc_ref, kc_ref, vc_ref, cos_ref, sin_ref, lam_ref, gcol_ref,
                 o_ref, oc_ref, q1_scr, q2_scr, k_scr, vt_scr, *, lam_init, tq):
    t, l = q_ref.shape[1], kc_ref.shape[1]
    _, half1, half2 = _head_half_scales(SCORE_MULT)
    lv = lam_ref[...]
    lam = (jnp.exp(jnp.sum(lv[0:1] * lv[1:2], axis=-1, keepdims=True))
           - jnp.exp(jnp.sum(lv[2:3] * lv[3:4], axis=-1, keepdims=True)) + lam_init)
    cos, sin = cos_ref[...], sin_ref[...]
    q = _rope(q_ref[0].astype(F32), cos, sin)
    q1_scr[...] = (q * half1).astype(BF16)
    q2_scr[...] = (q * half2).astype(BF16)
    k_scr[0:t, :] = _rope(k_ref[0].astype(F32), cos, sin).astype(BF16)
    k_scr[t:t + l, :] = kc_ref[0]
    for i in range(t // tq):
        vt_scr[:, i * tq:(i + 1) * tq] = v_ref[0, i * tq:(i + 1) * tq, :].astype(F32).T.astype(BF16)
    vct = vc_ref[0].astype(F32).T.astype(BF16)
    vt_scr[:, t:t + l] = vct
    gain = gcol_ref[...] * (1.0 - lam_init)

    def scores(q_pairs, k):
        return [lax.dot_general(k, q, NT_DIMS, preferred_element_type=F32) for pair in q_pairs for q in pair]

    def finish(s2, vt):
        ps = [jnp.exp2(s - jnp.max(s, axis=0, keepdims=True)) for s in s2]
        inv = [1.0 / jnp.sum(p, axis=0, keepdims=True) for p in ps]
        ot = [jnp.dot(vt, p.astype(BF16), preferred_element_type=F32) * r for p, r in zip(ps, inv)]
        outs = []
        for i in range(len(s2) // 2):
            d = ot[2 * i] - lam * ot[2 * i + 1]
            d = d * lax.rsqrt(jnp.mean(d * d, axis=0, keepdims=True) + NORM_EPS) * gain
            outs.append(d.T)
        return outs

    step = tq * DIFF_BLOCKS_PER_STEP
    group = lambda r0: [(q1_scr[r:r + tq, :], q2_scr[r:r + tq, :]) for r in range(r0, r0 + step, tq)]
    k_all, vt_all = k_scr[...], vt_scr[...]
    s_next = scores(group(0), k_all)
    for r0 in range(0, t, step):
        s_cur = s_next
        if r0 + step < t:
            s_next = scores(group(r0 + step), k_all)
        for j, o in enumerate(finish(s_cur, vt_all)):
            o_ref[0, r0 + j * tq:r0 + (j + 1) * tq, :] = o.astype(o_ref.dtype)
    qc = qc_ref[0].astype(F32)
    (oc,) = finish(scores([((qc * half1).astype(BF16), (qc * half2).astype(BF16))], kc_ref[0]), vct)
    oc_ref[0] = oc.astype(oc_ref.dtype)


def diff_attention(p_lat, p_ctx, cos, sin, lam_vecs, subln_g, layer_idx):
    b, t, _ = p_lat.shape
    l = p_ctx.shape[1]
    lam_init = 0.8 - 0.6 * math.exp(-0.3 * layer_idx)
    qb, kb, vb = A_W // LANES, 4 * A_W // LANES, 5 * A_W // LANES
    lat = lambda off: pl.BlockSpec((1, t, LANES), lambda i, j: (i, 0, off + j))
    cx = lambda off: pl.BlockSpec((1, l, LANES), lambda i, j: (i, 0, off + j))
    const = lambda shape: pl.BlockSpec(shape, lambda i, j: (0, 0))
    tq = 256
    assert t % (tq * DIFF_BLOCKS_PER_STEP) == 0
    return pl.pallas_call(
        functools.partial(_diff_kernel, lam_init=lam_init, tq=tq),
        grid=(b, DIFF_HEADS),
        in_specs=[lat(qb), lat(kb), lat(vb), cx(qb), cx(kb), cx(vb),
                  const((t, LANES)), const((t, LANES)), const(lam_vecs.shape), const((LANES, 1))],
        out_specs=[pl.BlockSpec((1, t, LANES), lambda i, j: (i, 0, j)),
                   pl.BlockSpec((1, l, LANES), lambda i, j: (i, 0, j))],
        out_shape=[jax.ShapeDtypeStruct((b, t, A_W), BF16), jax.ShapeDtypeStruct((b, l, A_W), BF16)],
        scratch_shapes=[pltpu.VMEM((t, LANES), BF16), pltpu.VMEM((t, LANES), BF16),
                        pltpu.VMEM((t + l, LANES), BF16), pltpu.VMEM((LANES, t + l), BF16)],
        compiler_params=_cparams(("parallel", "parallel"), 48),
        name="diff_attention",
    )(p_lat, p_lat, p_lat, p_ctx, p_ctx, p_ctx, cos, sin, lam_vecs, subln_g.reshape(LANES, 1))


def _split2(x):
    a = x.astype(BF16)
    return a, (x - a.astype(F32)).astype(BF16)


def _rec_masks(forward):
    t = lax.broadcasted_iota(jnp.int32, (REC_CHUNK, REC_CHUNK), 0)
    s = lax.broadcasted_iota(jnp.int32, (REC_CHUNK, REC_CHUNK), 1)
    incl = (s <= t) if forward else (s >= t)
    return jnp.where(incl, 1.0, 0.0).astype(BF16), incl


def _rec_group_local(qs, fxs, vb, lbs):
    c = REC_CHUNK
    items = [(d, j) for j in range(REC_GROUP) for d in range(2)]
    rows = lambda a, j: a[j * c:(j + 1) * c]
    masks = [_rec_masks(True), _rec_masks(False)]
    kks, g3s = [], []
    for fx, lb in zip(fxs, lbs):
        e = jnp.exp(-jnp.abs(fx))
        r = 1.0 / (1.0 + e)
        pos = fx >= 0
        f = lb + (1.0 - lb) * jnp.where(pos, r, e * r)
        kks.append((1.0 - lb) * jnp.where(pos, e * r, r))
        g3s.append(jnp.concatenate(_split2(jnp.log(f)), axis=1))
    cs = {(d, j): jnp.dot(masks[d][0], rows(g3s[d], j), preferred_element_type=F32) for d, j in items}
    row_sub = lax.broadcasted_iota(jnp.int32, (c, 1), 0) // REC_SUB
    band = lambda a, p: a[p * REC_SUB:(p + 1) * REC_SUB]
    subs = range(REC_NSUB)
    qcat, kcat, qtil, kdec, dec = {}, {}, {}, {}, {}
    for d, j in items:
        x = cs[d, j]
        x = x[:, 0:LANES] + x[:, LANES:2 * LANES]
        bcum = x
        q, kk = rows(qs, j), rows(kks[d], j)
        if d == 0:
            ends = [bcum[(p + 1) * REC_SUB - 1:(p + 1) * REC_SUB] for p in subs]
            starts = [jnp.zeros((1, LANES), F32)] + ends[:-1]
            blast = ends[-1]
        else:
            ends = [bcum[p * REC_SUB:p * REC_SUB + 1] for p in subs]
            starts = ends[1:] + [jnp.zeros((1, LANES), F32)]
            blast = ends[0]
        ends_mat = jnp.concatenate(ends, axis=0)
        per_row = lambda bounds: jnp.concatenate([jnp.broadcast_to(b_, (REC_SUB, LANES)) for b_ in bounds], axis=0)
        qhat = q * jnp.exp(bcum - per_row(starts))
        bend = per_row(ends)
        ke = kk * jnp.exp(bend - bcum)
        qcat[d, j] = jnp.concatenate([jnp.where(row_sub == p, qhat, 0.0) for p in subs], axis=1).astype(BF16)
        grow = [jnp.exp(jnp.minimum(starts[p] - ends_mat, REC_EXP_CLAMP)) for p in subs]
        kcat[d, j] = jnp.concatenate(
            [jnp.concatenate([band(ke, s) * grow[p][s:s + 1] for s in subs], axis=0) for p in subs],
            axis=1).astype(BF16)
        from_start = jnp.exp(jnp.concatenate(starts, axis=0))
        to_last = jnp.exp(blast - ends_mat)
        qtil[d, j] = jnp.concatenate([band(qhat, p) * from_start[p:p + 1] for p in subs], axis=0).astype(BF16)
        kdec[d, j] = jnp.concatenate([band(ke, p) * to_last[p:p + 1] for p in subs], axis=0).astype(BF16)
        dec[d, j] = jnp.exp(blast)
    sc = {k: lax.dot_general(qcat[k], kcat[k], NT_DIMS, preferred_element_type=F32) for k in items}
    u = {(d, j): lax.dot_general(rows(vb, j), kdec[d, j], TN_DIMS, preferred_element_type=F32)
         for d, j in items}
    scb = {(d, j): jnp.where(masks[d][1], sc[d, j], 0.0).astype(BF16) for d, j in items}
    o_intra = {(d, j): jnp.dot(scb[d, j], rows(vb, j), preferred_element_type=F32) for d, j in items}
    return {k: (o_intra[k], qtil[k], u[k], dec[k]) for k in items}


def _rec_kernel(q_ref, ff_ref, fb_ref, v_ref, g_ref, qc_ref, ffc_ref, fbc_ref, vc_ref, gc_ref,
                lb_ref, gn_ref, y_ref, yc_ref, o_scr, qt_scr, u_scr, dec_scr, *, layer_idx):
    t, l = q_ref.shape[1], qc_ref.shape[1]
    c, grp = REC_CHUNK, REC_GROUP

    def lower_bound(d):
        lg = lb_ref[d]
        ex = jnp.exp(lg - jnp.max(lg, axis=0, keepdims=True))
        return jnp.sum(ex[1:layer_idx + 1], axis=0, keepdims=True) / jnp.sum(ex, axis=0, keepdims=True)

    lbs = (lower_bound(0), lower_bound(1))

    def local_pass(q_r, f_refs, v_r, n_tok, tok0):
        def body(i, carry):
            r0 = pl.multiple_of(i * (grp * c), grp * c)
            q = q_r[0, pl.ds(r0, grp * c), :]
            qs = q * _sigmoid(q)
            vb = v_r[0, pl.ds(r0, grp * c), :].astype(BF16)
            fxs = [f_r[0, pl.ds(r0, grp * c), :] for f_r in f_refs]
            local = _rec_group_local(qs, fxs, vb, lbs)
            for j in range(grp):
                row = pl.multiple_of(tok0 + r0 + j * c, c)
                ci = tok0 // c + i * grp + j
                for d in range(2):
                    o_intra, qtil, u, dec = local[d, j]
                    o_scr[d, pl.ds(row, c), :] = o_intra
                    qt_scr[d, pl.ds(row, c), :] = qtil
                    u_scr[d, ci] = u
                    dec_scr[d, pl.ds(ci, 1), :] = dec
            return carry
        lax.fori_loop(0, n_tok // (grp * c), body, 0)

    def state_pass(c_lo, n, states):
        def body(i, carry):
            sts = list(carry)
            for k in range(grp):
                step = i * grp + k
                for d, ci in ((0, c_lo + step), (1, c_lo + n - 1 - step)):
                    row = pl.multiple_of(ci * c, c)
                    o_scr[d, pl.ds(row, c), :] += lax.dot_general(
                        qt_scr[d, pl.ds(row, c), :], sts[d].astype(BF16), NT_DIMS, preferred_element_type=F32)
                    sts[d] = sts[d] * dec_scr[d, pl.ds(ci, 1), :] + u_scr[d, ci]
            return tuple(sts)
        return lax.fori_loop(0, n // grp, body, states)

    def finish(tok0, n_tok, gate_ref, out_ref):
        o = o_scr[0, tok0:tok0 + n_tok, :] + o_scr[1, tok0:tok0 + n_tok, :]
        gate = gate_ref[0]
        out_ref[0] = ((_rms(o) * gn_ref[...]) * (gate * _sigmoid(gate))).astype(out_ref.dtype)

    local_pass(qc_ref, (ffc_ref, fbc_ref), vc_ref, l, 0)
    local_pass(q_ref, (ff_ref, fb_ref), v_ref, t, l)
    zero = jnp.zeros((REC_D, REC_D), F32)
    states = state_pass(0, l // c, (zero, zero))
    state_pass(l // c, t // c, states)
    finish(0, l, gc_ref, yc_ref)
    finish(l, t, g_ref, y_ref)


def hgrn2_mixer(p_lat, p_ctx, lb_logits, gnorm_g, layer_idx):
    b, t, _ = p_lat.shape
    l = p_ctx.shape[1]
    assert t % (REC_GROUP * REC_CHUNK) == 0 and l % (REC_GROUP * REC_CHUNK) == 0
    n_chunks = (t + l) // REC_CHUNK
    lat = lambda k: pl.BlockSpec((1, t, REC_D), lambda i, j: (i, 0, k * REC_HEADS + j))
    cx = lambda k: pl.BlockSpec((1, l, REC_D), lambda i, j: (i, 0, k * REC_HEADS + j))
    return pl.pallas_call(
        functools.partial(_rec_kernel, layer_idx=layer_idx),
        grid=(b, REC_HEADS),
        in_specs=[lat(k) for k in range(5)] + [cx(k) for k in range(5)] + [
            pl.BlockSpec((2, DEPTH, REC_D), lambda i, j: (0, 0, j)),
            pl.BlockSpec((1, REC_D), lambda i, j: (0, 0))],
        out_specs=[pl.BlockSpec((1, t, REC_D), lambda i, j: (i, 0, j)),
                   pl.BlockSpec((1, l, REC_D), lambda i, j: (i, 0, j))],
        out_shape=[jax.ShapeDtypeStruct((b, t, D_MODEL), BF16), jax.ShapeDtypeStruct((b, l, D_MODEL), BF16)],
        scratch_shapes=[pltpu.VMEM((2, t + l, REC_D), F32), pltpu.VMEM((2, t + l, REC_D), BF16),
                        pltpu.VMEM((2, n_chunks, REC_D, REC_D), F32), pltpu.VMEM((2, n_chunks, REC_D), F32)],
        compiler_params=_cparams(("parallel", "parallel"), 40),
        name="hgrn2_mixer",
    )(*([p_lat] * 5), *([p_ctx] * 5), lb_logits, gnorm_g.reshape(1, REC_D))


def _prefix_count(flags):
    e, t = flags.shape
    blk = min(PREFIX_BLOCK, t)
    ones = jnp.where(flags, 1.0, 0.0)
    s = lax.broadcasted_iota(jnp.int32, (blk, blk), 0)
    u = lax.broadcasted_iota(jnp.int32, (blk, blk), 1)
    upper = jnp.where(s < u, 1.0, 0.0).astype(BF16)
    carry = jnp.zeros((e, 1), F32)
    outs = []
    for i in range(t // blk):
        xs = ones[:, i * blk:(i + 1) * blk]
        outs.append(jnp.dot(xs.astype(BF16), upper, preferred_element_type=F32) + carry)
        carry = carry + jnp.sum(xs, axis=1, keepdims=True)
    return jnp.concatenate(outs, axis=1)


def _route_kernel(x_ref, g_ref, sc_ref, sh_ref, wr_ref, xsel_ref, slot_ref, gate_ref, hb_scr, slot_scr, *, cap):
    e = pl.program_id(1)
    t = x_ref.shape[1]

    @pl.when(e == 0)
    def _():
        h = _norm_mod(x_ref[0], g_ref[...], sc_ref[0], sh_ref[0])
        hb = h.astype(BF16)
        hb_scr[...] = hb
        h_lo = (h - hb.astype(F32)).astype(BF16)
        w_hi, w_lo = _split2(wr_ref[...])
        hi_terms = lax.dot_general(jnp.concatenate([w_hi, w_lo], axis=0), hb, NT_DIMS, preferred_element_type=F32)
        logits = (hi_terms[:N_EXPERTS] + hi_terms[N_EXPERTS:]
                  + lax.dot_general(w_hi, h_lo, NT_DIMS, preferred_element_type=F32))
        ex = jnp.exp(logits - jnp.max(logits, axis=0, keepdims=True))
        aff = ex / jnp.sum(ex, axis=0, keepdims=True)
        bits = lax.bitcast_convert_type(aff, jnp.int32)
        capf = jnp.float32(cap)
        thr = jnp.zeros((N_EXPERTS, 1), jnp.int32)
        for hi in range(30, -1, -THRESHOLD_RADIX_BITS):
            lo = max(hi - THRESHOLD_RADIX_BITS + 1, 0)
            cands = [thr | jnp.int32(m << lo) for m in range(1, 1 << (hi - lo + 1))]
            cnts = [jnp.sum(jnp.where(bits >= cand, 1.0, 0.0), axis=1, keepdims=True) for cand in cands]
            for cand, cnt in zip(cands, cnts):
                thr = jnp.where(cnt >= capf, cand, thr)
        gt = bits > thr
        eq = bits == thr
        need = capf - jnp.sum(jnp.where(gt, 1.0, 0.0), axis=1, keepdims=True)
        sel = gt | (eq & (_prefix_count(eq) < need))
        slot = jnp.where(sel, _prefix_count(sel), -1.0)
        slot_scr[...] = slot
        slot_ref[0] = slot
        gate_ref[0] = aff

    srow = slot_scr[pl.ds(e, 1), :]
    onehot = lax.broadcasted_iota(jnp.int32, (cap, t), 0).astype(F32) == srow
    xsel_ref[0] = jnp.dot(jnp.where(onehot, 1.0, 0.0).astype(BF16), hb_scr[...],
                          preferred_element_type=F32).astype(xsel_ref.dtype)


def route_gather(x, g, sc, sh, w_router):
    b, t, d = x.shape
    cap = EC_CAPACITY_FACTOR * t // N_EXPERTS
    return pl.pallas_call(
        functools.partial(_route_kernel, cap=cap),
        grid=(b, N_EXPERTS),
        in_specs=[
            pl.BlockSpec((1, t, d), lambda i, e: (i, 0, 0)),
            pl.BlockSpec((1, d), lambda i, e: (0, 0)),
            pl.BlockSpec((1, 1, d), lambda i, e: (i, 0, 0)),
            pl.BlockSpec((1, 1, d), lambda i, e: (i, 0, 0)),
            pl.BlockSpec((N_EXPERTS, d), lambda i, e: (0, 0)),
        ],
        out_specs=[pl.BlockSpec((1, cap, d), lambda i, e: (e, i, 0)),
                   pl.BlockSpec((1, N_EXPERTS, t), lambda i, e: (i, 0, 0)),
                   pl.BlockSpec((1, N_EXPERTS, t), lambda i, e: (i, 0, 0))],
        out_shape=[jax.ShapeDtypeStruct((N_EXPERTS, b * cap, d), BF16),
                   jax.ShapeDtypeStruct((b, N_EXPERTS, t), F32),
                   jax.ShapeDtypeStruct((b, N_EXPERTS, t), F32)],
        scratch_shapes=[pltpu.VMEM((t, d), BF16), pltpu.VMEM((N_EXPERTS, t), F32)],
        compiler_params=_cparams(("parallel", "arbitrary"), 48),
        name="route_gather",
    )(x, g.reshape(1, d), sc, sh, w_router.T)


def _ffn_kernel(*refs, n_lat_tiles, has_ctx, layer_idx):
    if has_ctx:
        xl_ref, xc_ref, wg_hbm, wu_hbm, wd_hbm, yl_ref, yc_ref, wg_s, wu_s, wd_s, stage, sem = refs
    else:
        xl_ref, wg_hbm, wu_hbm, wd_hbm, yl_ref, wg_s, wu_s, wd_s, stage, sem = refs
    e, r = pl.program_id(0), pl.program_id(1)
    half = stage.shape[0]
    w_hbm, w_s = (wg_hbm, wu_hbm, wd_hbm), (wg_s, wu_s, wd_s)

    def chunk_copy(expert, c):
        m, h = divmod(c, FFN_HALVES)
        if m < 2:
            src = w_hbm[m].at[layer_idx, expert, :, pl.ds(h * half, half)]
        else:
            src = w_hbm[m].at[layer_idx, expert, pl.ds(h * half, half), :]
        return pltpu.make_async_copy(src, stage, sem.at[0])

    def chunk_store(slot, c):
        m, h = divmod(c, FFN_HALVES)
        if m < 2:
            w_s[m][slot, :, h * half:(h + 1) * half] = stage[...].astype(BF16)
        else:
            w_s[m][slot, h * half:(h + 1) * half, :] = stage[...].astype(BF16)

    n_chunks = 3 * FFN_HALVES

    @pl.when((e == 0) & (r == 0))
    def _():
        for c in range(n_chunks):
            cp = chunk_copy(0, c)
            cp.start()
            cp.wait()
            chunk_store(0, c)

    nxt = e + 1
    prefetch = nxt < pl.num_programs(0)
    for c in range(n_chunks):
        @pl.when(prefetch & (r == c + 1))
        def _():
            chunk_copy(nxt, c).wait()
            chunk_store(nxt % 2, c)

        @pl.when(prefetch & (r == c))
        def _():
            chunk_copy(nxt, c).start()

    slot = e % 2

    def swiglu(x_ref, y_ref):
        x = x_ref[0]
        a = jnp.dot(x, wg_s[slot], preferred_element_type=F32)
        u = jnp.dot(x, wu_s[slot], preferred_element_type=F32)
        hid = ((a * _sigmoid(a)) * u).astype(BF16)
        y_ref[0] = jnp.dot(hid, wd_s[slot], preferred_element_type=F32).astype(y_ref.dtype)

    if has_ctx:
        @pl.when(r < n_lat_tiles)
        def _():
            swiglu(xl_ref, yl_ref)

        @pl.when(r == n_lat_tiles)
        def _():
            swiglu(xc_ref, yc_ref)
    else:
        swiglu(xl_ref, yl_ref)


def expert_ffn(xsels, w_gate, w_up, w_down, layer_idx):
    d, ff = w_gate.shape[2], w_gate.shape[3]
    has_ctx = len(xsels) == 2
    tm = min(FFN_ROW_TILE, xsels[1].shape[1]) if has_ctx else FFN_ROW_TILE
    while xsels[0].shape[1] // tm + has_ctx <= 3 * FFN_HALVES:
        tm //= 2
    n_lat_tiles = xsels[0].shape[1] // tm
    n_tiles = n_lat_tiles + (1 if has_ctx else 0)
    assert xsels[0].shape[1] % tm == 0 and ff == FFN_HALVES * d and n_tiles > 3 * FFN_HALVES
    if has_ctx:
        assert xsels[1].shape[1] == tm
    lat_spec = pl.BlockSpec((1, tm, d), lambda e, r: (e, jnp.minimum(r, n_lat_tiles - 1), 0))
    ctx_spec = pl.BlockSpec((1, tm, d), lambda e, r: (e, 0, 0))
    io_specs = [lat_spec, ctx_spec] if has_ctx else [lat_spec]
    hbm = pl.BlockSpec(memory_space=pl.ANY)
    return pl.pallas_call(
        functools.partial(_ffn_kernel, n_lat_tiles=n_lat_tiles, has_ctx=has_ctx, layer_idx=layer_idx),
        grid=(N_EXPERTS, n_tiles),
        in_specs=io_specs + [hbm, hbm, hbm],
        out_specs=io_specs,
        out_shape=[jax.ShapeDtypeStruct(x.shape, BF16) for x in xsels],
        scratch_shapes=[pltpu.VMEM((2, d, ff), BF16), pltpu.VMEM((2, d, ff), BF16), pltpu.VMEM((2, ff, d), BF16),
                        pltpu.VMEM((d, d), F32), pltpu.SemaphoreType.DMA((1,))],
        compiler_params=_cparams(("arbitrary", "arbitrary"), 56),
        name="expert_ffn",
    )(*xsels, w_gate, w_up, w_down)


def _combine_kernel(y_ref, slot_ref, gate_ref, x_ref, g2_ref, fg_ref, out_ref, *, final_norm):
    n_e, cap, d = y_ref.shape
    tt = x_ref.shape[1]
    slot, gate = slot_ref[0], gate_ref[0]
    row = lax.broadcasted_iota(jnp.int32, (cap, tt), 0).astype(F32)
    pg = jnp.concatenate([jnp.where(row == slot[e:e + 1], gate[e:e + 1], 0.0).astype(BF16) for e in range(n_e)],
                         axis=0)
    moe = lax.dot_general(pg, y_ref[...].reshape(n_e * cap, d), TN_DIMS, preferred_element_type=F32)
    xn = x_ref[0] + g2_ref[0] * moe
    if final_norm:
        xn = _rms(xn) * fg_ref[...]
    out_ref[0] = xn


def combine_residual(y_sel, slot, gate, x, g2, final_g=None):
    b, t, d = x.shape
    cap = EC_CAPACITY_FACTOR * t // N_EXPERTS
    tt = min(t, 512)
    final_norm = final_g is not None
    fg = (final_g if final_norm else jnp.ones((d,), F32)).reshape(1, d)
    return pl.pallas_call(
        functools.partial(_combine_kernel, final_norm=final_norm),
        grid=(b, t // tt),
        in_specs=[
            pl.BlockSpec((N_EXPERTS, cap, d), lambda i, j: (0, i, 0)),
            pl.BlockSpec((1, N_EXPERTS, tt), lambda i, j: (i, 0, j)),
            pl.BlockSpec((1, N_EXPERTS, tt), lambda i, j: (i, 0, j)),
            pl.BlockSpec((1, tt, d), lambda i, j: (i, j, 0)),
            pl.BlockSpec((1, 1, d), lambda i, j: (i, 0, 0)),
            pl.BlockSpec((1, d), lambda i, j: (0, 0)),
        ],
        out_specs=pl.BlockSpec((1, tt, d), lambda i, j: (i, j, 0)),
        out_shape=jax.ShapeDtypeStruct((b, t, d), F32),
        compiler_params=_cparams(("parallel", "parallel"), 48),
        name="combine_residual",
    )(y_sel, slot, gate, x, g2, fg)


def kernel(x, c, ctx, c_ctx, w_mod, b_mod, norm_g, att_w_in, att_w_out, na_rpb, diff_lambda, diff_subln_g,
           rec_w_in, rec_w_out, rec_lb_logits, rec_gnorm_g, moe_router, moe_w_gate, moe_w_up, moe_w_down,
           final_g):
    b, t, d = x.shape
    l_ctx = ctx.shape[1]
    n_c = b + 1
    r_pad = -(-n_c // 8) * 8
    c_all = jnp.concatenate([c, c_ctx[None, :], jnp.zeros((r_pad - n_c, d), F32)], axis=0)
    mods = modulation_all(c_all, w_mod, b_mod)
    cos, sin = rope_tables(t)

    for l in range(DEPTH):
        last = l == DEPTH - 1
        lat_m = [mods[l, :b, i * d:(i + 1) * d][:, None, :] for i in range(6)]
        ctx_m = [jnp.broadcast_to(mods[l, b, i * d:(i + 1) * d][None, None, :], (b, 1, d)) for i in range(6)]
        sh1, sc1, g1, sh2, sc2, g2 = lat_m
        csh1, csc1, cg1, csh2, csc2, cg2 = ctx_m

        if l % 2 == 0:
            e = l // 2
            w_in = att_w_in[e].astype(BF16)
            w_out = att_w_out[e].astype(BF16)
            p_lat = norm_mod_proj(x, norm_g[l, 0], sc1, sh1, w_in, BF16)
            p_ctx = norm_mod_proj(ctx, norm_g[l, 0], csc1, csh1, w_in, BF16)
            bias = na_bias_tables(na_rpb[e], t // GRID_W, l_ctx, LOG2E)
            oa_lat, oa_ctx = na_attention(p_lat, p_ctx, bias)
            ob_lat, ob_ctx = diff_attention(p_lat, p_ctx, cos, sin, diff_lambda[e], diff_subln_g[e], l)
            ws = [w_out[:A_W], w_out[A_W:]]
            x = outproj_residual([oa_lat, ob_lat], ws, x, g1)
            if not last:
                ctx = outproj_residual([oa_ctx, ob_ctx], ws, ctx, cg1)
        else:
            o = l // 2
            w_in = rec_w_in[o].astype(BF16)
            w_out = rec_w_out[o].astype(BF16)
            p_lat = norm_mod_proj(x, norm_g[l, 0], sc1, sh1, w_in, F32)
            p_ctx = norm_mod_proj(ctx, norm_g[l, 0], csc1, csh1, w_in, F32)
            y_lat, y_ctx = hgrn2_mixer(p_lat, p_ctx, rec_lb_logits, rec_gnorm_g[o], l)
            x = outproj_residual([y_lat], [w_out], x, g1)
            if not last:
                ctx = outproj_residual([y_ctx], [w_out], ctx, cg1)

        xsel_l, slot_l, gate_l = route_gather(x, norm_g[l, 1], sc2, sh2, moe_router[l])
        if last:
            (y_l,) = expert_ffn([xsel_l], moe_w_gate, moe_w_up, moe_w_down, l)
            x = combine_residual(y_l, slot_l, gate_l, x, g2, final_g)
        else:
            xsel_c, slot_c, gate_c = route_gather(ctx, norm_g[l, 1], csc2, csh2, moe_router[l])
            y_l, y_c = expert_ffn([xsel_l, xsel_c], moe_w_gate, moe_w_up, moe_w_down, l)
            x = combine_residual(y_l, slot_l, gate_l, x, g2)
            ctx = combine_residual(y_c, slot_c, gate_c, ctx, cg2)
    return x
```

```python
import functools
import math

import numpy as np
import jax
import jax.numpy as jnp
from jax import lax
from jax.experimental import pallas as pl
from jax.experimental.pallas import tpu as pltpu

F32 = jnp.float32
BF16 = jnp.bfloat16

D_MODEL = 1024
DEPTH = 4
GRID_W = 64
HEAD_DIM = 64
NA_HEADS = 8
NA_WIN_H = 8
NA_WIN_W = 16
DIFF_HEADS = 4
A_W = NA_HEADS * HEAD_DIM
ATT_IN_W = 3072
REC_HEADS = 8
REC_D = 128
N_EXPERTS = 16
EXPERT_FF = 2 * D_MODEL
EC_CAPACITY_FACTOR = 2
ROPE_THETA = 10000.0
NORM_EPS = 1e-6
NEG_INF = -1e30

LANES = 128
V7X_VMEM_BYTES = 64 * 1024 * 1024
MIB = 1024 * 1024

NA_QROWS = 4
NA_KROWS = NA_QROWS + NA_WIN_H - 1
NA_BLOCKS_PER_STEP = 2
DIFF_BLOCKS_PER_STEP = 2
REC_CHUNK = 64
REC_SUB = 16
REC_NSUB = REC_CHUNK // REC_SUB
REC_EXP_CLAMP = 60.0
REC_GROUP = 4
PREFIX_BLOCK = 256
THRESHOLD_RADIX_BITS = 3
ROUTE_EXPERTS_PER_STEP = 4
FFN_ROW_TILE = 512
FFN_HALVES = 2

LOG2E = math.log2(math.e)
SCORE_MULT = HEAD_DIM ** -0.5 * LOG2E

NT_DIMS = (((1,), (1,)), ((), ()))
TN_DIMS = (((0,), (0,)), ((), ()))


def _cparams(semantics, vmem_mib):
    assert vmem_mib * MIB < V7X_VMEM_BYTES
    return pltpu.CompilerParams(dimension_semantics=semantics, vmem_limit_bytes=vmem_mib * MIB)


def _sigmoid(x):
    return 1.0 / (1.0 + jnp.exp(-x))


def _rms(x, eps=NORM_EPS):
    return x * lax.rsqrt(jnp.mean(x * x, axis=-1, keepdims=True) + eps)


def _mod_kernel(c_ref, w_ref, b_ref, o_ref):
    c = c_ref[...]
    s = (c * _sigmoid(c)).astype(BF16)
    o_ref[0] = jnp.dot(s, w_ref[0].astype(BF16), preferred_element_type=F32) + b_ref[0]


def modulation_all(c_all, w_mod, b_mod):
    depth, d, n = w_mod.shape
    r = c_all.shape[0]
    tn = 1536
    return pl.pallas_call(
        _mod_kernel,
        grid=(depth, n // tn),
        in_specs=[
            pl.BlockSpec((r, d), lambda l, j: (0, 0)),
            pl.BlockSpec((1, d, tn), lambda l, j: (l, 0, j)),
            pl.BlockSpec((1, 1, tn), lambda l, j: (l, 0, j)),
        ],
        out_specs=pl.BlockSpec((1, r, tn), lambda l, j: (l, 0, j)),
        out_shape=jax.ShapeDtypeStruct((depth, r, n), F32),
        compiler_params=_cparams(("parallel", "parallel"), 32),
        name="modulation",
    )(c_all, w_mod, b_mod.reshape(depth, 1, n))


def _norm_mod(x, g, sc, sh):
    return _rms(x) * g * (1.0 + sc) + sh


def _proj_kernel(x_ref, g_ref, sc_ref, sh_ref, w_ref, o_ref, h_scr, *, tn):
    h_scr[...] = _norm_mod(x_ref[0], g_ref[...], sc_ref[0], sh_ref[0]).astype(BF16)
    n = w_ref.shape[1]
    for j in range(n // tn):
        o_ref[0, :, j * tn:(j + 1) * tn] = jnp.dot(
            h_scr[...], w_ref[:, j * tn:(j + 1) * tn], preferred_element_type=F32).astype(o_ref.dtype)


def norm_mod_proj(x, g, sc, sh, w_bf16, out_dtype):
    b, t, d = x.shape
    n = w_bf16.shape[1]
    tm = min(t, 256)
    return pl.pallas_call(
        functools.partial(_proj_kernel, tn=512),
        grid=(b, t // tm),
        in_specs=[
            pl.BlockSpec((1, tm, d), lambda i, j: (i, j, 0)),
            pl.BlockSpec((1, d), lambda i, j: (0, 0)),
            pl.BlockSpec((1, 1, d), lambda i, j: (i, 0, 0)),
            pl.BlockSpec((1, 1, d), lambda i, j: (i, 0, 0)),
            pl.BlockSpec((d, n), lambda i, j: (0, 0)),
        ],
        out_specs=pl.BlockSpec((1, tm, n), lambda i, j: (i, j, 0)),
        out_shape=jax.ShapeDtypeStruct((b, t, n), out_dtype),
        scratch_shapes=[pltpu.VMEM((tm, d), BF16)],
        compiler_params=_cparams(("parallel", "parallel"), 48),
        name="norm_mod_proj",
    )(x, g.reshape(1, d), sc, sh, w_bf16)


def _outproj_kernel(*refs, n_in):
    o_refs, w_refs = refs[:n_in], refs[n_in:2 * n_in]
    x_ref, gate_ref, out_ref = refs[2 * n_in:]
    acc = jnp.dot(o_refs[0][0], w_refs[0][...], preferred_element_type=F32)
    for o_ref, w_ref in zip(o_refs[1:], w_refs[1:]):
        acc = acc + jnp.dot(o_ref[0], w_ref[...], preferred_element_type=F32)
    out_ref[0] = x_ref[0] + gate_ref[0] * acc


def outproj_residual(os_, ws, x, gate):
    b, t, d = x.shape
    tm = min(t, 512)
    n_in = len(os_)
    in_specs = [pl.BlockSpec((1, tm, o.shape[2]), lambda i, j: (i, j, 0)) for o in os_]
    in_specs += [pl.BlockSpec(w.shape, lambda i, j: (0, 0)) for w in ws]
    in_specs += [pl.BlockSpec((1, tm, d), lambda i, j: (i, j, 0)),
                 pl.BlockSpec((1, 1, d), lambda i, j: (i, 0, 0))]
    return pl.pallas_call(
        functools.partial(_outproj_kernel, n_in=n_in),
        grid=(b, t // tm),
        in_specs=in_specs,
        out_specs=pl.BlockSpec((1, tm, d), lambda i, j: (i, j, 0)),
        out_shape=jax.ShapeDtypeStruct((b, t, d), F32),
        compiler_params=_cparams(("parallel", "parallel"), 32),
        name="outproj_residual",
    )(*os_, *ws, x, gate)


def _na_block_tables(rows):
    kh = min(NA_WIN_H, rows)
    assert kh == NA_WIN_H and rows % NA_QROWS == 0 and rows >= NA_KROWS
    n_qb = rows // NA_QROWS
    starts = [min(max(qb * NA_QROWS - kh // 2, 0), rows - NA_KROWS) for qb in range(n_qb)]
    tbl_of, reps = [], []
    for qb in range(n_qb):
        r0 = qb * NA_QROWS
        interior = (r0 - kh // 2 >= 0) and (r0 + NA_QROWS - 1 - kh // 2 <= rows - kh) and starts[qb] == r0 - kh // 2
        key = "interior" if interior else qb
        if key not in reps:
            reps.append(key)
        tbl_of.append(reps.index(key))
    rep_qb = [next(qb for qb in range(n_qb) if tbl_of[qb] == i) for i in range(len(reps))]
    tiles = []
    for qb in rep_qb:
        tiles.append([])
        for qi in range(NA_QROWS):
            r = qb * NA_QROWS + qi
            rs = min(max(r - kh // 2, 0), rows - kh)
            tiles[-1].append([(starts[qb] + kk - r + NA_WIN_H - 1) if rs <= starts[qb] + kk < rs + kh else None
                              for kk in range(NA_KROWS)])
    return starts, tbl_of, tiles


def na_bias_tables(rpb, rows, ctx_len, mult):
    _, _, tiles = _na_block_tables(rows)
    c = np.arange(GRID_W)
    cs = np.clip(c - NA_WIN_W // 2, 0, GRID_W - NA_WIN_W)
    col_ok = (c[None, :] >= cs[:, None]) & (c[None, :] < cs[:, None] + NA_WIN_W)
    dcol = c[None, :] - c[:, None] + NA_WIN_W - 1
    pick = ((dcol[None] == np.arange(2 * NA_WIN_W - 1)[:, None, None]) & col_ok[None]).astype(np.float32)
    toe = jnp.einsum("hrd,dqk->hrqk", rpb.astype(F32), jnp.asarray(pick), precision=lax.Precision.HIGHEST)
    toe = jnp.where(jnp.asarray(col_ok), toe * mult, NEG_INF)
    neg = jnp.full((rpb.shape[0], GRID_W, GRID_W), NEG_INF, F32)
    tables = [jnp.concatenate([jnp.concatenate([neg if d is None else toe[:, d] for d in row], axis=-1)
                               for row in tbl], axis=-2) for tbl in tiles]
    loc = jnp.stack(tables, axis=1)
    return jnp.concatenate([loc, jnp.zeros(loc.shape[:3] + (ctx_len,), F32)], axis=-1)


def _scores_many(qs, ks):
    return [lax.dot_general(q, k, NT_DIMS, preferred_element_type=F32) for q, k in zip(qs, ks)]


def _softmax_pv_many(s2, vs, biases=None):
    if biases is not None:
        s2 = [s + b for s, b in zip(s2, biases)]
    ps = [jnp.exp2(s - jnp.max(s, axis=-1, keepdims=True)) for s in s2]
    inv = [1.0 / jnp.sum(p, axis=-1, keepdims=True) for p in ps]
    return [jnp.dot(p.astype(BF16), v, preferred_element_type=F32) * r for p, v, r in zip(ps, vs, inv)]


def _head_half_scales(mult):
    first = lax.broadcasted_iota(jnp.int32, (1, LANES), 1) < HEAD_DIM
    return first, jnp.where(first, mult, 0.0), jnp.where(first, 0.0, mult)


def _na_kernel(q_ref, k_ref, v_ref, qc_ref, kc_ref, vc_ref, bias_ref, o_ref, oc_ref, *, starts, tbl_of):
    first, *halves = _head_half_scales(SCORE_MULT)
    nq, nk = NA_QROWS * GRID_W, NA_KROWS * GRID_W
    kc, vc = kc_ref[0], vc_ref[0]
    n_qb = len(starts)
    assert n_qb % NA_BLOCKS_PER_STEP == 0

    def group_scores(qb0):
        qs, ks = [], []
        for qb in range(qb0, qb0 + NA_BLOCKS_PER_STEP):
            q = q_ref[0, qb * nq:(qb + 1) * nq, :].astype(F32)
            k_all = jnp.concatenate([k_ref[0, starts[qb] * GRID_W:starts[qb] * GRID_W + nk, :], kc], axis=0)
            for hh in range(2):
                qs.append((q * halves[hh]).astype(BF16))
                ks.append(k_all)
        return _scores_many(qs, ks)

    s_next = group_scores(0)
    for qb0 in range(0, n_qb, NA_BLOCKS_PER_STEP):
        blocks = range(qb0, qb0 + NA_BLOCKS_PER_STEP)
        s_cur = s_next
        if qb0 + NA_BLOCKS_PER_STEP < n_qb:
            s_next = group_scores(qb0 + NA_BLOCKS_PER_STEP)
        vs, bs = [], []
        for qb in blocks:
            v_all = jnp.concatenate([v_ref[0, starts[qb] * GRID_W:starts[qb] * GRID_W + nk, :], vc], axis=0)
            for hh in range(2):
                vs.append(v_all)
                bs.append(bias_ref[hh, tbl_of[qb]])
        outs = _softmax_pv_many(s_cur, vs, bs)
        for i, qb in enumerate(blocks):
            o_ref[0, qb * nq:(qb + 1) * nq, :] = jnp.where(first, outs[2 * i], outs[2 * i + 1]).astype(o_ref.dtype)
    qc = qc_ref[0].astype(F32)
    outs = _softmax_pv_many(_scores_many([(qc * h).astype(BF16) for h in halves], [kc, kc]), [vc, vc])
    oc_ref[0] = jnp.where(first, outs[0], outs[1]).astype(oc_ref.dtype)


def na_attention(p_lat, p_ctx, bias):
    b, t, _ = p_lat.shape
    l = p_ctx.shape[1]
    rows = t // GRID_W
    starts, tbl_of, _ = _na_block_tables(rows)
    n_pairs = NA_HEADS // 2
    kb, vb = 2 * A_W // LANES, 3 * A_W // LANES
    lat = lambda off: pl.BlockSpec((1, t, LANES), lambda j, i: (i, 0, off + j))
    cx = lambda off: pl.BlockSpec((1, l, LANES), lambda j, i: (i, 0, off + j))
    n_tbl, nq, nkc = bias.shape[1:]
    return pl.pallas_call(
        functools.partial(_na_kernel, starts=tuple(starts), tbl_of=tuple(tbl_of)),
        grid=(n_pairs, b),
        in_specs=[lat(0), lat(kb), lat(vb), cx(0), cx(kb), cx(vb),
                  pl.BlockSpec((2, n_tbl, nq, nkc), lambda j, i: (j, 0, 0, 0))],
        out_specs=[pl.BlockSpec((1, t, LANES), lambda j, i: (i, 0, j)),
                   pl.BlockSpec((1, l, LANES), lambda j, i: (i, 0, j))],
        out_shape=[jax.ShapeDtypeStruct((b, t, A_W), BF16), jax.ShapeDtypeStruct((b, l, A_W), BF16)],
        compiler_params=_cparams(("parallel", "parallel"), 48),
        name="na_attention",
    )(p_lat, p_lat, p_lat, p_ctx, p_ctx, p_ctx, bias)


def rope_tables(n):
    t = np.arange(n)
    rows, cols = (t // GRID_W).astype(np.float64), (t % GRID_W).astype(np.float64)
    n_freq = HEAD_DIM // 4
    inv = ROPE_THETA ** (-np.arange(n_freq, dtype=np.float64) / n_freq)
    ang = np.concatenate([rows[:, None] * inv, cols[:, None] * inv], axis=-1)
    cos = np.tile(np.cos(ang), (1, 4))
    sin = np.tile(np.concatenate([-np.sin(ang), np.sin(ang)], axis=-1), (1, 2))
    return jnp.asarray(cos, F32), jnp.asarray(sin, F32)


def _rope(a, cos, sin_signed):
    lane = lax.broadcasted_iota(jnp.int32, (1, LANES), 1)
    low = (lane % HEAD_DIM) < HEAD_DIM // 2
    partner = jnp.where(low, pltpu.roll(a, LANES - HEAD_DIM // 2, 1), pltpu.roll(a, HEAD_DIM // 2, 1))
    return a * cos + partner * sin_signed


def _diff_kernel(q_ref, k_ref, v_ref, qc_ref, kc_ref, vc_ref, cos_ref, sin_ref, lam_ref, gcol_ref,
                 o_ref, oc_ref, q1_scr, q2_scr, k_scr, vt_scr, *, lam_init, tq):
    t, l = q_ref.shape[1], kc_ref.shape[1]
    _, half1, half2 = _head_half_scales(SCORE_MULT)
    lv = lam_ref[...]
    lam = (jnp.exp(jnp.sum(lv[0:1] * lv[1:2], axis=-1, keepdims=True))
           - jnp.exp(jnp.sum(lv[2:3] * lv[3:4], axis=-1, keepdims=True)) + lam_init)
    cos, sin = cos_ref[...], sin_ref[...]
    q = _rope(q_ref[0].astype(F32), cos, sin)
    q1_scr[...] = (q * half1).astype(BF16)
    q2_scr[...] = (q * half2).astype(BF16)
    k_scr[0:t, :] = _rope(k_ref[0].astype(F32), cos, sin).astype(BF16)
    k_scr[t:t + l, :] = kc_ref[0]
    for i in range(t // tq):
        vt_scr[:, i * tq:(i + 1) * tq] = v_ref[0, i * tq:(i + 1) * tq, :].astype(F32).T.astype(BF16)
    vct = vc_ref[0].astype(F32).T.astype(BF16)
    vt_scr[:, t:t + l] = vct
    gain = gcol_ref[...] * (1.0 - lam_init)

    def scores(q_pairs, k):
        return [lax.dot_general(k, q, NT_DIMS, preferred_element_type=F32) for pair in q_pairs for q in pair]

    def finish(s2, vt):
        ps = [jnp.exp2(s - jnp.max(s, axis=0, keepdims=True)) for s in s2]
        inv = [1.0 / jnp.sum(p, axis=0, keepdims=True) for p in ps]
        ot = [jnp.dot(vt, p.astype(BF16), preferred_element_type=F32) * r for p, r in zip(ps, inv)]
        outs = []
        for i in range(len(s2) // 2):
            d = ot[2 * i] - lam * ot[2 * i + 1]
            d = d * lax.rsqrt(jnp.mean(d * d, axis=0, keepdims=True) + NORM_EPS) * gain
            outs.append(d.T)
        return outs

    step = tq * DIFF_BLOCKS_PER_STEP
    group = lambda r0: [(q1_scr[r:r + tq, :], q2_scr[r:r + tq, :]) for r in range(r0, r0 + step, tq)]
    k_all, vt_all = k_scr[...], vt_scr[...]
    s_next = scores(group(0), k_all)
    for r0 in range(0, t, step):
        s_cur = s_next
        if r0 + step < t:
            s_next = scores(group(r0 + step), k_all)
        for j, o in enumerate(finish(s_cur, vt_all)):
            o_ref[0, r0 + j * tq:r0 + (j + 1) * tq, :] = o.astype(o_ref.dtype)
    qc = qc_ref[0].astype(F32)
    (oc,) = finish(scores([((qc * half1).astype(BF16), (qc * half2).astype(BF16))], kc_ref[0]), vct)
    oc_ref[0] = oc.astype(oc_ref.dtype)


def diff_attention(p_lat, p_ctx, cos, sin, lam_vecs, subln_g, layer_idx):
    b, t, _ = p_lat.shape
    l = p_ctx.shape[1]
    lam_init = 0.8 - 0.6 * math.exp(-0.3 * layer_idx)
    qb, kb, vb = A_W // LANES, 4 * A_W // LANES, 5 * A_W // LANES
    lat = lambda off: pl.BlockSpec((1, t, LANES), lambda i, j: (i, 0, off + j))
    cx = lambda off: pl.BlockSpec((1, l, LANES), lambda i, j: (i, 0, off + j))
    const = lambda shape: pl.BlockSpec(shape, lambda i, j: (0, 0))
    tq = 256
    assert t % (tq * DIFF_BLOCKS_PER_STEP) == 0
    return pl.pallas_call(
        functools.partial(_diff_kernel, lam_init=lam_init, tq=tq),
        grid=(b, DIFF_HEADS),
        in_specs=[lat(qb), lat(kb), lat(vb), cx(qb), cx(kb), cx(vb),
                  const((t, LANES)), const((t, LANES)), const(lam_vecs.shape), const((LANES, 1))],
        out_specs=[pl.BlockSpec((1, t, LANES), lambda i, j: (i, 0, j)),
                   pl.BlockSpec((1, l, LANES), lambda i, j: (i, 0, j))],
        out_shape=[jax.ShapeDtypeStruct((b, t, A_W), BF16), jax.ShapeDtypeStruct((b, l, A_W), BF16)],
        scratch_shapes=[pltpu.VMEM((t, LANES), BF16), pltpu.VMEM((t, LANES), BF16),
                        pltpu.VMEM((t + l, LANES), BF16), pltpu.VMEM((LANES, t + l), BF16)],
        compiler_params=_cparams(("parallel", "parallel"), 48),
        name="diff_attention",
    )(p_lat, p_lat, p_lat, p_ctx, p_ctx, p_ctx, cos, sin, lam_vecs, subln_g.reshape(LANES, 1))


def _split2(x):
    a = x.astype(BF16)
    return a, (x - a.astype(F32)).astype(BF16)


def _rec_masks(forward):
    t = lax.broadcasted_iota(jnp.int32, (REC_CHUNK, REC_CHUNK), 0)
    s = lax.broadcasted_iota(jnp.int32, (REC_CHUNK, REC_CHUNK), 1)
    incl = (s <= t) if forward else (s >= t)
    return jnp.where(incl, 1.0, 0.0).astype(BF16), incl


def _rec_group_local(qs, fxs, vb, lbs):
    c = REC_CHUNK
    items = [(d, j) for j in range(REC_GROUP) for d in range(2)]
    rows = lambda a, j: a[j * c:(j + 1) * c]
    masks = [_rec_masks(True), _rec_masks(False)]
    kks, g3s = [], []
    for fx, lb in zip(fxs, lbs):
        e = jnp.exp(-jnp.abs(fx))
        r = 1.0 / (1.0 + e)
        pos = fx >= 0
        f = lb + (1.0 - lb) * jnp.where(pos, r, e * r)
        kks.append((1.0 - lb) * jnp.where(pos, e * r, r))
        g3s.append(jnp.concatenate(_split2(jnp.log(f)), axis=1))
    cs = {(d, j): jnp.dot(masks[d][0], rows(g3s[d], j), preferred_element_type=F32) for d, j in items}
    row_sub = lax.broadcasted_iota(jnp.int32, (c, 1), 0) // REC_SUB
    band = lambda a, p: a[p * REC_SUB:(p + 1) * REC_SUB]
    subs = range(REC_NSUB)
    qcat, kcat, qtil, kdec, dec = {}, {}, {}, {}, {}
    for d, j in items:
        x = cs[d, j]
        x = x[:, 0:LANES] + x[:, LANES:2 * LANES]
        bcum = x
        q, kk = rows(qs, j), rows(kks[d], j)
        if d == 0:
            ends = [bcum[(p + 1) * REC_SUB - 1:(p + 1) * REC_SUB] for p in subs]
            starts = [jnp.zeros((1, LANES), F32)] + ends[:-1]
            blast = ends[-1]
        else:
            ends = [bcum[p * REC_SUB:p * REC_SUB + 1] for p in subs]
            starts = ends[1:] + [jnp.zeros((1, LANES), F32)]
            blast = ends[0]
        ends_mat = jnp.concatenate(ends, axis=0)
        per_row = lambda bounds: jnp.concatenate([jnp.broadcast_to(b_, (REC_SUB, LANES)) for b_ in bounds], axis=0)
        qhat = q * jnp.exp(bcum - per_row(starts))
        bend = per_row(ends)
        ke = kk * jnp.exp(bend - bcum)
        qcat[d, j] = jnp.concatenate([jnp.where(row_sub == p, qhat, 0.0) for p in subs], axis=1).astype(BF16)
        grow = [jnp.exp(jnp.minimum(starts[p] - ends_mat, REC_EXP_CLAMP)) for p in subs]
        kcat[d, j] = jnp.concatenate(
            [jnp.concatenate([band(ke, s) * grow[p][s:s + 1] for s in subs], axis=0) for p in subs],
            axis=1).astype(BF16)
        from_start = jnp.exp(jnp.concatenate(starts, axis=0))
        to_last = jnp.exp(blast - ends_mat)
        qtil[d, j] = jnp.concatenate([band(qhat, p) * from_start[p:p + 1] for p in subs], axis=0).astype(BF16)
        kdec[d, j] = jnp.concatenate([band(ke, p) * to_last[p:p + 1] for p in subs], axis=0).astype(BF16)
        dec[d, j] = jnp.exp(blast)
    sc = {k: lax.dot_general(qcat[k], kcat[k], NT_DIMS, preferred_element_type=F32) for k in items}
    u2 = [lax.dot_general(rows(vb, j), jnp.concatenate([kdec[0, j], kdec[1, j]], axis=1), TN_DIMS,
                          preferred_element_type=F32) for j in range(REC_GROUP)]
    u = {(d, j): u2[j][:, d * LANES:(d + 1) * LANES] for d, j in items}
    scb = {(d, j): jnp.where(masks[d][1], sc[d, j], 0.0).astype(BF16) for d, j in items}
    o_intra = {(d, j): jnp.dot(scb[d, j], rows(vb, j), preferred_element_type=F32) for d, j in items}
    return {k: (o_intra[k], qtil[k], u[k], dec[k]) for k in items}


def _rec_kernel(q_ref, ff_ref, fb_ref, v_ref, g_ref, qc_ref, ffc_ref, fbc_ref, vc_ref, gc_ref,
                lb_ref, gn_ref, y_ref, yc_ref, o_scr, qt_scr, u_scr, dec_scr, *, layer_idx):
    t, l = q_ref.shape[1], qc_ref.shape[1]
    c, grp = REC_CHUNK, REC_GROUP

    def lower_bound(d):
        lg = lb_ref[d]
        ex = jnp.exp(lg - jnp.max(lg, axis=0, keepdims=True))
        return jnp.sum(ex[1:layer_idx + 1], axis=0, keepdims=True) / jnp.sum(ex, axis=0, keepdims=True)

    lbs = (lower_bound(0), lower_bound(1))

    def local_pass(q_r, f_refs, v_r, n_tok, tok0):
        def body(i, carry):
            r0 = pl.multiple_of(i * (grp * c), grp * c)
            q = q_r[0, pl.ds(r0, grp * c), :]
            qs = q * _sigmoid(q)
            vb = v_r[0, pl.ds(r0, grp * c), :].astype(BF16)
            fxs = [f_r[0, pl.ds(r0, grp * c), :] for f_r in f_refs]
            local = _rec_group_local(qs, fxs, vb, lbs)
            for j in range(grp):
                row = pl.multiple_of(tok0 + r0 + j * c, c)
                ci = tok0 // c + i * grp + j
                for d in range(2):
                    o_intra, qtil, u, dec = local[d, j]
                    o_scr[d, pl.ds(row, c), :] = o_intra
                    qt_scr[d, pl.ds(row, c), :] = qtil
                    u_scr[d, ci] = u
                    dec_scr[d, pl.ds(ci, 1), :] = dec
            return carry
        lax.fori_loop(0, n_tok // (grp * c), body, 0)

    def state_pass(c_lo, n, states):
        def body(i, carry):
            sts = list(carry)
            for k in range(grp):
                step = i * grp + k
                for d, ci in ((0, c_lo + step), (1, c_lo + n - 1 - step)):
                    row = pl.multiple_of(ci * c, c)
                    o_scr[d, pl.ds(row, c), :] += lax.dot_general(
                        qt_scr[d, pl.ds(row, c), :], sts[d].astype(BF16), NT_DIMS, preferred_element_type=F32)
                    sts[d] = sts[d] * dec_scr[d, pl.ds(ci, 1), :] + u_scr[d, ci]
            return tuple(sts)
        return lax.fori_loop(0, n // grp, body, states)

    def finish(tok0, n_tok, gate_ref, out_ref):
        o = o_scr[0, tok0:tok0 + n_tok, :] + o_scr[1, tok0:tok0 + n_tok, :]
        gate = gate_ref[0]
        out_ref[0] = ((_rms(o) * gn_ref[...]) * (gate * _sigmoid(gate))).astype(out_ref.dtype)

    local_pass(qc_ref, (ffc_ref, fbc_ref), vc_ref, l, 0)
    local_pass(q_ref, (ff_ref, fb_ref), v_ref, t, l)
    zero = jnp.zeros((REC_D, REC_D), F32)
    states = state_pass(0, l // c, (zero, zero))
    state_pass(l // c, t // c, states)
    finish(0, l, gc_ref, yc_ref)
    finish(l, t, g_ref, y_ref)


def hgrn2_mixer(p_lat, p_ctx, lb_logits, gnorm_g, layer_idx):
    b, t, _ = p_lat.shape
    l = p_ctx.shape[1]
    assert t % (REC_GROUP * REC_CHUNK) == 0 and l % (REC_GROUP * REC_CHUNK) == 0
    n_chunks = (t + l) // REC_CHUNK
    lat = lambda k: pl.BlockSpec((1, t, REC_D), lambda i, j: (i, 0, k * REC_HEADS + j))
    cx = lambda k: pl.BlockSpec((1, l, REC_D), lambda i, j: (i, 0, k * REC_HEADS + j))
    return pl.pallas_call(
        functools.partial(_rec_kernel, layer_idx=layer_idx),
        grid=(b, REC_HEADS),
        in_specs=[lat(k) for k in range(5)] + [cx(k) for k in range(5)] + [
            pl.BlockSpec((2, DEPTH, REC_D), lambda i, j: (0, 0, j)),
            pl.BlockSpec((1, REC_D), lambda i, j: (0, 0))],
        out_specs=[pl.BlockSpec((1, t, REC_D), lambda i, j: (i, 0, j)),
                   pl.BlockSpec((1, l, REC_D), lambda i, j: (i, 0, j))],
        out_shape=[jax.ShapeDtypeStruct((b, t, D_MODEL), BF16), jax.ShapeDtypeStruct((b, l, D_MODEL), BF16)],
        scratch_shapes=[pltpu.VMEM((2, t + l, REC_D), F32), pltpu.VMEM((2, t + l, REC_D), BF16),
                        pltpu.VMEM((2, n_chunks, REC_D, REC_D), F32), pltpu.VMEM((2, n_chunks, REC_D), F32)],
        compiler_params=_cparams(("parallel", "parallel"), 40),
        name="hgrn2_mixer",
    )(*([p_lat] * 5), *([p_ctx] * 5), lb_logits, gnorm_g.reshape(1, REC_D))


def _prefix_count(flags):
    e, t = flags.shape
    blk = min(PREFIX_BLOCK, t)
    ones = jnp.where(flags, 1.0, 0.0)
    s = lax.broadcasted_iota(jnp.int32, (blk, blk), 0)
    u = lax.broadcasted_iota(jnp.int32, (blk, blk), 1)
    upper = jnp.where(s < u, 1.0, 0.0).astype(BF16)
    carry = jnp.zeros((e, 1), F32)
    outs = []
    for i in range(t // blk):
        xs = ones[:, i * blk:(i + 1) * blk]
        outs.append(jnp.dot(xs.astype(BF16), upper, preferred_element_type=F32) + carry)
        carry = carry + jnp.sum(xs, axis=1, keepdims=True)
    return jnp.concatenate(outs, axis=1)


def _capacity_slots(affs, caps):
    bits = [lax.bitcast_convert_type(a, jnp.int32) for a in affs]
    capf = [jnp.float32(c) for c in caps]
    thr = [jnp.zeros((N_EXPERTS, 1), jnp.int32) for _ in affs]
    for hi in range(30, -1, -THRESHOLD_RADIX_BITS):
        lo = max(hi - THRESHOLD_RADIX_BITS + 1, 0)
        cands = [[t | jnp.int32(m << lo) for m in range(1, 1 << (hi - lo + 1))] for t in thr]
        cnts = [[jnp.sum(jnp.where(b >= cand, 1.0, 0.0), axis=1, keepdims=True) for cand in cs]
                for b, cs in zip(bits, cands)]
        for i in range(len(affs)):
            for cand, cnt in zip(cands[i], cnts[i]):
                thr[i] = jnp.where(cnt >= capf[i], cand, thr[i])
    slots = []
    for b, t, c in zip(bits, thr, capf):
        gt = b > t
        eq = b == t
        need = c - jnp.sum(jnp.where(gt, 1.0, 0.0), axis=1, keepdims=True)
        sel = gt | (eq & (_prefix_count(eq) < need))
        slots.append(jnp.where(sel, _prefix_count(sel), -1.0))
    return slots


def _route_kernel(*refs, caps):
    n = len(caps)
    x_refs, g_ref = refs[:n], refs[n]
    sc_refs, sh_refs, wr_ref = refs[n + 1:2 * n + 1], refs[2 * n + 1:3 * n + 1], refs[3 * n + 1]
    outs = refs[3 * n + 2:]
    xsel_refs, slot_refs, gate_refs = outs[:n], outs[n:2 * n], outs[2 * n:3 * n]
    hb_scrs, slot_scrs = outs[3 * n:4 * n], outs[4 * n:5 * n]
    e = pl.program_id(1)

    @pl.when(e == 0)
    def _():
        w_hi, w_lo = _split2(wr_ref[...])
        w2 = jnp.concatenate([w_hi, w_lo], axis=0)
        affs = []
        for x_ref, sc_ref, sh_ref, hb_scr in zip(x_refs, sc_refs, sh_refs, hb_scrs):
            h = _norm_mod(x_ref[0], g_ref[...], sc_ref[0], sh_ref[0])
            hb = h.astype(BF16)
            hb_scr[...] = hb
            h_lo = (h - hb.astype(F32)).astype(BF16)
            hi_terms = lax.dot_general(w2, hb, NT_DIMS, preferred_element_type=F32)
            logits = (hi_terms[:N_EXPERTS] + hi_terms[N_EXPERTS:]
                      + lax.dot_general(w_hi, h_lo, NT_DIMS, preferred_element_type=F32))
            ex = jnp.exp(logits - jnp.max(logits, axis=0, keepdims=True))
            affs.append(ex / jnp.sum(ex, axis=0, keepdims=True))
        for aff, slot, slot_scr, slot_ref, gate_ref in zip(affs, _capacity_slots(affs, caps), slot_scrs, slot_refs,
                                                           gate_refs):
            slot_scr[...] = slot
            slot_ref[0] = slot
            gate_ref[0] = aff

    for cap, xsel_ref, hb_scr, slot_scr in zip(caps, xsel_refs, hb_scrs, slot_scrs):
        row = lax.broadcasted_iota(jnp.int32, (cap, hb_scr.shape[0]), 0).astype(F32)
        picks = [jnp.where(row == slot_scr[pl.ds(e * ROUTE_EXPERTS_PER_STEP + i, 1), :], 1.0, 0.0).astype(BF16)
                 for i in range(ROUTE_EXPERTS_PER_STEP)]
        for i, pick in enumerate(picks):
            xsel_ref[i] = jnp.dot(pick, hb_scr[...], preferred_element_type=F32).astype(xsel_ref.dtype)


def route_gather(xs, g, scs, shs, w_router):
    b, _, d = xs[0].shape
    ts = [x.shape[1] for x in xs]
    caps = [EC_CAPACITY_FACTOR * t // N_EXPERTS for t in ts]
    n = len(xs)
    vec = pl.BlockSpec((1, 1, d), lambda i, e: (i, 0, 0))
    per_sample = lambda t: pl.BlockSpec((1, N_EXPERTS, t), lambda i, e: (i, 0, 0))
    outs = pl.pallas_call(
        functools.partial(_route_kernel, caps=tuple(caps)),
        grid=(b, N_EXPERTS // ROUTE_EXPERTS_PER_STEP),
        in_specs=([pl.BlockSpec((1, t, d), lambda i, e: (i, 0, 0)) for t in ts]
                  + [pl.BlockSpec((1, d), lambda i, e: (0, 0))] + [vec] * (2 * n)
                  + [pl.BlockSpec((N_EXPERTS, d), lambda i, e: (0, 0))]),
        out_specs=([pl.BlockSpec((ROUTE_EXPERTS_PER_STEP, c, d), lambda i, e: (e, i, 0)) for c in caps]
                   + [per_sample(t) for t in ts] * 2),
        out_shape=([jax.ShapeDtypeStruct((N_EXPERTS, b * c, d), BF16) for c in caps]
                   + [jax.ShapeDtypeStruct((b, N_EXPERTS, t), F32) for t in ts] * 2),
        scratch_shapes=([pltpu.VMEM((t, d), BF16) for t in ts] + [pltpu.VMEM((N_EXPERTS, t), F32) for t in ts]),
        compiler_params=_cparams(("parallel", "arbitrary"), 48),
        name="route_gather",
    )(*xs, g.reshape(1, d), *scs, *shs, w_router.T)
    return [(outs[i], outs[n + i], outs[2 * n + i]) for i in range(n)]


def _ffn_kernel(*refs, n_lat_tiles, has_ctx, layer_idx):
    if has_ctx:
        xl_ref, xc_ref, wg_hbm, wu_hbm, wd_hbm, yl_ref, yc_ref, wg_s, wu_s, wd_s, stage, sem = refs
    else:
        xl_ref, wg_hbm, wu_hbm, wd_hbm, yl_ref, wg_s, wu_s, wd_s, stage, sem = refs
    e, r = pl.program_id(0), pl.program_id(1)
    half = stage.shape[0]
    w_hbm, w_s = (wg_hbm, wu_hbm, wd_hbm), (wg_s, wu_s, wd_s)

    def chunk_copy(expert, c):
        m, h = divmod(c, FFN_HALVES)
        if m < 2:
            src = w_hbm[m].at[layer_idx, expert, :, pl.ds(h * half, half)]
        else:
            src = w_hbm[m].at[layer_idx, expert, pl.ds(h * half, half), :]
        return pltpu.make_async_copy(src, stage, sem.at[0])

    def chunk_store(slot, c):
        m, h = divmod(c, FFN_HALVES)
        if m < 2:
            w_s[m][slot, :, h * half:(h + 1) * half] = stage[...].astype(BF16)
        else:
            w_s[m][slot, h * half:(h + 1) * half, :] = stage[...].astype(BF16)

    n_chunks = 3 * FFN_HALVES

    @pl.when((e == 0) & (r == 0))
    def _():
        for c in range(n_chunks):
            cp = chunk_copy(0, c)
            cp.start()
            cp.wait()
            chunk_store(0, c)

    nxt = e + 1
    prefetch = nxt < pl.num_programs(0)
    for c in range(n_chunks):
        @pl.when(prefetch & (r == c + 1))
        def _():
            chunk_copy(nxt, c).wait()
            chunk_store(nxt % 2, c)

        @pl.when(prefetch & (r == c))
        def _():
            chunk_copy(nxt, c).start()

    slot = e % 2

    def swiglu(x_ref, y_ref):
        x = x_ref[0]
        a = jnp.dot(x, wg_s[slot], preferred_element_type=F32)
        u = jnp.dot(x, wu_s[slot], preferred_element_type=F32)
        hid = ((a * _sigmoid(a)) * u).astype(BF16)
        y_ref[0] = jnp.dot(hid, wd_s[slot], preferred_element_type=F32).astype(y_ref.dtype)

    if has_ctx:
        @pl.when(r < n_lat_tiles)
        def _():
            swiglu(xl_ref, yl_ref)

        @pl.when(r == n_lat_tiles)
        def _():
            swiglu(xc_ref, yc_ref)
    else:
        swiglu(xl_ref, yl_ref)


def expert_ffn(xsels, w_gate, w_up, w_down, layer_idx):
    d, ff = w_gate.shape[2], w_gate.shape[3]
    has_ctx = len(xsels) == 2
    tm = min(FFN_ROW_TILE, xsels[1].shape[1]) if has_ctx else FFN_ROW_TILE
    while xsels[0].shape[1] // tm + has_ctx <= 3 * FFN_HALVES:
        tm //= 2
    n_lat_tiles = xsels[0].shape[1] // tm
    n_tiles = n_lat_tiles + (1 if has_ctx else 0)
    assert xsels[0].shape[1] % tm == 0 and ff == FFN_HALVES * d and n_tiles > 3 * FFN_HALVES
    if has_ctx:
        assert xsels[1].shape[1] == tm
    lat_spec = pl.BlockSpec((1, tm, d), lambda e, r: (e, jnp.minimum(r, n_lat_tiles - 1), 0))
    ctx_spec = pl.BlockSpec((1, tm, d), lambda e, r: (e, 0, 0))
    io_specs = [lat_spec, ctx_spec] if has_ctx else [lat_spec]
    hbm = pl.BlockSpec(memory_space=pl.ANY)
    return pl.pallas_call(
        functools.partial(_ffn_kernel, n_lat_tiles=n_lat_tiles, has_ctx=has_ctx, layer_idx=layer_idx),
        grid=(N_EXPERTS, n_tiles),
        in_specs=io_specs + [hbm, hbm, hbm],
        out_specs=io_specs,
        out_shape=[jax.ShapeDtypeStruct(x.shape, BF16) for x in xsels],
        scratch_shapes=[pltpu.VMEM((2, d, ff), BF16), pltpu.VMEM((2, d, ff), BF16), pltpu.VMEM((2, ff, d), BF16),
                        pltpu.VMEM((d, d), F32), pltpu.SemaphoreType.DMA((1,))],
        compiler_params=_cparams(("arbitrary", "arbitrary"), 56),
        name="expert_ffn",
    )(*xsels, w_gate, w_up, w_down)


def _combine_kernel(y_ref, slot_ref, gate_ref, x_ref, g2_ref, fg_ref, out_ref, *, final_norm):
    n_e, cap, d = y_ref.shape
    tt = x_ref.shape[1]
    slot, gate = slot_ref[0], gate_ref[0]
    row = lax.broadcasted_iota(jnp.int32, (cap, tt), 0).astype(F32)
    pg = jnp.concatenate([jnp.where(row == slot[e:e + 1], gate[e:e + 1], 0.0).astype(BF16) for e in range(n_e)],
                         axis=0)
    moe = lax.dot_general(pg, y_ref[...].reshape(n_e * cap, d), TN_DIMS, preferred_element_type=F32)
    xn = x_ref[0] + g2_ref[0] * moe
    if final_norm:
        xn = _rms(xn) * fg_ref[...]
    out_ref[0] = xn


def combine_residual(y_sel, slot, gate, x, g2, final_g=None):
    b, t, d = x.shape
    cap = EC_CAPACITY_FACTOR * t // N_EXPERTS
    tt = min(t, 512)
    final_norm = final_g is not None
    fg = (final_g if final_norm else jnp.ones((d,), F32)).reshape(1, d)
    return pl.pallas_call(
        functools.partial(_combine_kernel, final_norm=final_norm),
        grid=(b, t // tt),
        in_specs=[
            pl.BlockSpec((N_EXPERTS, cap, d), lambda i, j: (0, i, 0)),
            pl.BlockSpec((1, N_EXPERTS, tt), lambda i, j: (i, 0, j)),
            pl.BlockSpec((1, N_EXPERTS, tt), lambda i, j: (i, 0, j)),
            pl.BlockSpec((1, tt, d), lambda i, j: (i, j, 0)),
            pl.BlockSpec((1, 1, d), lambda i, j: (i, 0, 0)),
            pl.BlockSpec((1, d), lambda i, j: (0, 0)),
        ],
        out_specs=pl.BlockSpec((1, tt, d), lambda i, j: (i, j, 0)),
        out_shape=jax.ShapeDtypeStruct((b, t, d), F32),
        compiler_params=_cparams(("parallel", "parallel"), 48),
        name="combine_residual",
    )(y_sel, slot, gate, x, g2, fg)


def kernel(x, c, ctx, c_ctx, w_mod, b_mod, norm_g, att_w_in, att_w_out, na_rpb, diff_lambda, diff_subln_g,
           rec_w_in, rec_w_out, rec_lb_logits, rec_gnorm_g, moe_router, moe_w_gate, moe_w_up, moe_w_down,
           final_g):
    b, t, d = x.shape
    l_ctx = ctx.shape[1]
    n_c = b + 1
    r_pad = -(-n_c // 8) * 8
    c_all = jnp.concatenate([c, c_ctx[None, :], jnp.zeros((r_pad - n_c, d), F32)], axis=0)
    mods = modulation_all(c_all, w_mod, b_mod)
    cos, sin = rope_tables(t)

    for l in range(DEPTH):
        last = l == DEPTH - 1
        lat_m = [mods[l, :b, i * d:(i + 1) * d][:, None, :] for i in range(6)]
        ctx_m = [jnp.broadcast_to(mods[l, b, i * d:(i + 1) * d][None, None, :], (b, 1, d)) for i in range(6)]
        sh1, sc1, g1, sh2, sc2, g2 = lat_m
        csh1, csc1, cg1, csh2, csc2, cg2 = ctx_m

        if l % 2 == 0:
            e = l // 2
            w_in = att_w_in[e].astype(BF16)
            w_out = att_w_out[e].astype(BF16)
            p_lat = norm_mod_proj(x, norm_g[l, 0], sc1, sh1, w_in, BF16)
            p_ctx = norm_mod_proj(ctx, norm_g[l, 0], csc1, csh1, w_in, BF16)
            bias = na_bias_tables(na_rpb[e], t // GRID_W, l_ctx, LOG2E)
            oa_lat, oa_ctx = na_attention(p_lat, p_ctx, bias)
            ob_lat, ob_ctx = diff_attention(p_lat, p_ctx, cos, sin, diff_lambda[e], diff_subln_g[e], l)
            ws = [w_out[:A_W], w_out[A_W:]]
            x = outproj_residual([oa_lat, ob_lat], ws, x, g1)
            if not last:
                ctx = outproj_residual([oa_ctx, ob_ctx], ws, ctx, cg1)
        else:
            o = l // 2
            w_in = rec_w_in[o].astype(BF16)
            w_out = rec_w_out[o].astype(BF16)
            p_lat = norm_mod_proj(x, norm_g[l, 0], sc1, sh1, w_in, F32)
            p_ctx = norm_mod_proj(ctx, norm_g[l, 0], csc1, csh1, w_in, F32)
            y_lat, y_ctx = hgrn2_mixer(p_lat, p_ctx, rec_lb_logits, rec_gnorm_g[o], l)
            x = outproj_residual([y_lat], [w_out], x, g1)
            if not last:
                ctx = outproj_residual([y_ctx], [w_out], ctx, cg1)

        if last:
            ((xsel_l, slot_l, gate_l),) = route_gather([x], norm_g[l, 1], [sc2], [sh2], moe_router[l])
            (y_l,) = expert_ffn([xsel_l], moe_w_gate, moe_w_up, moe_w_down, l)
            x = combine_residual(y_l, slot_l, gate_l, x, g2, final_g)
        else:
            (xsel_l, slot_l, gate_l), (xsel_c, slot_c, gate_c) = route_gather(
                [x, ctx], norm_g[l, 1], [sc2, csc2], [sh2, csh2], moe_router[l])
            y_l, y_c = expert_ffn([xsel_l, xsel_c], moe_w_gate, moe_w_up, moe_w_down, l)
            x = combine_residual(y_l, slot_l, gate_l, x, g2)
            ctx = combine_residual(y_c, slot_c, gate_c, ctx, cg2)
    return x
```

```python
import functools
import math

import numpy as np
import jax
import jax.numpy as jnp
from jax import lax
from jax.experimental import pallas as pl
from jax.experimental.pallas import tpu as pltpu

F32 = jnp.float32
BF16 = jnp.bfloat16

D_MODEL = 1024
DEPTH = 4
GRID_W = 64
HEAD_DIM = 64
NA_HEADS = 8
NA_WIN_H = 8
NA_WIN_W = 16
DIFF_HEADS = 4
A_W = NA_HEADS * HEAD_DIM
ATT_IN_W = 3072
REC_HEADS = 8
REC_D = 128
N_EXPERTS = 16
EXPERT_FF = 2 * D_MODEL
EC_CAPACITY_FACTOR = 2
ROPE_THETA = 10000.0
NORM_EPS = 1e-6
NEG_INF = -1e30

LANES = 128
V7X_VMEM_BYTES = 64 * 1024 * 1024
MIB = 1024 * 1024

NA_QROWS = 4
NA_KROWS = NA_QROWS + NA_WIN_H - 1
NA_BLOCKS_PER_STEP = 2
DIFF_BLOCKS_PER_STEP = 2
REC_CHUNK = 64
REC_SUB = 16
REC_NSUB = REC_CHUNK // REC_SUB
REC_EXP_CLAMP = 60.0
REC_GROUP = 4
REC_GROUP_LATENT = 8
PREFIX_BLOCK = 256
THRESHOLD_RADIX_BITS = 3
ROUTE_EXPERTS_PER_STEP = 4
FFN_ROW_TILE = 512
FFN_HALVES = 2

LOG2E = math.log2(math.e)
SCORE_MULT = HEAD_DIM ** -0.5 * LOG2E

NT_DIMS = (((1,), (1,)), ((), ()))
TN_DIMS = (((0,), (0,)), ((), ()))


def _cparams(semantics, vmem_mib):
    assert vmem_mib * MIB < V7X_VMEM_BYTES
    return pltpu.CompilerParams(dimension_semantics=semantics, vmem_limit_bytes=vmem_mib * MIB)


def _sigmoid(x):
    return 1.0 / (1.0 + jnp.exp(-x))


def _rms(x, eps=NORM_EPS):
    return x * lax.rsqrt(jnp.mean(x * x, axis=-1, keepdims=True) + eps)


def _mod_kernel(c_ref, w_ref, b_ref, o_ref):
    c = c_ref[...]
    s = (c * _sigmoid(c)).astype(BF16)
    o_ref[0] = jnp.dot(s, w_ref[0].astype(BF16), preferred_element_type=F32) + b_ref[0]


def modulation_all(c_all, w_mod, b_mod):
    depth, d, n = w_mod.shape
    r = c_all.shape[0]
    tn = 1536
    return pl.pallas_call(
        _mod_kernel,
        grid=(depth, n // tn),
        in_specs=[
            pl.BlockSpec((r, d), lambda l, j: (0, 0)),
            pl.BlockSpec((1, d, tn), lambda l, j: (l, 0, j)),
            pl.BlockSpec((1, 1, tn), lambda l, j: (l, 0, j)),
        ],
        out_specs=pl.BlockSpec((1, r, tn), lambda l, j: (l, 0, j)),
        out_shape=jax.ShapeDtypeStruct((depth, r, n), F32),
        compiler_params=_cparams(("parallel", "parallel"), 32),
        name="modulation",
    )(c_all, w_mod, b_mod.reshape(depth, 1, n))


def _norm_mod(x, g, sc, sh):
    return _rms(x) * g * (1.0 + sc) + sh


def _proj_kernel(x_ref, g_ref, sc_ref, sh_ref, w_ref, o_ref, h_scr, *, tn):
    h_scr[...] = _norm_mod(x_ref[0], g_ref[...], sc_ref[0], sh_ref[0]).astype(BF16)
    n = w_ref.shape[1]
    for j in range(n // tn):
        o_ref[0, :, j * tn:(j + 1) * tn] = jnp.dot(
            h_scr[...], w_ref[:, j * tn:(j + 1) * tn], preferred_element_type=F32).astype(o_ref.dtype)


def norm_mod_proj(x, g, sc, sh, w_bf16, out_dtype):
    b, t, d = x.shape
    n = w_bf16.shape[1]
    tm = min(t, 512)
    return pl.pallas_call(
        functools.partial(_proj_kernel, tn=512),
        grid=(b, t // tm),
        in_specs=[
            pl.BlockSpec((1, tm, d), lambda i, j: (i, j, 0)),
            pl.BlockSpec((1, d), lambda i, j: (0, 0)),
            pl.BlockSpec((1, 1, d), lambda i, j: (i, 0, 0)),
            pl.BlockSpec((1, 1, d), lambda i, j: (i, 0, 0)),
            pl.BlockSpec((d, n), lambda i, j: (0, 0)),
        ],
        out_specs=pl.BlockSpec((1, tm, n), lambda i, j: (i, j, 0)),
        out_shape=jax.ShapeDtypeStruct((b, t, n), out_dtype),
        scratch_shapes=[pltpu.VMEM((tm, d), BF16)],
        compiler_params=_cparams(("parallel", "parallel"), 56),
        name="norm_mod_proj",
    )(x, g.reshape(1, d), sc, sh, w_bf16)


def _outproj_kernel(*refs, n_in):
    o_refs, w_refs = refs[:n_in], refs[n_in:2 * n_in]
    x_ref, gate_ref, out_ref = refs[2 * n_in:]
    acc = jnp.dot(o_refs[0][0], w_refs[0][...], preferred_element_type=F32)
    for o_ref, w_ref in zip(o_refs[1:], w_refs[1:]):
        acc = acc + jnp.dot(o_ref[0], w_ref[...], preferred_element_type=F32)
    out_ref[0] = x_ref[0] + gate_ref[0] * acc


def outproj_residual(os_, ws, x, gate):
    b, t, d = x.shape
    tm = min(t, 512)
    n_in = len(os_)
    in_specs = [pl.BlockSpec((1, tm, o.shape[2]), lambda i, j: (i, j, 0)) for o in os_]
    in_specs += [pl.BlockSpec(w.shape, lambda i, j: (0, 0)) for w in ws]
    in_specs += [pl.BlockSpec((1, tm, d), lambda i, j: (i, j, 0)),
                 pl.BlockSpec((1, 1, d), lambda i, j: (i, 0, 0))]
    return pl.pallas_call(
        functools.partial(_outproj_kernel, n_in=n_in),
        grid=(b, t // tm),
        in_specs=in_specs,
        out_specs=pl.BlockSpec((1, tm, d), lambda i, j: (i, j, 0)),
        out_shape=jax.ShapeDtypeStruct((b, t, d), F32),
        compiler_params=_cparams(("parallel", "parallel"), 32),
        name="outproj_residual",
    )(*os_, *ws, x, gate)


def _na_block_tables(rows):
    kh = min(NA_WIN_H, rows)
    assert kh == NA_WIN_H and rows % NA_QROWS == 0 and rows >= NA_KROWS
    n_qb = rows // NA_QROWS
    starts = [min(max(qb * NA_QROWS - kh // 2, 0), rows - NA_KROWS) for qb in range(n_qb)]
    tbl_of, reps = [], []
    for qb in range(n_qb):
        r0 = qb * NA_QROWS
        interior = (r0 - kh // 2 >= 0) and (r0 + NA_QROWS - 1 - kh // 2 <= rows - kh) and starts[qb] == r0 - kh // 2
        key = "interior" if interior else qb
        if key not in reps:
            reps.append(key)
        tbl_of.append(reps.index(key))
    rep_qb = [next(qb for qb in range(n_qb) if tbl_of[qb] == i) for i in range(len(reps))]
    tiles = []
    for qb in rep_qb:
        tiles.append([])
        for qi in range(NA_QROWS):
            r = qb * NA_QROWS + qi
            rs = min(max(r - kh // 2, 0), rows - kh)
            tiles[-1].append([(starts[qb] + kk - r + NA_WIN_H - 1) if rs <= starts[qb] + kk < rs + kh else None
                              for kk in range(NA_KROWS)])
    return starts, tbl_of, tiles


def na_bias_tables(rpb, rows, ctx_len, mult):
    _, _, tiles = _na_block_tables(rows)
    c = np.arange(GRID_W)
    cs = np.clip(c - NA_WIN_W // 2, 0, GRID_W - NA_WIN_W)
    col_ok = (c[None, :] >= cs[:, None]) & (c[None, :] < cs[:, None] + NA_WIN_W)
    dcol = c[None, :] - c[:, None] + NA_WIN_W - 1
    pick = ((dcol[None] == np.arange(2 * NA_WIN_W - 1)[:, None, None]) & col_ok[None]).astype(np.float32)
    toe = jnp.einsum("hrd,dqk->hrqk", rpb.astype(F32), jnp.asarray(pick), precision=lax.Precision.HIGHEST)
    toe = jnp.where(jnp.asarray(col_ok), toe * mult, NEG_INF)
    neg = jnp.full((rpb.shape[0], GRID_W, GRID_W), NEG_INF, F32)
    tables = [jnp.concatenate([jnp.concatenate([neg if d is None else toe[:, d] for d in row], axis=-1)
                               for row in tbl], axis=-2) for tbl in tiles]
    loc = jnp.stack(tables, axis=1)
    return jnp.concatenate([loc, jnp.zeros(loc.shape[:3] + (ctx_len,), F32)], axis=-1)


def _scores_many(qs, ks):
    return [lax.dot_general(q, k, NT_DIMS, preferred_element_type=F32) for q, k in zip(qs, ks)]


def _softmax_pv_many(s2, vs, biases=None):
    if biases is not None:
        s2 = [s + b for s, b in zip(s2, biases)]
    ps = [jnp.exp2(s - jnp.max(s, axis=-1, keepdims=True)) for s in s2]
    inv = [1.0 / jnp.sum(p, axis=-1, keepdims=True) for p in ps]
    return [jnp.dot(p.astype(BF16), v, preferred_element_type=F32) * r for p, v, r in zip(ps, vs, inv)]


def _head_half_scales(mult):
    first = lax.broadcasted_iota(jnp.int32, (1, LANES), 1) < HEAD_DIM
    return first, jnp.where(first, mult, 0.0), jnp.where(first, 0.0, mult)


def _na_kernel(q_ref, k_ref, v_ref, qc_ref, kc_ref, vc_ref, bias_ref, o_ref, oc_ref, *, starts, tbl_of):
    first, *halves = _head_half_scales(SCORE_MULT)
    nq, nk = NA_QROWS * GRID_W, NA_KROWS * GRID_W
    kc, vc = kc_ref[0], vc_ref[0]
    n_qb = len(starts)
    assert n_qb % NA_BLOCKS_PER_STEP == 0

    def group_scores(qb0):
        qs, ks = [], []
        for qb in range(qb0, qb0 + NA_BLOCKS_PER_STEP):
            q = q_ref[0, qb * nq:(qb + 1) * nq, :].astype(F32)
            k_all = jnp.concatenate([k_ref[0, starts[qb] * GRID_W:starts[qb] * GRID_W + nk, :], kc], axis=0)
            for hh in range(2):
                qs.append((q * halves[hh]).astype(BF16))
                ks.append(k_all)
        return _scores_many(qs, ks)

    s_next = group_scores(0)
    for qb0 in range(0, n_qb, NA_BLOCKS_PER_STEP):
        blocks = range(qb0, qb0 + NA_BLOCKS_PER_STEP)
        s_cur = s_next
        if qb0 + NA_BLOCKS_PER_STEP < n_qb:
            s_next = group_scores(qb0 + NA_BLOCKS_PER_STEP)
        vs, bs = [], []
        for qb in blocks:
            v_all = jnp.concatenate([v_ref[0, starts[qb] * GRID_W:starts[qb] * GRID_W + nk, :], vc], axis=0)
            for hh in range(2):
                vs.append(v_all)
                bs.append(bias_ref[hh, tbl_of[qb]])
        outs = _softmax_pv_many(s_cur, vs, bs)
        for i, qb in enumerate(blocks):
            o_ref[0, qb * nq:(qb + 1) * nq, :] = jnp.where(first, outs[2 * i], outs[2 * i + 1]).astype(o_ref.dtype)
    qc = qc_ref[0].astype(F32)
    outs = _softmax_pv_many(_scores_many([(qc * h).astype(BF16) for h in halves], [kc, kc]), [vc, vc])
    oc_ref[0] = jnp.where(first, outs[0], outs[1]).astype(oc_ref.dtype)


def na_attention(p_lat, p_ctx, bias):
    b, t, _ = p_lat.shape
    l = p_ctx.shape[1]
    rows = t // GRID_W
    starts, tbl_of, _ = _na_block_tables(rows)
    n_pairs = NA_HEADS // 2
    kb, vb = 2 * A_W // LANES, 3 * A_W // LANES
    lat = lambda off: pl.BlockSpec((1, t, LANES), lambda j, i: (i, 0, off + j))
    cx = lambda off: pl.BlockSpec((1, l, LANES), lambda j, i: (i, 0, off + j))
    n_tbl, nq, nkc = bias.shape[1:]
    return pl.pallas_call(
        functools.partial(_na_kernel, starts=tuple(starts), tbl_of=tuple(tbl_of)),
        grid=(n_pairs, b),
        in_specs=[lat(0), lat(kb), lat(vb), cx(0), cx(kb), cx(vb),
                  pl.BlockSpec((2, n_tbl, nq, nkc), lambda j, i: (j, 0, 0, 0))],
        out_specs=[pl.BlockSpec((1, t, LANES), lambda j, i: (i, 0, j)),
                   pl.BlockSpec((1, l, LANES), lambda j, i: (i, 0, j))],
        out_shape=[jax.ShapeDtypeStruct((b, t, A_W), BF16), jax.ShapeDtypeStruct((b, l, A_W), BF16)],
        compiler_params=_cparams(("parallel", "parallel"), 48),
        name="na_attention",
    )(p_lat, p_lat, p_lat, p_ctx, p_ctx, p_ctx, bias)


def rope_tables(n):
    t = np.arange(n)
    rows, cols = (t // GRID_W).astype(np.float64), (t % GRID_W).astype(np.float64)
    n_freq = HEAD_DIM // 4
    inv = ROPE_THETA ** (-np.arange(n_freq, dtype=np.float64) / n_freq)
    ang = np.concatenate([rows[:, None] * inv, cols[:, None] * inv], axis=-1)
    cos = np.tile(np.cos(ang), (1, 4))
    sin = np.tile(np.concatenate([-np.sin(ang), np.sin(ang)], axis=-1), (1, 2))
    return jnp.asarray(cos, F32), jnp.asarray(sin, F32)


def _rope(a, cos, sin_signed):
    lane = lax.broadcasted_iota(jnp.int32, (1, LANES), 1)
    low = (lane % HEAD_DIM) < HEAD_DIM // 2
    partner = jnp.where(low, pltpu.roll(a, LANES - HEAD_DIM // 2, 1), pltpu.roll(a, HEAD_DIM // 2, 1))
    return a * cos + partner * sin_signed


def _diff_kernel(q_ref, k_ref, v_ref, qc_ref, kc_ref, vc_ref, cos_ref, sin_ref, lam_ref, gcol_ref,
                 o_ref, oc_ref, q1_scr, q2_scr, k_scr, vt_scr, *, lam_init, tq):
    t, l = q_ref.shape[1], kc_ref.shape[1]
    _, half1, half2 = _head_half_scales(SCORE_MULT)
    lv = lam_ref[...]
    lam = (jnp.exp(jnp.sum(lv[0:1] * lv[1:2], axis=-1, keepdims=True))
           - jnp.exp(jnp.sum(lv[2:3] * lv[3:4], axis=-1, keepdims=True)) + lam_init)
    cos, sin = cos_ref[...], sin_ref[...]
    q = _rope(q_ref[0].astype(F32), cos, sin)
    q1_scr[...] = (q * half1).astype(BF16)
    q2_scr[...] = (q * half2).astype(BF16)
    k_scr[0:t, :] = _rope(k_ref[0].astype(F32), cos, sin).astype(BF16)
    k_scr[t:t + l, :] = kc_ref[0]
    for i in range(t // tq):
        vt_scr[:, i * tq:(i + 1) * tq] = v_ref[0, i * tq:(i + 1) * tq, :].astype(F32).T.astype(BF16)
    vct = vc_ref[0].astype(F32).T.astype(BF16)
    vt_scr[:, t:t + l] = vct
    gain = gcol_ref[...] * (1.0 - lam_init)

    def scores(q_pairs, k):
        return [lax.dot_general(k, q, NT_DIMS, preferred_element_type=F32) for pair in q_pairs for q in pair]

    def finish(s2, vt):
        ps = [jnp.exp2(s - jnp.max(s, axis=0, keepdims=True)) for s in s2]
        inv = [1.0 / jnp.sum(p, axis=0, keepdims=True) for p in ps]
        ot = [jnp.dot(vt, p.astype(BF16), preferred_element_type=F32) * r for p, r in zip(ps, inv)]
        outs = []
        for i in range(len(s2) // 2):
            d = ot[2 * i] - lam * ot[2 * i + 1]
            d = d * lax.rsqrt(jnp.mean(d * d, axis=0, keepdims=True) + NORM_EPS) * gain
            outs.append(d.T)
        return outs

    step = tq * DIFF_BLOCKS_PER_STEP
    group = lambda r0: [(q1_scr[r:r + tq, :], q2_scr[r:r + tq, :]) for r in range(r0, r0 + step, tq)]
    k_all, vt_all = k_scr[...], vt_scr[...]
    s_next = scores(group(0), k_all)
    for r0 in range(0, t, step):
        s_cur = s_next
        if r0 + step < t:
            s_next = scores(group(r0 + step), k_all)
        for j, o in enumerate(finish(s_cur, vt_all)):
            o_ref[0, r0 + j * tq:r0 + (j + 1) * tq, :] = o.astype(o_ref.dtype)
    qc = qc_ref[0].astype(F32)
    (oc,) = finish(scores([((qc * half1).astype(BF16), (qc * half2).astype(BF16))], kc_ref[0]), vct)
    oc_ref[0] = oc.astype(oc_ref.dtype)


def diff_attention(p_lat, p_ctx, cos, sin, lam_vecs, subln_g, layer_idx):
    b, t, _ = p_lat.shape
    l = p_ctx.shape[1]
    lam_init = 0.8 - 0.6 * math.exp(-0.3 * layer_idx)
    qb, kb, vb = A_W // LANES, 4 * A_W // LANES, 5 * A_W // LANES
    lat = lambda off: pl.BlockSpec((1, t, LANES), lambda i, j: (i, 0, off + j))
    cx = lambda off: pl.BlockSpec((1, l, LANES), lambda i, j: (i, 0, off + j))
    const = lambda shape: pl.BlockSpec(shape, lambda i, j: (0, 0))
    tq = 256
    assert t % (tq * DIFF_BLOCKS_PER_STEP) == 0
    return pl.pallas_call(
        functools.partial(_diff_kernel, lam_init=lam_init, tq=tq),
        grid=(b, DIFF_HEADS),
        in_specs=[lat(qb), lat(kb), lat(vb), cx(qb), cx(kb), cx(vb),
                  const((t, LANES)), const((t, LANES)), const(lam_vecs.shape), const((LANES, 1))],
        out_specs=[pl.BlockSpec((1, t, LANES), lambda i, j: (i, 0, j)),
                   pl.BlockSpec((1, l, LANES), lambda i, j: (i, 0, j))],
        out_shape=[jax.ShapeDtypeStruct((b, t, A_W), BF16), jax.ShapeDtypeStruct((b, l, A_W), BF16)],
        scratch_shapes=[pltpu.VMEM((t, LANES), BF16), pltpu.VMEM((t, LANES), BF16),
                        pltpu.VMEM((t + l, LANES), BF16), pltpu.VMEM((LANES, t + l), BF16)],
        compiler_params=_cparams(("parallel", "parallel"), 48),
        name="diff_attention",
    )(p_lat, p_lat, p_lat, p_ctx, p_ctx, p_ctx, cos, sin, lam_vecs, subln_g.reshape(LANES, 1))


def _split2(x):
    a = x.astype(BF16)
    return a, (x - a.astype(F32)).astype(BF16)


def _rec_masks(forward):
    t = lax.broadcasted_iota(jnp.int32, (REC_CHUNK, REC_CHUNK), 0)
    s = lax.broadcasted_iota(jnp.int32, (REC_CHUNK, REC_CHUNK), 1)
    incl = (s <= t) if forward else (s >= t)
    return jnp.where(incl, 1.0, 0.0).astype(BF16), incl


def _rec_group_local(qs, fxs, vb, lbs):
    c = REC_CHUNK
    n_grp = qs.shape[0] // c
    items = [(d, j) for j in range(n_grp) for d in range(2)]
    rows = lambda a, j: a[j * c:(j + 1) * c]
    masks = [_rec_masks(True), _rec_masks(False)]
    kks, g3s = [], []
    for fx, lb in zip(fxs, lbs):
        e = jnp.exp(-jnp.abs(fx))
        r = 1.0 / (1.0 + e)
        pos = fx >= 0
        f = lb + (1.0 - lb) * jnp.where(pos, r, e * r)
        kks.append((1.0 - lb) * jnp.where(pos, e * r, r))
        g3s.append(jnp.concatenate(_split2(jnp.log(f)), axis=1))
    cs = {(d, j): jnp.dot(masks[d][0], rows(g3s[d], j), preferred_element_type=F32) for d, j in items}
    row_sub = lax.broadcasted_iota(jnp.int32, (c, 1), 0) // REC_SUB
    band = lambda a, p: a[p * REC_SUB:(p + 1) * REC_SUB]
    subs = range(REC_NSUB)
    qcat, kcat, qtil, kdec, dec = {}, {}, {}, {}, {}
    for d, j in items:
        x = cs[d, j]
        x = x[:, 0:LANES] + x[:, LANES:2 * LANES]
        bcum = x
        q, kk = rows(qs, j), rows(kks[d], j)
        if d == 0:
            ends = [bcum[(p + 1) * REC_SUB - 1:(p + 1) * REC_SUB] for p in subs]
            starts = [jnp.zeros((1, LANES), F32)] + ends[:-1]
            blast = ends[-1]
        else:
            ends = [bcum[p * REC_SUB:p * REC_SUB + 1] for p in subs]
            starts = ends[1:] + [jnp.zeros((1, LANES), F32)]
            blast = ends[0]
        ends_mat = jnp.concatenate(ends, axis=0)
        per_row = lambda bounds: jnp.concatenate([jnp.broadcast_to(b_, (REC_SUB, LANES)) for b_ in bounds], axis=0)
        qhat = q * jnp.exp(bcum - per_row(starts))
        bend = per_row(ends)
        ke = kk * jnp.exp(bend - bcum)
        qcat[d, j] = jnp.concatenate([jnp.where(row_sub == p, qhat, 0.0) for p in subs], axis=1).astype(BF16)
        grow = [jnp.exp(jnp.minimum(starts[p] - ends_mat, REC_EXP_CLAMP)) for p in subs]
        kcat[d, j] = jnp.concatenate(
            [jnp.concatenate([band(ke, s) * grow[p][s:s + 1] for s in subs], axis=0) for p in subs],
            axis=1).astype(BF16)
        from_start = jnp.exp(jnp.concatenate(starts, axis=0))
        to_last = jnp.exp(blast - ends_mat)
        qtil[d, j] = jnp.concatenate([band(qhat, p) * from_start[p:p + 1] for p in subs], axis=0).astype(BF16)
        kdec[d, j] = jnp.concatenate([band(ke, p) * to_last[p:p + 1] for p in subs], axis=0).astype(BF16)
        dec[d, j] = jnp.exp(blast)
    sc = {k: lax.dot_general(qcat[k], kcat[k], NT_DIMS, preferred_element_type=F32) for k in items}
    u2 = [lax.dot_general(rows(vb, j), jnp.concatenate([kdec[0, j], kdec[1, j]], axis=1), TN_DIMS,
                          preferred_element_type=F32) for j in range(n_grp)]
    u = {(d, j): u2[j][:, d * LANES:(d + 1) * LANES] for d, j in items}
    scb = {(d, j): jnp.where(masks[d][1], sc[d, j], 0.0).astype(BF16) for d, j in items}
    o_intra = {(d, j): jnp.dot(scb[d, j], rows(vb, j), preferred_element_type=F32) for d, j in items}
    return {k: (o_intra[k], qtil[k], u[k], dec[k]) for k in items}


def _rec_kernel(q_ref, ff_ref, fb_ref, v_ref, g_ref, qc_ref, ffc_ref, fbc_ref, vc_ref, gc_ref,
                lb_ref, gn_ref, y_ref, yc_ref, o_scr, qt_scr, u_scr, dec_scr, *, layer_idx):
    t, l = q_ref.shape[1], qc_ref.shape[1]
    c, grp = REC_CHUNK, REC_GROUP

    def lower_bound(d):
        lg = lb_ref[d]
        ex = jnp.exp(lg - jnp.max(lg, axis=0, keepdims=True))
        return jnp.sum(ex[1:layer_idx + 1], axis=0, keepdims=True) / jnp.sum(ex, axis=0, keepdims=True)

    lbs = (lower_bound(0), lower_bound(1))

    def local_pass(q_r, f_refs, v_r, n_tok, tok0, grp):
        def body(i, carry):
            r0 = pl.multiple_of(i * (grp * c), grp * c)
            q = q_r[0, pl.ds(r0, grp * c), :]
            qs = q * _sigmoid(q)
            vb = v_r[0, pl.ds(r0, grp * c), :].astype(BF16)
            fxs = [f_r[0, pl.ds(r0, grp * c), :] for f_r in f_refs]
            local = _rec_group_local(qs, fxs, vb, lbs)
            for j in range(grp):
                row = pl.multiple_of(tok0 + r0 + j * c, c)
                ci = tok0 // c + i * grp + j
                for d in range(2):
                    o_intra, qtil, u, dec = local[d, j]
                    o_scr[d, pl.ds(row, c), :] = o_intra
                    qt_scr[d, pl.ds(row, c), :] = qtil
                    u_scr[d, ci] = u
                    dec_scr[d, pl.ds(ci, 1), :] = dec
            return carry
        lax.fori_loop(0, n_tok // (grp * c), body, 0)

    def state_pass(c_lo, n, states):
        def body(i, carry):
            sts = list(carry)
            for k in range(grp):
                step = i * grp + k
                for d, ci in ((0, c_lo + step), (1, c_lo + n - 1 - step)):
                    row = pl.multiple_of(ci * c, c)
                    o_scr[d, pl.ds(row, c), :] += lax.dot_general(
                        qt_scr[d, pl.ds(row, c), :], sts[d].astype(BF16), NT_DIMS, preferred_element_type=F32)
                    sts[d] = sts[d] * dec_scr[d, pl.ds(ci, 1), :] + u_scr[d, ci]
            return tuple(sts)
        return lax.fori_loop(0, n // grp, body, states)

    def finish(tok0, n_tok, gate_ref, out_ref):
        o = o_scr[0, tok0:tok0 + n_tok, :] + o_scr[1, tok0:tok0 + n_tok, :]
        gate = gate_ref[0]
        out_ref[0] = ((_rms(o) * gn_ref[...]) * (gate * _sigmoid(gate))).astype(out_ref.dtype)

    local_pass(qc_ref, (ffc_ref, fbc_ref), vc_ref, l, 0, REC_GROUP)
    local_pass(q_ref, (ff_ref, fb_ref), v_ref, t, l, REC_GROUP_LATENT)
    zero = jnp.zeros((REC_D, REC_D), F32)
    states = state_pass(0, l // c, (zero, zero))
    state_pass(l // c, t // c, states)
    finish(0, l, gc_ref, yc_ref)
    finish(l, t, g_ref, y_ref)


def hgrn2_mixer(p_lat, p_ctx, lb_logits, gnorm_g, layer_idx):
    b, t, _ = p_lat.shape
    l = p_ctx.shape[1]
    assert t % (REC_GROUP_LATENT * REC_CHUNK) == 0 and l % (REC_GROUP * REC_CHUNK) == 0
    n_chunks = (t + l) // REC_CHUNK
    lat = lambda k: pl.BlockSpec((1, t, REC_D), lambda i, j: (i, 0, k * REC_HEADS + j))
    cx = lambda k: pl.BlockSpec((1, l, REC_D), lambda i, j: (i, 0, k * REC_HEADS + j))
    return pl.pallas_call(
        functools.partial(_rec_kernel, layer_idx=layer_idx),
        grid=(b, REC_HEADS),
        in_specs=[lat(k) for k in range(5)] + [cx(k) for k in range(5)] + [
            pl.BlockSpec((2, DEPTH, REC_D), lambda i, j: (0, 0, j)),
            pl.BlockSpec((1, REC_D), lambda i, j: (0, 0))],
        out_specs=[pl.BlockSpec((1, t, REC_D), lambda i, j: (i, 0, j)),
                   pl.BlockSpec((1, l, REC_D), lambda i, j: (i, 0, j))],
        out_shape=[jax.ShapeDtypeStruct((b, t, D_MODEL), BF16), jax.ShapeDtypeStruct((b, l, D_MODEL), BF16)],
        scratch_shapes=[pltpu.VMEM((2, t + l, REC_D), F32), pltpu.VMEM((2, t + l, REC_D), BF16),
                        pltpu.VMEM((2, n_chunks, REC_D, REC_D), F32), pltpu.VMEM((2, n_chunks, REC_D), F32)],
        compiler_params=_cparams(("parallel", "parallel"), 40),
        name="hgrn2_mixer",
    )(*([p_lat] * 5), *([p_ctx] * 5), lb_logits, gnorm_g.reshape(1, REC_D))


def _prefix_count(flags):
    e, t = flags.shape
    blk = min(PREFIX_BLOCK, t)
    ones = jnp.where(flags, 1.0, 0.0)
    s = lax.broadcasted_iota(jnp.int32, (blk, blk), 0)
    u = lax.broadcasted_iota(jnp.int32, (blk, blk), 1)
    upper = jnp.where(s < u, 1.0, 0.0).astype(BF16)
    carry = jnp.zeros((e, 1), F32)
    outs = []
    for i in range(t // blk):
        xs = ones[:, i * blk:(i + 1) * blk]
        outs.append(jnp.dot(xs.astype(BF16), upper, preferred_element_type=F32) + carry)
        carry = carry + jnp.sum(xs, axis=1, keepdims=True)
    return jnp.concatenate(outs, axis=1)


def _capacity_slots(affs, caps):
    as_float = lambda bits: lax.bitcast_convert_type(bits, F32)
    capf = [jnp.float32(c) for c in caps]
    thr = [jnp.zeros((N_EXPERTS, 1), jnp.int32) for _ in affs]
    for hi in range(30, -1, -THRESHOLD_RADIX_BITS):
        lo = max(hi - THRESHOLD_RADIX_BITS + 1, 0)
        cands = [[t | jnp.int32(m << lo) for m in range(1, 1 << (hi - lo + 1))] for t in thr]
        cnts = [[jnp.sum(jnp.where(a >= as_float(cand), 1.0, 0.0), axis=1, keepdims=True) for cand in cs]
                for a, cs in zip(affs, cands)]
        for i in range(len(affs)):
            for cand, cnt in zip(cands[i], cnts[i]):
                thr[i] = jnp.where(cnt >= capf[i], cand, thr[i])
    slots = []
    for a, t, c in zip(affs, thr, capf):
        gt = a > as_float(t)
        eq = a == as_float(t)
        need = c - jnp.sum(jnp.where(gt, 1.0, 0.0), axis=1, keepdims=True)
        sel = gt | (eq & (_prefix_count(eq) < need))
        slots.append(jnp.where(sel, _prefix_count(sel), -1.0))
    return slots


def _route_kernel(*refs, caps):
    n = len(caps)
    x_refs, g_ref = refs[:n], refs[n]
    sc_refs, sh_refs, wr_ref = refs[n + 1:2 * n + 1], refs[2 * n + 1:3 * n + 1], refs[3 * n + 1]
    outs = refs[3 * n + 2:]
    xsel_refs, slot_refs, gate_refs = outs[:n], outs[n:2 * n], outs[2 * n:3 * n]
    hb_scrs, slot_scrs = outs[3 * n:4 * n], outs[4 * n:5 * n]
    e = pl.program_id(1)

    @pl.when(e == 0)
    def _():
        w_hi, w_lo = _split2(wr_ref[...])
        w2 = jnp.concatenate([w_hi, w_lo], axis=0)
        affs = []
        for x_ref, sc_ref, sh_ref, hb_scr in zip(x_refs, sc_refs, sh_refs, hb_scrs):
            h = _norm_mod(x_ref[0], g_ref[...], sc_ref[0], sh_ref[0])
            hb = h.astype(BF16)
            hb_scr[...] = hb
            h_lo = (h - hb.astype(F32)).astype(BF16)
            hi_terms = lax.dot_general(w2, hb, NT_DIMS, preferred_element_type=F32)
            logits = (hi_terms[:N_EXPERTS] + hi_terms[N_EXPERTS:]
                      + lax.dot_general(w_hi, h_lo, NT_DIMS, preferred_element_type=F32))
            ex = jnp.exp(logits - jnp.max(logits, axis=0, keepdims=True))
            affs.append(ex / jnp.sum(ex, axis=0, keepdims=True))
        for aff, slot, slot_scr, slot_ref, gate_ref in zip(affs, _capacity_slots(affs, caps), slot_scrs, slot_refs,
                                                           gate_refs):
            slot_scr[...] = slot
            slot_ref[0] = slot
            gate_ref[0] = aff

    for cap, xsel_ref, hb_scr, slot_scr in zip(caps, xsel_refs, hb_scrs, slot_scrs):
        row = lax.broadcasted_iota(jnp.int32, (cap, hb_scr.shape[0]), 0).astype(F32)
        picks = [jnp.where(row == slot_scr[pl.ds(e * ROUTE_EXPERTS_PER_STEP + i, 1), :], 1.0, 0.0).astype(BF16)
                 for i in range(ROUTE_EXPERTS_PER_STEP)]
        for i, pick in enumerate(picks):
            xsel_ref[i] = jnp.dot(pick, hb_scr[...], preferred_element_type=F32).astype(xsel_ref.dtype)


def route_gather(xs, g, scs, shs, w_router):
    b, _, d = xs[0].shape
    ts = [x.shape[1] for x in xs]
    caps = [EC_CAPACITY_FACTOR * t // N_EXPERTS for t in ts]
    n = len(xs)
    vec = pl.BlockSpec((1, 1, d), lambda i, e: (i, 0, 0))
    per_sample = lambda t: pl.BlockSpec((1, N_EXPERTS, t), lambda i, e: (i, 0, 0))
    outs = pl.pallas_call(
        functools.partial(_route_kernel, caps=tuple(caps)),
        grid=(b, N_EXPERTS // ROUTE_EXPERTS_PER_STEP),
        in_specs=([pl.BlockSpec((1, t, d), lambda i, e: (i, 0, 0)) for t in ts]
                  + [pl.BlockSpec((1, d), lambda i, e: (0, 0))] + [vec] * (2 * n)
                  + [pl.BlockSpec((N_EXPERTS, d), lambda i, e: (0, 0))]),
        out_specs=([pl.BlockSpec((ROUTE_EXPERTS_PER_STEP, c, d), lambda i, e: (e, i, 0)) for c in caps]
                   + [per_sample(t) for t in ts] * 2),
        out_shape=([jax.ShapeDtypeStruct((N_EXPERTS, b * c, d), BF16) for c in caps]
                   + [jax.ShapeDtypeStruct((b, N_EXPERTS, t), F32) for t in ts] * 2),
        scratch_shapes=([pltpu.VMEM((t, d), BF16) for t in ts] + [pltpu.VMEM((N_EXPERTS, t), F32) for t in ts]),
        compiler_params=_cparams(("parallel", "arbitrary"), 48),
        name="route_gather",
    )(*xs, g.reshape(1, d), *scs, *shs, w_router.T)
    return [(outs[i], outs[n + i], outs[2 * n + i]) for i in range(n)]


def _ffn_kernel(*refs, n_lat_tiles, has_ctx, layer_idx):
    if has_ctx:
        xl_ref, xc_ref, wg_hbm, wu_hbm, wd_hbm, yl_ref, yc_ref, wg_s, wu_s, wd_s, stage, sem = refs
    else:
        xl_ref, wg_hbm, wu_hbm, wd_hbm, yl_ref, wg_s, wu_s, wd_s, stage, sem = refs
    e, r = pl.program_id(0), pl.program_id(1)
    half = stage.shape[0]
    w_hbm, w_s = (wg_hbm, wu_hbm, wd_hbm), (wg_s, wu_s, wd_s)

    def chunk_copy(expert, c):
        m, h = divmod(c, FFN_HALVES)
        if m < 2:
            src = w_hbm[m].at[layer_idx, expert, :, pl.ds(h * half, half)]
        else:
            src = w_hbm[m].at[layer_idx, expert, pl.ds(h * half, half), :]
        return pltpu.make_async_copy(src, stage, sem.at[0])

    def chunk_store(slot, c):
        m, h = divmod(c, FFN_HALVES)
        if m < 2:
            w_s[m][slot, :, h * half:(h + 1) * half] = stage[...].astype(BF16)
        else:
            w_s[m][slot, h * half:(h + 1) * half, :] = stage[...].astype(BF16)

    n_chunks = 3 * FFN_HALVES

    @pl.when((e == 0) & (r == 0))
    def _():
        for c in range(n_chunks):
            cp = chunk_copy(0, c)
            cp.start()
            cp.wait()
            chunk_store(0, c)

    nxt = e + 1
    prefetch = nxt < pl.num_programs(0)
    for c in range(n_chunks):
        @pl.when(prefetch & (r == c + 1))
        def _():
            chunk_copy(nxt, c).wait()
            chunk_store(nxt % 2, c)

        @pl.when(prefetch & (r == c))
        def _():
            chunk_copy(nxt, c).start()

    slot = e % 2

    def swiglu(x_ref, y_ref):
        x = x_ref[0]
        a = jnp.dot(x, wg_s[slot], preferred_element_type=F32)
        u = jnp.dot(x, wu_s[slot], preferred_element_type=F32)
        hid = ((a * _sigmoid(a)) * u).astype(BF16)
        y_ref[0] = jnp.dot(hid, wd_s[slot], preferred_element_type=F32).astype(y_ref.dtype)

    if has_ctx:
        @pl.when(r < n_lat_tiles)
        def _():
            swiglu(xl_ref, yl_ref)

        @pl.when(r == n_lat_tiles)
        def _():
            swiglu(xc_ref, yc_ref)
    else:
        swiglu(xl_ref, yl_ref)


def expert_ffn(xsels, w_gate, w_up, w_down, layer_idx):
    d, ff = w_gate.shape[2], w_gate.shape[3]
    has_ctx = len(xsels) == 2
    tm = min(FFN_ROW_TILE, xsels[1].shape[1]) if has_ctx else FFN_ROW_TILE
    while xsels[0].shape[1] // tm + has_ctx <= 3 * FFN_HALVES:
        tm //= 2
    n_lat_tiles = xsels[0].shape[1] // tm
    n_tiles = n_lat_tiles + (1 if has_ctx else 0)
    assert xsels[0].shape[1] % tm == 0 and ff == FFN_HALVES * d and n_tiles > 3 * FFN_HALVES
    if has_ctx:
        assert xsels[1].shape[1] == tm
    lat_spec = pl.BlockSpec((1, tm, d), lambda e, r: (e, jnp.minimum(r, n_lat_tiles - 1), 0))
    ctx_spec = pl.BlockSpec((1, tm, d), lambda e, r: (e, 0, 0))
    io_specs = [lat_spec, ctx_spec] if has_ctx else [lat_spec]
    hbm = pl.BlockSpec(memory_space=pl.ANY)
    return pl.pallas_call(
        functools.partial(_ffn_kernel, n_lat_tiles=n_lat_tiles, has_ctx=has_ctx, layer_idx=layer_idx),
        grid=(N_EXPERTS, n_tiles),
        in_specs=io_specs + [hbm, hbm, hbm],
        out_specs=io_specs,
        out_shape=[jax.ShapeDtypeStruct(x.shape, BF16) for x in xsels],
        scratch_shapes=[pltpu.VMEM((2, d, ff), BF16), pltpu.VMEM((2, d, ff), BF16), pltpu.VMEM((2, ff, d), BF16),
                        pltpu.VMEM((d, d), F32), pltpu.SemaphoreType.DMA((1,))],
        compiler_params=_cparams(("arbitrary", "arbitrary"), 56),
        name="expert_ffn",
    )(*xsels, w_gate, w_up, w_down)


def _combine_kernel(y_ref, slot_ref, gate_ref, x_ref, g2_ref, fg_ref, out_ref, *, final_norm):
    n_e, cap, d = y_ref.shape
    tt = x_ref.shape[1]
    slot, gate = slot_ref[0], gate_ref[0]
    row = lax.broadcasted_iota(jnp.int32, (cap, tt), 0).astype(F32)
    pg = jnp.concatenate([jnp.where(row == slot[e:e + 1], gate[e:e + 1], 0.0).astype(BF16) for e in range(n_e)],
                         axis=0)
    moe = lax.dot_general(pg, y_ref[...].reshape(n_e * cap, d), TN_DIMS, preferred_element_type=F32)
    xn = x_ref[0] + g2_ref[0] * moe
    if final_norm:
        xn = _rms(xn) * fg_ref[...]
    out_ref[0] = xn


def combine_residual(y_sel, slot, gate, x, g2, final_g=None):
    b, t, d = x.shape
    cap = EC_CAPACITY_FACTOR * t // N_EXPERTS
    tt = min(t, 512)
    final_norm = final_g is not None
    fg = (final_g if final_norm else jnp.ones((d,), F32)).reshape(1, d)
    return pl.pallas_call(
        functools.partial(_combine_kernel, final_norm=final_norm),
        grid=(b, t // tt),
        in_specs=[
            pl.BlockSpec((N_EXPERTS, cap, d), lambda i, j: (0, i, 0)),
            pl.BlockSpec((1, N_EXPERTS, tt), lambda i, j: (i, 0, j)),
            pl.BlockSpec((1, N_EXPERTS, tt), lambda i, j: (i, 0, j)),
            pl.BlockSpec((1, tt, d), lambda i, j: (i, j, 0)),
            pl.BlockSpec((1, 1, d), lambda i, j: (i, 0, 0)),
            pl.BlockSpec((1, d), lambda i, j: (0, 0)),
        ],
        out_specs=pl.BlockSpec((1, tt, d), lambda i, j: (i, j, 0)),
        out_shape=jax.ShapeDtypeStruct((b, t, d), F32),
        compiler_params=_cparams(("parallel", "parallel"), 48),
        name="combine_residual",
    )(y_sel, slot, gate, x, g2, fg)


def kernel(x, c, ctx, c_ctx, w_mod, b_mod, norm_g, att_w_in, att_w_out, na_rpb, diff_lambda, diff_subln_g,
           rec_w_in, rec_w_out, rec_lb_logits, rec_gnorm_g, moe_router, moe_w_gate, moe_w_up, moe_w_down,
           final_g):
    b, t, d = x.shape
    l_ctx = ctx.shape[1]
    n_c = b + 1
    r_pad = -(-n_c // 8) * 8
    c_all = jnp.concatenate([c, c_ctx[None, :], jnp.zeros((r_pad - n_c, d), F32)], axis=0)
    mods = modulation_all(c_all, w_mod, b_mod)
    cos, sin = rope_tables(t)

    for l in range(DEPTH):
        last = l == DEPTH - 1
        lat_m = [mods[l, :b, i * d:(i + 1) * d][:, None, :] for i in range(6)]
        ctx_m = [jnp.broadcast_to(mods[l, b, i * d:(i + 1) * d][None, None, :], (b, 1, d)) for i in range(6)]
        sh1, sc1, g1, sh2, sc2, g2 = lat_m
        csh1, csc1, cg1, csh2, csc2, cg2 = ctx_m

        if l % 2 == 0:
            e = l // 2
            w_in = att_w_in[e].astype(BF16)
            w_out = att_w_out[e].astype(BF16)
            p_lat = norm_mod_proj(x, norm_g[l, 0], sc1, sh1, w_in, BF16)
            p_ctx = norm_mod_proj(ctx, norm_g[l, 0], csc1, csh1, w_in, BF16)
            bias = na_bias_tables(na_rpb[e], t // GRID_W, l_ctx, LOG2E)
            oa_lat, oa_ctx = na_attention(p_lat, p_ctx, bias)
            ob_lat, ob_ctx = diff_attention(p_lat, p_ctx, cos, sin, diff_lambda[e], diff_subln_g[e], l)
            ws = [w_out[:A_W], w_out[A_W:]]
            x = outproj_residual([oa_lat, ob_lat], ws, x, g1)
            if not last:
                ctx = outproj_residual([oa_ctx, ob_ctx], ws, ctx, cg1)
        else:
            o = l // 2
            w_in = rec_w_in[o].astype(BF16)
            w_out = rec_w_out[o].astype(BF16)
            p_lat = norm_mod_proj(x, norm_g[l, 0], sc1, sh1, w_in, F32)
            p_ctx = norm_mod_proj(ctx, norm_g[l, 0], csc1, csh1, w_in, F32)
            y_lat, y_ctx = hgrn2_mixer(p_lat, p_ctx, rec_lb_logits, rec_gnorm_g[o], l)
            x = outproj_residual([y_lat], [w_out], x, g1)
            if not last:
                ctx = outproj_residual([y_ctx], [w_out], ctx, cg1)

        if last:
            ((xsel_l, slot_l, gate_l),) = route_gather([x], norm_g[l, 1], [sc2], [sh2], moe_router[l])
            (y_l,) = expert_ffn([xsel_l], moe_w_gate, moe_w_up, moe_w_down, l)
            x = combine_residual(y_l, slot_l, gate_l, x, g2, final_g)
        else:
            (xsel_l, slot_l, gate_l), (xsel_c, slot_c, gate_c) = route_gather(
                [x, ctx], norm_g[l, 1], [sc2, csc2], [sh2, csh2], moe_router[l])
            y_l, y_c = expert_ffn([xsel_l, xsel_c], moe_w_gate, moe_w_up, moe_w_down, l)
            x = combine_residual(y_l, slot_l, gate_l, x, g2)
            ctx = combine_residual(y_c, slot_c, gate_c, ctx, cg2)
    return x
```

```python
import functools
import math

import numpy as np
import jax
import jax.numpy as jnp
from jax import lax
from jax.experimental import pallas as pl
from jax.experimental.pallas import tpu as pltpu

F32 = jnp.float32
BF16 = jnp.bfloat16

D_MODEL = 1024
DEPTH = 4
GRID_W = 64
HEAD_DIM = 64
NA_HEADS = 8
NA_WIN_H = 8
NA_WIN_W = 16
DIFF_HEADS = 4
A_W = NA_HEADS * HEAD_DIM
ATT_IN_W = 3072
REC_HEADS = 8
REC_D = 128
N_EXPERTS = 16
EXPERT_FF = 2 * D_MODEL
EC_CAPACITY_FACTOR = 2
ROPE_THETA = 10000.0
NORM_EPS = 1e-6
NEG_INF = -1e30

LANES = 128
V7X_VMEM_BYTES = 64 * 1024 * 1024
MIB = 1024 * 1024

NA_QROWS = 4
NA_KROWS = NA_QROWS + NA_WIN_H - 1
NA_BLOCKS_PER_STEP = 2
DIFF_BLOCKS_PER_STEP = 2
REC_CHUNK = 64
REC_SUB = 16
REC_NSUB = REC_CHUNK // REC_SUB
REC_EXP_CLAMP = 60.0
REC_GROUP = 4
REC_GROUP_LATENT = 16
PREFIX_BLOCK = 256
THRESHOLD_RADIX_BITS = 3
ROUTE_EXPERTS_PER_STEP = 4
FFN_ROW_TILE = 512
FFN_HALVES = 2

LOG2E = math.log2(math.e)
SCORE_MULT = HEAD_DIM ** -0.5 * LOG2E

NT_DIMS = (((1,), (1,)), ((), ()))
TN_DIMS = (((0,), (0,)), ((), ()))


def _cparams(semantics, vmem_mib):
    assert vmem_mib * MIB < V7X_VMEM_BYTES
    return pltpu.CompilerParams(dimension_semantics=semantics, vmem_limit_bytes=vmem_mib * MIB)


def _sigmoid(x):
    return 1.0 / (1.0 + jnp.exp(-x))


def _rms(x, eps=NORM_EPS):
    return x * lax.rsqrt(jnp.mean(x * x, axis=-1, keepdims=True) + eps)


def _mod_kernel(c_ref, w_ref, b_ref, o_ref):
    c = c_ref[...]
    s = (c * _sigmoid(c)).astype(BF16)
    o_ref[0] = jnp.dot(s, w_ref[0].astype(BF16), preferred_element_type=F32) + b_ref[0]


def modulation_all(c_all, w_mod, b_mod):
    depth, d, n = w_mod.shape
    r = c_all.shape[0]
    tn = 1536
    return pl.pallas_call(
        _mod_kernel,
        grid=(depth, n // tn),
        in_specs=[
            pl.BlockSpec((r, d), lambda l, j: (0, 0)),
            pl.BlockSpec((1, d, tn), lambda l, j: (l, 0, j)),
            pl.BlockSpec((1, 1, tn), lambda l, j: (l, 0, j)),
        ],
        out_specs=pl.BlockSpec((1, r, tn), lambda l, j: (l, 0, j)),
        out_shape=jax.ShapeDtypeStruct((depth, r, n), F32),
        compiler_params=_cparams(("parallel", "parallel"), 32),
        name="modulation",
    )(c_all, w_mod, b_mod.reshape(depth, 1, n))


def _norm_mod(x, g, sc, sh):
    return _rms(x) * g * (1.0 + sc) + sh


def _proj_kernel(x_ref, g_ref, sc_ref, sh_ref, w_ref, o_ref, h_scr, *, tn):
    h_scr[...] = _norm_mod(x_ref[0], g_ref[...], sc_ref[0], sh_ref[0]).astype(BF16)
    n = w_ref.shape[1]
    for j in range(n // tn):
        o_ref[0, :, j * tn:(j + 1) * tn] = jnp.dot(
            h_scr[...], w_ref[:, j * tn:(j + 1) * tn], preferred_element_type=F32).astype(o_ref.dtype)


def norm_mod_proj(x, g, sc, sh, w_bf16, out_dtype):
    b, t, d = x.shape
    n = w_bf16.shape[1]
    tm = min(t, 512)
    return pl.pallas_call(
        functools.partial(_proj_kernel, tn=512),
        grid=(b, t // tm),
        in_specs=[
            pl.BlockSpec((1, tm, d), lambda i, j: (i, j, 0)),
            pl.BlockSpec((1, d), lambda i, j: (0, 0)),
            pl.BlockSpec((1, 1, d), lambda i, j: (i, 0, 0)),
            pl.BlockSpec((1, 1, d), lambda i, j: (i, 0, 0)),
            pl.BlockSpec((d, n), lambda i, j: (0, 0)),
        ],
        out_specs=pl.BlockSpec((1, tm, n), lambda i, j: (i, j, 0)),
        out_shape=jax.ShapeDtypeStruct((b, t, n), out_dtype),
        scratch_shapes=[pltpu.VMEM((tm, d), BF16)],
        compiler_params=_cparams(("parallel", "parallel"), 56),
        name="norm_mod_proj",
    )(x, g.reshape(1, d), sc, sh, w_bf16)


def _outproj_kernel(*refs, n_in):
    o_refs, w_refs = refs[:n_in], refs[n_in:2 * n_in]
    x_ref, gate_ref, out_ref = refs[2 * n_in:]
    acc = jnp.dot(o_refs[0][0], w_refs[0][...], preferred_element_type=F32)
    for o_ref, w_ref in zip(o_refs[1:], w_refs[1:]):
        acc = acc + jnp.dot(o_ref[0], w_ref[...], preferred_element_type=F32)
    out_ref[0] = x_ref[0] + gate_ref[0] * acc


def outproj_residual(os_, ws, x, gate):
    b, t, d = x.shape
    tm = min(t, 512)
    n_in = len(os_)
    in_specs = [pl.BlockSpec((1, tm, o.shape[2]), lambda i, j: (i, j, 0)) for o in os_]
    in_specs += [pl.BlockSpec(w.shape, lambda i, j: (0, 0)) for w in ws]
    in_specs += [pl.BlockSpec((1, tm, d), lambda i, j: (i, j, 0)),
                 pl.BlockSpec((1, 1, d), lambda i, j: (i, 0, 0))]
    return pl.pallas_call(
        functools.partial(_outproj_kernel, n_in=n_in),
        grid=(b, t // tm),
        in_specs=in_specs,
        out_specs=pl.BlockSpec((1, tm, d), lambda i, j: (i, j, 0)),
        out_shape=jax.ShapeDtypeStruct((b, t, d), F32),
        compiler_params=_cparams(("parallel", "parallel"), 32),
        name="outproj_residual",
    )(*os_, *ws, x, gate)


def _na_block_tables(rows):
    kh = min(NA_WIN_H, rows)
    assert kh == NA_WIN_H and rows % NA_QROWS == 0 and rows >= NA_KROWS
    n_qb = rows // NA_QROWS
    starts = [min(max(qb * NA_QROWS - kh // 2, 0), rows - NA_KROWS) for qb in range(n_qb)]
    tbl_of, reps = [], []
    for qb in range(n_qb):
        r0 = qb * NA_QROWS
        interior = (r0 - kh // 2 >= 0) and (r0 + NA_QROWS - 1 - kh // 2 <= rows - kh) and starts[qb] == r0 - kh // 2
        key = "interior" if interior else qb
        if key not in reps:
            reps.append(key)
        tbl_of.append(reps.index(key))
    rep_qb = [next(qb for qb in range(n_qb) if tbl_of[qb] == i) for i in range(len(reps))]
    tiles = []
    for qb in rep_qb:
        tiles.append([])
        for qi in range(NA_QROWS):
            r = qb * NA_QROWS + qi
            rs = min(max(r - kh // 2, 0), rows - kh)
            tiles[-1].append([(starts[qb] + kk - r + NA_WIN_H - 1) if rs <= starts[qb] + kk < rs + kh else None
                              for kk in range(NA_KROWS)])
    return starts, tbl_of, tiles


def na_bias_tables(rpb, rows, ctx_len, mult):
    _, _, tiles = _na_block_tables(rows)
    c = np.arange(GRID_W)
    cs = np.clip(c - NA_WIN_W // 2, 0, GRID_W - NA_WIN_W)
    col_ok = (c[None, :] >= cs[:, None]) & (c[None, :] < cs[:, None] + NA_WIN_W)
    dcol = c[None, :] - c[:, None] + NA_WIN_W - 1
    pick = ((dcol[None] == np.arange(2 * NA_WIN_W - 1)[:, None, None]) & col_ok[None]).astype(np.float32)
    toe = jnp.einsum("hrd,dqk->hrqk", rpb.astype(F32), jnp.asarray(pick), precision=lax.Precision.HIGHEST)
    toe = jnp.where(jnp.asarray(col_ok), toe * mult, NEG_INF)
    neg = jnp.full((rpb.shape[0], GRID_W, GRID_W), NEG_INF, F32)
    tables = [jnp.concatenate([jnp.concatenate([neg if d is None else toe[:, d] for d in row], axis=-1)
                               for row in tbl], axis=-2) for tbl in tiles]
    loc = jnp.stack(tables, axis=1)
    return jnp.concatenate([loc, jnp.zeros(loc.shape[:3] + (ctx_len,), F32)], axis=-1)


def _scores_many(qs, ks):
    return [lax.dot_general(q, k, NT_DIMS, preferred_element_type=F32) for q, k in zip(qs, ks)]


def _softmax_pv_many(s2, vs, biases=None):
    if biases is not None:
        s2 = [s + b for s, b in zip(s2, biases)]
    ps = [jnp.exp2(s - jnp.max(s, axis=-1, keepdims=True)) for s in s2]
    inv = [1.0 / jnp.sum(p, axis=-1, keepdims=True) for p in ps]
    return [jnp.dot(p.astype(BF16), v, preferred_element_type=F32) * r for p, v, r in zip(ps, vs, inv)]


def _head_half_scales(mult):
    first = lax.broadcasted_iota(jnp.int32, (1, LANES), 1) < HEAD_DIM
    return first, jnp.where(first, mult, 0.0), jnp.where(first, 0.0, mult)


def _na_kernel(q_ref, k_ref, v_ref, qc_ref, kc_ref, vc_ref, bias_ref, o_ref, oc_ref, *, starts, tbl_of):
    first, *halves = _head_half_scales(SCORE_MULT)
    nq, nk = NA_QROWS * GRID_W, NA_KROWS * GRID_W
    kc, vc = kc_ref[0], vc_ref[0]
    n_qb = len(starts)
    assert n_qb % NA_BLOCKS_PER_STEP == 0

    def group_scores(qb0):
        qs, ks = [], []
        for qb in range(qb0, qb0 + NA_BLOCKS_PER_STEP):
            q = q_ref[0, qb * nq:(qb + 1) * nq, :].astype(F32)
            k_all = jnp.concatenate([k_ref[0, starts[qb] * GRID_W:starts[qb] * GRID_W + nk, :], kc], axis=0)
            for hh in range(2):
                qs.append((q * halves[hh]).astype(BF16))
                ks.append(k_all)
        return _scores_many(qs, ks)

    s_next = group_scores(0)
    for qb0 in range(0, n_qb, NA_BLOCKS_PER_STEP):
        blocks = range(qb0, qb0 + NA_BLOCKS_PER_STEP)
        s_cur = s_next
        if qb0 + NA_BLOCKS_PER_STEP < n_qb:
            s_next = group_scores(qb0 + NA_BLOCKS_PER_STEP)
        vs, bs = [], []
        for qb in blocks:
            v_all = jnp.concatenate([v_ref[0, starts[qb] * GRID_W:starts[qb] * GRID_W + nk, :], vc], axis=0)
            for hh in range(2):
                vs.append(v_all)
                bs.append(bias_ref[hh, tbl_of[qb]])
        outs = _softmax_pv_many(s_cur, vs, bs)
        for i, qb in enumerate(blocks):
            o_ref[0, qb * nq:(qb + 1) * nq, :] = jnp.where(first, outs[2 * i], outs[2 * i + 1]).astype(o_ref.dtype)
    qc = qc_ref[0].astype(F32)
    outs = _softmax_pv_many(_scores_many([(qc * h).astype(BF16) for h in halves], [kc, kc]), [vc, vc])
    oc_ref[0] = jnp.where(first, outs[0], outs[1]).astype(oc_ref.dtype)


def na_attention(p_lat, p_ctx, bias):
    b, t, _ = p_lat.shape
    l = p_ctx.shape[1]
    rows = t // GRID_W
    starts, tbl_of, _ = _na_block_tables(rows)
    n_pairs = NA_HEADS // 2
    kb, vb = 2 * A_W // LANES, 3 * A_W // LANES
    lat = lambda off: pl.BlockSpec((1, t, LANES), lambda j, i: (i, 0, off + j))
    cx = lambda off: pl.BlockSpec((1, l, LANES), lambda j, i: (i, 0, off + j))
    n_tbl, nq, nkc = bias.shape[1:]
    return pl.pallas_call(
        functools.partial(_na_kernel, starts=tuple(starts), tbl_of=tuple(tbl_of)),
        grid=(n_pairs, b),
        in_specs=[lat(0), lat(kb), lat(vb), cx(0), cx(kb), cx(vb),
                  pl.BlockSpec((2, n_tbl, nq, nkc), lambda j, i: (j, 0, 0, 0))],
        out_specs=[pl.BlockSpec((1, t, LANES), lambda j, i: (i, 0, j)),
                   pl.BlockSpec((1, l, LANES), lambda j, i: (i, 0, j))],
        out_shape=[jax.ShapeDtypeStruct((b, t, A_W), BF16), jax.ShapeDtypeStruct((b, l, A_W), BF16)],
        compiler_params=_cparams(("parallel", "parallel"), 48),
        name="na_attention",
    )(p_lat, p_lat, p_lat, p_ctx, p_ctx, p_ctx, bias)


def rope_tables(n):
    t = np.arange(n)
    rows, cols = (t // GRID_W).astype(np.float64), (t % GRID_W).astype(np.float64)
    n_freq = HEAD_DIM // 4
    inv = ROPE_THETA ** (-np.arange(n_freq, dtype=np.float64) / n_freq)
    ang = np.concatenate([rows[:, None] * inv, cols[:, None] * inv], axis=-1)
    cos = np.tile(np.cos(ang), (1, 4))
    sin = np.tile(np.concatenate([-np.sin(ang), np.sin(ang)], axis=-1), (1, 2))
    return jnp.asarray(cos, F32), jnp.asarray(sin, F32)


def _rope(a, cos, sin_signed):
    lane = lax.broadcasted_iota(jnp.int32, (1, LANES), 1)
    low = (lane % HEAD_DIM) < HEAD_DIM // 2
    partner = jnp.where(low, pltpu.roll(a, LANES - HEAD_DIM // 2, 1), pltpu.roll(a, HEAD_DIM // 2, 1))
    return a * cos + partner * sin_signed


def _diff_kernel(q_ref, k_ref, v_ref, qc_ref, kc_ref, vc_ref, cos_ref, sin_ref, lam_ref, gcol_ref,
                 o_ref, oc_ref, q1_scr, q2_scr, k_scr, vt_scr, *, lam_init, tq):
    t, l = q_ref.shape[1], kc_ref.shape[1]
    _, half1, half2 = _head_half_scales(SCORE_MULT)
    lv = lam_ref[...]
    lam = (jnp.exp(jnp.sum(lv[0:1] * lv[1:2], axis=-1, keepdims=True))
           - jnp.exp(jnp.sum(lv[2:3] * lv[3:4], axis=-1, keepdims=True)) + lam_init)
    cos, sin = cos_ref[...], sin_ref[...]
    q = _rope(q_ref[0].astype(F32), cos, sin)
    q1_scr[...] = (q * half1).astype(BF16)
    q2_scr[...] = (q * half2).astype(BF16)
    k_scr[0:t, :] = _rope(k_ref[0].astype(F32), cos, sin).astype(BF16)
    k_scr[t:t + l, :] = kc_ref[0]
    for i in range(t // tq):
        vt_scr[:, i * tq:(i + 1) * tq] = v_ref[0, i * tq:(i + 1) * tq, :].astype(F32).T.astype(BF16)
    vct = vc_ref[0].astype(F32).T.astype(BF16)
    vt_scr[:, t:t + l] = vct
    gain = gcol_ref[...] * (1.0 - lam_init)

    def scores(q_pairs, k):
        return [lax.dot_general(k, q, NT_DIMS, preferred_element_type=F32) for pair in q_pairs for q in pair]

    def finish(s2, vt):
        ps = [jnp.exp2(s - jnp.max(s, axis=0, keepdims=True)) for s in s2]
        inv = [1.0 / jnp.sum(p, axis=0, keepdims=True) for p in ps]
        ot = [jnp.dot(vt, p.astype(BF16), preferred_element_type=F32) * r for p, r in zip(ps, inv)]
        outs = []
        for i in range(len(s2) // 2):
            d = ot[2 * i] - lam * ot[2 * i + 1]
            d = d * lax.rsqrt(jnp.mean(d * d, axis=0, keepdims=True) + NORM_EPS) * gain
            outs.append(d.T)
        return outs

    step = tq * DIFF_BLOCKS_PER_STEP
    group = lambda r0: [(q1_scr[r:r + tq, :], q2_scr[r:r + tq, :]) for r in range(r0, r0 + step, tq)]
    k_all, vt_all = k_scr[...], vt_scr[...]
    s_next = scores(group(0), k_all)
    for r0 in range(0, t, step):
        s_cur = s_next
        if r0 + step < t:
            s_next = scores(group(r0 + step), k_all)
        for j, o in enumerate(finish(s_cur, vt_all)):
            o_ref[0, r0 + j * tq:r0 + (j + 1) * tq, :] = o.astype(o_ref.dtype)
    qc = qc_ref[0].astype(F32)
    (oc,) = finish(scores([((qc * half1).astype(BF16), (qc * half2).astype(BF16))], kc_ref[0]), vct)
    oc_ref[0] = oc.astype(oc_ref.dtype)


def diff_attention(p_lat, p_ctx, cos, sin, lam_vecs, subln_g, layer_idx):
    b, t, _ = p_lat.shape
    l = p_ctx.shape[1]
    lam_init = 0.8 - 0.6 * math.exp(-0.3 * layer_idx)
    qb, kb, vb = A_W // LANES, 4 * A_W // LANES, 5 * A_W // LANES
    lat = lambda off: pl.BlockSpec((1, t, LANES), lambda i, j: (i, 0, off + j))
    cx = lambda off: pl.BlockSpec((1, l, LANES), lambda i, j: (i, 0, off + j))
    const = lambda shape: pl.BlockSpec(shape, lambda i, j: (0, 0))
    tq = 256
    assert t % (tq * DIFF_BLOCKS_PER_STEP) == 0
    return pl.pallas_call(
        functools.partial(_diff_kernel, lam_init=lam_init, tq=tq),
        grid=(b, DIFF_HEADS),
        in_specs=[lat(qb), lat(kb), lat(vb), cx(qb), cx(kb), cx(vb),
                  const((t, LANES)), const((t, LANES)), const(lam_vecs.shape), const((LANES, 1))],
        out_specs=[pl.BlockSpec((1, t, LANES), lambda i, j: (i, 0, j)),
                   pl.BlockSpec((1, l, LANES), lambda i, j: (i, 0, j))],
        out_shape=[jax.ShapeDtypeStruct((b, t, A_W), BF16), jax.ShapeDtypeStruct((b, l, A_W), BF16)],
        scratch_shapes=[pltpu.VMEM((t, LANES), BF16), pltpu.VMEM((t, LANES), BF16),
                        pltpu.VMEM((t + l, LANES), BF16), pltpu.VMEM((LANES, t + l), BF16)],
        compiler_params=_cparams(("parallel", "parallel"), 48),
        name="diff_attention",
    )(p_lat, p_lat, p_lat, p_ctx, p_ctx, p_ctx, cos, sin, lam_vecs, subln_g.reshape(LANES, 1))


def _split2(x):
    a = x.astype(BF16)
    return a, (x - a.astype(F32)).astype(BF16)


def _rec_masks(forward):
    t = lax.broadcasted_iota(jnp.int32, (REC_CHUNK, REC_CHUNK), 0)
    s = lax.broadcasted_iota(jnp.int32, (REC_CHUNK, REC_CHUNK), 1)
    incl = (s <= t) if forward else (s >= t)
    return jnp.where(incl, 1.0, 0.0).astype(BF16), incl


def _rec_group_local(qs, fxs, vb, lbs):
    c = REC_CHUNK
    n_grp = qs.shape[0] // c
    items = [(d, j) for j in range(n_grp) for d in range(2)]
    rows = lambda a, j: a[j * c:(j + 1) * c]
    masks = [_rec_masks(True), _rec_masks(False)]
    kks, g3s = [], []
    for fx, lb in zip(fxs, lbs):
        e = jnp.exp(-jnp.abs(fx))
        r = 1.0 / (1.0 + e)
        pos = fx >= 0
        f = lb + (1.0 - lb) * jnp.where(pos, r, e * r)
        kks.append((1.0 - lb) * jnp.where(pos, e * r, r))
        g3s.append(jnp.concatenate(_split2(jnp.log(f)), axis=1))
    cs = {(d, j): jnp.dot(masks[d][0], rows(g3s[d], j), preferred_element_type=F32) for d, j in items}
    row_sub = lax.broadcasted_iota(jnp.int32, (c, 1), 0) // REC_SUB
    band = lambda a, p: a[p * REC_SUB:(p + 1) * REC_SUB]
    subs = range(REC_NSUB)
    qcat, kcat, qtil, kdec, dec = {}, {}, {}, {}, {}
    for d, j in items:
        x = cs[d, j]
        x = x[:, 0:LANES] + x[:, LANES:2 * LANES]
        bcum = x
        q, kk = rows(qs, j), rows(kks[d], j)
        if d == 0:
            ends = [bcum[(p + 1) * REC_SUB - 1:(p + 1) * REC_SUB] for p in subs]
            starts = [jnp.zeros((1, LANES), F32)] + ends[:-1]
            blast = ends[-1]
        else:
            ends = [bcum[p * REC_SUB:p * REC_SUB + 1] for p in subs]
            starts = ends[1:] + [jnp.zeros((1, LANES), F32)]
            blast = ends[0]
        ends_mat = jnp.concatenate(ends, axis=0)
        per_row = lambda bounds: jnp.concatenate([jnp.broadcast_to(b_, (REC_SUB, LANES)) for b_ in bounds], axis=0)
        qhat = q * jnp.exp(bcum - per_row(starts))
        bend = per_row(ends)
        ke = kk * jnp.exp(bend - bcum)
        qcat[d, j] = jnp.concatenate([jnp.where(row_sub == p, qhat, 0.0) for p in subs], axis=1).astype(BF16)
        grow = [jnp.exp(jnp.minimum(starts[p] - ends_mat, REC_EXP_CLAMP)) for p in subs]
        kcat[d, j] = jnp.concatenate(
            [jnp.concatenate([band(ke, s) * grow[p][s:s + 1] for s in subs], axis=0) for p in subs],
            axis=1).astype(BF16)
        from_start = jnp.exp(jnp.concatenate(starts, axis=0))
        to_last = jnp.exp(blast - ends_mat)
        qtil[d, j] = jnp.concatenate([band(qhat, p) * from_start[p:p + 1] for p in subs], axis=0).astype(BF16)
        kdec[d, j] = jnp.concatenate([band(ke, p) * to_last[p:p + 1] for p in subs], axis=0).astype(BF16)
        dec[d, j] = jnp.exp(blast)
    sc = {k: lax.dot_general(qcat[k], kcat[k], NT_DIMS, preferred_element_type=F32) for k in items}
    u2 = [lax.dot_general(rows(vb, j), jnp.concatenate([kdec[0, j], kdec[1, j]], axis=1), TN_DIMS,
                          preferred_element_type=F32) for j in range(n_grp)]
    u = {(d, j): u2[j][:, d * LANES:(d + 1) * LANES] for d, j in items}
    scb = {(d, j): jnp.where(masks[d][1], sc[d, j], 0.0).astype(BF16) for d, j in items}
    o_intra = {(d, j): jnp.dot(scb[d, j], rows(vb, j), preferred_element_type=F32) for d, j in items}
    return {k: (o_intra[k], qtil[k], u[k], dec[k]) for k in items}


def _rec_kernel(q_ref, ff_ref, fb_ref, v_ref, g_ref, qc_ref, ffc_ref, fbc_ref, vc_ref, gc_ref,
                lb_ref, gn_ref, y_ref, yc_ref, o_scr, qt_scr, u_scr, dec_scr, *, layer_idx):
    t, l = q_ref.shape[1], qc_ref.shape[1]
    c, grp = REC_CHUNK, REC_GROUP

    def lower_bound(d):
        lg = lb_ref[d]
        ex = jnp.exp(lg - jnp.max(lg, axis=0, keepdims=True))
        return jnp.sum(ex[1:layer_idx + 1], axis=0, keepdims=True) / jnp.sum(ex, axis=0, keepdims=True)

    lbs = (lower_bound(0), lower_bound(1))

    def local_pass(q_r, f_refs, v_r, n_tok, tok0, grp):
        def body(i, carry):
            r0 = pl.multiple_of(i * (grp * c), grp * c)
            q = q_r[0, pl.ds(r0, grp * c), :]
            qs = q * _sigmoid(q)
            vb = v_r[0, pl.ds(r0, grp * c), :].astype(BF16)
            fxs = [f_r[0, pl.ds(r0, grp * c), :] for f_r in f_refs]
            local = _rec_group_local(qs, fxs, vb, lbs)
            for j in range(grp):
                row = pl.multiple_of(tok0 + r0 + j * c, c)
                ci = tok0 // c + i * grp + j
                for d in range(2):
                    o_intra, qtil, u, dec = local[d, j]
                    o_scr[d, pl.ds(row, c), :] = o_intra
                    qt_scr[d, pl.ds(row, c), :] = qtil
                    u_scr[d, ci] = u
                    dec_scr[d, pl.ds(ci, 1), :] = dec
            return carry
        lax.fori_loop(0, n_tok // (grp * c), body, 0)

    def state_pass(c_lo, n, states):
        def body(i, carry):
            sts = list(carry)
            for k in range(grp):
                step = i * grp + k
                for d, ci in ((0, c_lo + step), (1, c_lo + n - 1 - step)):
                    row = pl.multiple_of(ci * c, c)
                    o_scr[d, pl.ds(row, c), :] += lax.dot_general(
                        qt_scr[d, pl.ds(row, c), :], sts[d].astype(BF16), NT_DIMS, preferred_element_type=F32)
                    sts[d] = sts[d] * dec_scr[d, pl.ds(ci, 1), :] + u_scr[d, ci]
            return tuple(sts)
        return lax.fori_loop(0, n // grp, body, states)

    def finish(tok0, n_tok, gate_ref, out_ref):
        o = o_scr[0, tok0:tok0 + n_tok, :] + o_scr[1, tok0:tok0 + n_tok, :]
        gate = gate_ref[0]
        out_ref[0] = ((_rms(o) * gn_ref[...]) * (gate * _sigmoid(gate))).astype(out_ref.dtype)

    local_pass(qc_ref, (ffc_ref, fbc_ref), vc_ref, l, 0, REC_GROUP)
    local_pass(q_ref, (ff_ref, fb_ref), v_ref, t, l, REC_GROUP_LATENT)
    zero = jnp.zeros((REC_D, REC_D), F32)
    states = state_pass(0, l // c, (zero, zero))
    state_pass(l // c, t // c, states)
    finish(0, l, gc_ref, yc_ref)
    finish(l, t, g_ref, y_ref)


def hgrn2_mixer(p_lat, p_ctx, lb_logits, gnorm_g, layer_idx):
    b, t, _ = p_lat.shape
    l = p_ctx.shape[1]
    assert t % (REC_GROUP_LATENT * REC_CHUNK) == 0 and l % (REC_GROUP * REC_CHUNK) == 0
    n_chunks = (t + l) // REC_CHUNK
    lat = lambda k: pl.BlockSpec((1, t, REC_D), lambda i, j: (i, 0, k * REC_HEADS + j))
    cx = lambda k: pl.BlockSpec((1, l, REC_D), lambda i, j: (i, 0, k * REC_HEADS + j))
    return pl.pallas_call(
        functools.partial(_rec_kernel, layer_idx=layer_idx),
        grid=(b, REC_HEADS),
        in_specs=[lat(k) for k in range(5)] + [cx(k) for k in range(5)] + [
            pl.BlockSpec((2, DEPTH, REC_D), lambda i, j: (0, 0, j)),
            pl.BlockSpec((1, REC_D), lambda i, j: (0, 0))],
        out_specs=[pl.BlockSpec((1, t, REC_D), lambda i, j: (i, 0, j)),
                   pl.BlockSpec((1, l, REC_D), lambda i, j: (i, 0, j))],
        out_shape=[jax.ShapeDtypeStruct((b, t, D_MODEL), BF16), jax.ShapeDtypeStruct((b, l, D_MODEL), BF16)],
        scratch_shapes=[pltpu.VMEM((2, t + l, REC_D), F32), pltpu.VMEM((2, t + l, REC_D), BF16),
                        pltpu.VMEM((2, n_chunks, REC_D, REC_D), F32), pltpu.VMEM((2, n_chunks, REC_D), F32)],
        compiler_params=_cparams(("parallel", "parallel"), 40),
        name="hgrn2_mixer",
    )(*([p_lat] * 5), *([p_ctx] * 5), lb_logits, gnorm_g.reshape(1, REC_D))


def _prefix_count(flags):
    e, t = flags.shape
    blk = min(PREFIX_BLOCK, t)
    ones = jnp.where(flags, 1.0, 0.0)
    s = lax.broadcasted_iota(jnp.int32, (blk, blk), 0)
    u = lax.broadcasted_iota(jnp.int32, (blk, blk), 1)
    upper = jnp.where(s < u, 1.0, 0.0).astype(BF16)
    carry = jnp.zeros((e, 1), F32)
    outs = []
    for i in range(t // blk):
        xs = ones[:, i * blk:(i + 1) * blk]
        outs.append(jnp.dot(xs.astype(BF16), upper, preferred_element_type=F32) + carry)
        carry = carry + jnp.sum(xs, axis=1, keepdims=True)
    return jnp.concatenate(outs, axis=1)


def _capacity_slots(affs, caps):
    as_float = lambda bits: lax.bitcast_convert_type(bits, F32)
    capf = [jnp.float32(c) for c in caps]
    thr = [jnp.zeros((N_EXPERTS, 1), jnp.int32) for _ in affs]
    for hi in range(30, -1, -THRESHOLD_RADIX_BITS):
        lo = max(hi - THRESHOLD_RADIX_BITS + 1, 0)
        cands = [[t | jnp.int32(m << lo) for m in range(1, 1 << (hi - lo + 1))] for t in thr]
        cnts = [[jnp.sum(jnp.where(a >= as_float(cand), 1.0, 0.0), axis=1, keepdims=True) for cand in cs]
                for a, cs in zip(affs, cands)]
        for i in range(len(affs)):
            for cand, cnt in zip(cands[i], cnts[i]):
                thr[i] = jnp.where(cnt >= capf[i], cand, thr[i])
    slots = []
    for a, t, c in zip(affs, thr, capf):
        gt = a > as_float(t)
        eq = a == as_float(t)
        need = c - jnp.sum(jnp.where(gt, 1.0, 0.0), axis=1, keepdims=True)
        sel = gt | (eq & (_prefix_count(eq) < need))
        slots.append(jnp.where(sel, _prefix_count(sel), -1.0))
    return slots


def _route_kernel(*refs, caps):
    n = len(caps)
    x_refs, g_ref = refs[:n], refs[n]
    sc_refs, sh_refs, wr_ref = refs[n + 1:2 * n + 1], refs[2 * n + 1:3 * n + 1], refs[3 * n + 1]
    outs = refs[3 * n + 2:]
    xsel_refs, slot_refs, gate_refs = outs[:n], outs[n:2 * n], outs[2 * n:3 * n]
    hb_scrs, slot_scrs = outs[3 * n:4 * n], outs[4 * n:5 * n]
    e = pl.program_id(1)

    @pl.when(e == 0)
    def _():
        w_hi, w_lo = _split2(wr_ref[...])
        w2 = jnp.concatenate([w_hi, w_lo], axis=0)
        affs = []
        for x_ref, sc_ref, sh_ref, hb_scr in zip(x_refs, sc_refs, sh_refs, hb_scrs):
            h = _norm_mod(x_ref[0], g_ref[...], sc_ref[0], sh_ref[0])
            hb = h.astype(BF16)
            hb_scr[...] = hb
            h_lo = (h - hb.astype(F32)).astype(BF16)
            hi_terms = lax.dot_general(w2, hb, NT_DIMS, preferred_element_type=F32)
            logits = (hi_terms[:N_EXPERTS] + hi_terms[N_EXPERTS:]
                      + lax.dot_general(w_hi, h_lo, NT_DIMS, preferred_element_type=F32))
            ex = jnp.exp(logits - jnp.max(logits, axis=0, keepdims=True))
            affs.append(ex / jnp.sum(ex, axis=0, keepdims=True))
        for aff, slot, slot_scr, slot_ref, gate_ref in zip(affs, _capacity_slots(affs, caps), slot_scrs, slot_refs,
                                                           gate_refs):
            slot_scr[...] = slot
            slot_ref[0] = slot
            gate_ref[0] = aff

    for cap, xsel_ref, hb_scr, slot_scr in zip(caps, xsel_refs, hb_scrs, slot_scrs):
        row = lax.broadcasted_iota(jnp.int32, (cap, hb_scr.shape[0]), 0).astype(F32)
        picks = [jnp.where(row == slot_scr[pl.ds(e * ROUTE_EXPERTS_PER_STEP + i, 1), :], 1.0, 0.0).astype(BF16)
                 for i in range(ROUTE_EXPERTS_PER_STEP)]
        for i, pick in enumerate(picks):
            xsel_ref[i] = jnp.dot(pick, hb_scr[...], preferred_element_type=F32).astype(xsel_ref.dtype)


def route_gather(xs, g, scs, shs, w_router):
    b, _, d = xs[0].shape
    ts = [x.shape[1] for x in xs]
    caps = [EC_CAPACITY_FACTOR * t // N_EXPERTS for t in ts]
    n = len(xs)
    vec = pl.BlockSpec((1, 1, d), lambda i, e: (i, 0, 0))
    per_sample = lambda t: pl.BlockSpec((1, N_EXPERTS, t), lambda i, e: (i, 0, 0))
    outs = pl.pallas_call(
        functools.partial(_route_kernel, caps=tuple(caps)),
        grid=(b, N_EXPERTS // ROUTE_EXPERTS_PER_STEP),
        in_specs=([pl.BlockSpec((1, t, d), lambda i, e: (i, 0, 0)) for t in ts]
                  + [pl.BlockSpec((1, d), lambda i, e: (0, 0))] + [vec] * (2 * n)
                  + [pl.BlockSpec((N_EXPERTS, d), lambda i, e: (0, 0))]),
        out_specs=([pl.BlockSpec((ROUTE_EXPERTS_PER_STEP, c, d), lambda i, e: (e, i, 0)) for c in caps]
                   + [per_sample(t) for t in ts] * 2),
        out_shape=([jax.ShapeDtypeStruct((N_EXPERTS, b * c, d), BF16) for c in caps]
                   + [jax.ShapeDtypeStruct((b, N_EXPERTS, t), F32) for t in ts] * 2),
        scratch_shapes=([pltpu.VMEM((t, d), BF16) for t in ts] + [pltpu.VMEM((N_EXPERTS, t), F32) for t in ts]),
        compiler_params=_cparams(("parallel", "arbitrary"), 48),
        name="route_gather",
    )(*xs, g.reshape(1, d), *scs, *shs, w_router.T)
    return [(outs[i], outs[n + i], outs[2 * n + i]) for i in range(n)]


def _ffn_kernel(*refs, n_lat_tiles, has_ctx, layer_idx):
    if has_ctx:
        xl_ref, xc_ref, wg_hbm, wu_hbm, wd_hbm, yl_ref, yc_ref, wg_s, wu_s, wd_s, stage, sem = refs
    else:
        xl_ref, wg_hbm, wu_hbm, wd_hbm, yl_ref, wg_s, wu_s, wd_s, stage, sem = refs
    e, r = pl.program_id(0), pl.program_id(1)
    half = stage.shape[0]
    w_hbm, w_s = (wg_hbm, wu_hbm, wd_hbm), (wg_s, wu_s, wd_s)

    def chunk_copy(expert, c):
        m, h = divmod(c, FFN_HALVES)
        if m < 2:
            src = w_hbm[m].at[layer_idx, expert, :, pl.ds(h * half, half)]
        else:
            src = w_hbm[m].at[layer_idx, expert, pl.ds(h * half, half), :]
        return pltpu.make_async_copy(src, stage, sem.at[0])

    def chunk_store(slot, c):
        m, h = divmod(c, FFN_HALVES)
        if m < 2:
            w_s[m][slot, :, h * half:(h + 1) * half] = stage[...].astype(BF16)
        else:
            w_s[m][slot, h * half:(h + 1) * half, :] = stage[...].astype(BF16)

    n_chunks = 3 * FFN_HALVES

    @pl.when((e == 0) & (r == 0))
    def _():
        for c in range(n_chunks):
            cp = chunk_copy(0, c)
            cp.start()
            cp.wait()
            chunk_store(0, c)

    nxt = e + 1
    prefetch = nxt < pl.num_programs(0)
    for c in range(n_chunks):
        @pl.when(prefetch & (r == c + 1))
        def _():
            chunk_copy(nxt, c).wait()
            chunk_store(nxt % 2, c)

        @pl.when(prefetch & (r == c))
        def _():
            chunk_copy(nxt, c).start()

    slot = e % 2

    def swiglu(x_ref, y_ref):
        x = x_ref[0]
        a = jnp.dot(x, wg_s[slot], preferred_element_type=F32)
        u = jnp.dot(x, wu_s[slot], preferred_element_type=F32)
        hid = ((a * _sigmoid(a)) * u).astype(BF16)
        y_ref[0] = jnp.dot(hid, wd_s[slot], preferred_element_type=F32).astype(y_ref.dtype)

    if has_ctx:
        @pl.when(r < n_lat_tiles)
        def _():
            swiglu(xl_ref, yl_ref)

        @pl.when(r == n_lat_tiles)
        def _():
            swiglu(xc_ref, yc_ref)
    else:
        swiglu(xl_ref, yl_ref)


def expert_ffn(xsels, w_gate, w_up, w_down, layer_idx):
    d, ff = w_gate.shape[2], w_gate.shape[3]
    has_ctx = len(xsels) == 2
    tm = min(FFN_ROW_TILE, xsels[1].shape[1]) if has_ctx else FFN_ROW_TILE
    while xsels[0].shape[1] // tm + has_ctx <= 3 * FFN_HALVES:
        tm //= 2
    n_lat_tiles = xsels[0].shape[1] // tm
    n_tiles = n_lat_tiles + (1 if has_ctx else 0)
    assert xsels[0].shape[1] % tm == 0 and ff == FFN_HALVES * d and n_tiles > 3 * FFN_HALVES
    if has_ctx:
        assert xsels[1].shape[1] == tm
    lat_spec = pl.BlockSpec((1, tm, d), lambda e, r: (e, jnp.minimum(r, n_lat_tiles - 1), 0))
    ctx_spec = pl.BlockSpec((1, tm, d), lambda e, r: (e, 0, 0))
    io_specs = [lat_spec, ctx_spec] if has_ctx else [lat_spec]
    hbm = pl.BlockSpec(memory_space=pl.ANY)
    return pl.pallas_call(
        functools.partial(_ffn_kernel, n_lat_tiles=n_lat_tiles, has_ctx=has_ctx, layer_idx=layer_idx),
        grid=(N_EXPERTS, n_tiles),
        in_specs=io_specs + [hbm, hbm, hbm],
        out_specs=io_specs,
        out_shape=[jax.ShapeDtypeStruct(x.shape, BF16) for x in xsels],
        scratch_shapes=[pltpu.VMEM((2, d, ff), BF16), pltpu.VMEM((2, d, ff), BF16), pltpu.VMEM((2, ff, d), BF16),
                        pltpu.VMEM((d, d), F32), pltpu.SemaphoreType.DMA((1,))],
        compiler_params=_cparams(("arbitrary", "arbitrary"), 56),
        name="expert_ffn",
    )(*xsels, w_gate, w_up, w_down)


def _combine_kernel(y_ref, slot_ref, gate_ref, x_ref, g2_ref, fg_ref, out_ref, *, final_norm):
    n_e, cap, d = y_ref.shape
    tt = x_ref.shape[1]
    slot, gate = slot_ref[0], gate_ref[0]
    row = lax.broadcasted_iota(jnp.int32, (cap, tt), 0).astype(F32)
    pg = jnp.concatenate([jnp.where(row == slot[e:e + 1], gate[e:e + 1], 0.0).astype(BF16) for e in range(n_e)],
                         axis=0)
    moe = lax.dot_general(pg, y_ref[...].reshape(n_e * cap, d), TN_DIMS, preferred_element_type=F32)
    xn = x_ref[0] + g2_ref[0] * moe
    if final_norm:
        xn = _rms(xn) * fg_ref[...]
    out_ref[0] = xn


def combine_residual(y_sel, slot, gate, x, g2, final_g=None):
    b, t, d = x.shape
    cap = EC_CAPACITY_FACTOR * t // N_EXPERTS
    tt = min(t, 1024)
    final_norm = final_g is not None
    fg = (final_g if final_norm else jnp.ones((d,), F32)).reshape(1, d)
    return pl.pallas_call(
        functools.partial(_combine_kernel, final_norm=final_norm),
        grid=(b, t // tt),
        in_specs=[
            pl.BlockSpec((N_EXPERTS, cap, d), lambda i, j: (0, i, 0)),
            pl.BlockSpec((1, N_EXPERTS, tt), lambda i, j: (i, 0, j)),
            pl.BlockSpec((1, N_EXPERTS, tt), lambda i, j: (i, 0, j)),
            pl.BlockSpec((1, tt, d), lambda i, j: (i, j, 0)),
            pl.BlockSpec((1, 1, d), lambda i, j: (i, 0, 0)),
            pl.BlockSpec((1, d), lambda i, j: (0, 0)),
        ],
        out_specs=pl.BlockSpec((1, tt, d), lambda i, j: (i, j, 0)),
        out_shape=jax.ShapeDtypeStruct((b, t, d), F32),
        compiler_params=_cparams(("parallel", "parallel"), 48),
        name="combine_residual",
    )(y_sel, slot, gate, x, g2, fg)


def kernel(x, c, ctx, c_ctx, w_mod, b_mod, norm_g, att_w_in, att_w_out, na_rpb, diff_lambda, diff_subln_g,
           rec_w_in, rec_w_out, rec_lb_logits, rec_gnorm_g, moe_router, moe_w_gate, moe_w_up, moe_w_down,
           final_g):
    b, t, d = x.shape
    l_ctx = ctx.shape[1]
    n_c = b + 1
    r_pad = -(-n_c // 8) * 8
    c_all = jnp.concatenate([c, c_ctx[None, :], jnp.zeros((r_pad - n_c, d), F32)], axis=0)
    mods = modulation_all(c_all, w_mod, b_mod)
    cos, sin = rope_tables(t)

    for l in range(DEPTH):
        last = l == DEPTH - 1
        lat_m = [mods[l, :b, i * d:(i + 1) * d][:, None, :] for i in range(6)]
        ctx_m = [jnp.broadcast_to(mods[l, b, i * d:(i + 1) * d][None, None, :], (b, 1, d)) for i in range(6)]
        sh1, sc1, g1, sh2, sc2, g2 = lat_m
        csh1, csc1, cg1, csh2, csc2, cg2 = ctx_m

        if l % 2 == 0:
            e = l // 2
            w_in = att_w_in[e].astype(BF16)
            w_out = att_w_out[e].astype(BF16)
            p_lat = norm_mod_proj(x, norm_g[l, 0], sc1, sh1, w_in, BF16)
            p_ctx = norm_mod_proj(ctx, norm_g[l, 0], csc1, csh1, w_in, BF16)
            bias = na_bias_tables(na_rpb[e], t // GRID_W, l_ctx, LOG2E)
            oa_lat, oa_ctx = na_attention(p_lat, p_ctx, bias)
            ob_lat, ob_ctx = diff_attention(p_lat, p_ctx, cos, sin, diff_lambda[e], diff_subln_g[e], l)
            ws = [w_out[:A_W], w_out[A_W:]]
            x = outproj_residual([oa_lat, ob_lat], ws, x, g1)
            if not last:
                ctx = outproj_residual([oa_ctx, ob_ctx], ws, ctx, cg1)
        else:
            o = l // 2
            w_in = rec_w_in[o].astype(BF16)
            w_out = rec_w_out[o].astype(BF16)
            p_lat = norm_mod_proj(x, norm_g[l, 0], sc1, sh1, w_in, F32)
            p_ctx = norm_mod_proj(ctx, norm_g[l, 0], csc1, csh1, w_in, F32)
            y_lat, y_ctx = hgrn2_mixer(p_lat, p_ctx, rec_lb_logits, rec_gnorm_g[o], l)
            x = outproj_residual([y_lat], [w_out], x, g1)
            if not last:
                ctx = outproj_residual([y_ctx], [w_out], ctx, cg1)

        if last:
            ((xsel_l, slot_l, gate_l),) = route_gather([x], norm_g[l, 1], [sc2], [sh2], moe_router[l])
            (y_l,) = expert_ffn([xsel_l], moe_w_gate, moe_w_up, moe_w_down, l)
            x = combine_residual(y_l, slot_l, gate_l, x, g2, final_g)
        else:
            (xsel_l, slot_l, gate_l), (xsel_c, slot_c, gate_c) = route_gather(
                [x, ctx], norm_g[l, 1], [sc2, csc2], [sh2, csh2], moe_router[l])
            y_l, y_c = expert_ffn([xsel_l, xsel_c], moe_w_gate, moe_w_up, moe_w_down, l)
            x = combine_residual(y_l, slot_l, gate_l, x, g2)
            ctx = combine_residual(y_c, slot_c, gate_c, ctx, cg2)
    return x
```

```python
import functools
import math

import numpy as np
import jax
import jax.numpy as jnp
from jax import lax
from jax.experimental import pallas as pl
from jax.experimental.pallas import tpu as pltpu

F32 = jnp.float32
BF16 = jnp.bfloat16

D_MODEL = 1024
DEPTH = 4
GRID_W = 64
HEAD_DIM = 64
NA_HEADS = 8
NA_WIN_H = 8
NA_WIN_W = 16
DIFF_HEADS = 4
A_W = NA_HEADS * HEAD_DIM
ATT_IN_W = 3072
REC_HEADS = 8
REC_D = 128
N_EXPERTS = 16
EXPERT_FF = 2 * D_MODEL
EC_CAPACITY_FACTOR = 2
ROPE_THETA = 10000.0
NORM_EPS = 1e-6
NEG_INF = -1e30

LANES = 128
V7X_VMEM_BYTES = 64 * 1024 * 1024
MIB = 1024 * 1024

NA_QROWS = 4
NA_KROWS = NA_QROWS + NA_WIN_H - 1
NA_BLOCKS_PER_STEP = 2
DIFF_BLOCKS_PER_STEP = 2
REC_CHUNK = 64
REC_SUB = 16
REC_NSUB = REC_CHUNK // REC_SUB
REC_EXP_CLAMP = 60.0
REC_GROUP = 4
REC_GROUP_LATENT = 16
PREFIX_BLOCK = 256
THRESHOLD_RADIX_BITS = 3
ROUTE_EXPERTS_PER_STEP = 8
FFN_ROW_TILE = 512
FFN_HALVES = 2

LOG2E = math.log2(math.e)
SCORE_MULT = HEAD_DIM ** -0.5 * LOG2E

NT_DIMS = (((1,), (1,)), ((), ()))
TN_DIMS = (((0,), (0,)), ((), ()))


def _cparams(semantics, vmem_mib):
    assert vmem_mib * MIB < V7X_VMEM_BYTES
    return pltpu.CompilerParams(dimension_semantics=semantics, vmem_limit_bytes=vmem_mib * MIB)


def _sigmoid(x):
    return 1.0 / (1.0 + jnp.exp(-x))


def _rms(x, eps=NORM_EPS):
    return x * lax.rsqrt(jnp.mean(x * x, axis=-1, keepdims=True) + eps)


def _mod_kernel(c_ref, w_ref, b_ref, o_ref):
    c = c_ref[...]
    s = (c * _sigmoid(c)).astype(BF16)
    o_ref[0] = jnp.dot(s, w_ref[0].astype(BF16), preferred_element_type=F32) + b_ref[0]


def modulation_all(c_all, w_mod, b_mod):
    depth, d, n = w_mod.shape
    r = c_all.shape[0]
    tn = 1536
    return pl.pallas_call(
        _mod_kernel,
        grid=(depth, n // tn),
        in_specs=[
            pl.BlockSpec((r, d), lambda l, j: (0, 0)),
            pl.BlockSpec((1, d, tn), lambda l, j: (l, 0, j)),
            pl.BlockSpec((1, 1, tn), lambda l, j: (l, 0, j)),
        ],
        out_specs=pl.BlockSpec((1, r, tn), lambda l, j: (l, 0, j)),
        out_shape=jax.ShapeDtypeStruct((depth, r, n), F32),
        compiler_params=_cparams(("parallel", "parallel"), 32),
        name="modulation",
    )(c_all, w_mod, b_mod.reshape(depth, 1, n))


def _norm_mod(x, g, sc, sh):
    return _rms(x) * g * (1.0 + sc) + sh


def _proj_kernel(x_ref, g_ref, sc_ref, sh_ref, w_ref, o_ref, h_scr, *, tn):
    h_scr[...] = _norm_mod(x_ref[0], g_ref[...], sc_ref[0], sh_ref[0]).astype(BF16)
    n = w_ref.shape[1]
    for j in range(n // tn):
        o_ref[0, :, j * tn:(j + 1) * tn] = jnp.dot(
            h_scr[...], w_ref[:, j * tn:(j + 1) * tn], preferred_element_type=F32).astype(o_ref.dtype)


def norm_mod_proj(x, g, sc, sh, w_bf16, out_dtype):
    b, t, d = x.shape
    n = w_bf16.shape[1]
    tm = min(t, 512)
    return pl.pallas_call(
        functools.partial(_proj_kernel, tn=512),
        grid=(b, t // tm),
        in_specs=[
            pl.BlockSpec((1, tm, d), lambda i, j: (i, j, 0)),
            pl.BlockSpec((1, d), lambda i, j: (0, 0)),
            pl.BlockSpec((1, 1, d), lambda i, j: (i, 0, 0)),
            pl.BlockSpec((1, 1, d), lambda i, j: (i, 0, 0)),
            pl.BlockSpec((d, n), lambda i, j: (0, 0)),
        ],
        out_specs=pl.BlockSpec((1, tm, n), lambda i, j: (i, j, 0)),
        out_shape=jax.ShapeDtypeStruct((b, t, n), out_dtype),
        scratch_shapes=[pltpu.VMEM((tm, d), BF16)],
        compiler_params=_cparams(("parallel", "parallel"), 56),
        name="norm_mod_proj",
    )(x, g.reshape(1, d), sc, sh, w_bf16)


def _outproj_kernel(*refs, n_in):
    o_refs, w_refs = refs[:n_in], refs[n_in:2 * n_in]
    x_ref, gate_ref, out_ref = refs[2 * n_in:]
    acc = jnp.dot(o_refs[0][0], w_refs[0][...], preferred_element_type=F32)
    for o_ref, w_ref in zip(o_refs[1:], w_refs[1:]):
        acc = acc + jnp.dot(o_ref[0], w_ref[...], preferred_element_type=F32)
    out_ref[0] = x_ref[0] + gate_ref[0] * acc


def outproj_residual(os_, ws, x, gate):
    b, t, d = x.shape
    tm = min(t, 512)
    n_in = len(os_)
    in_specs = [pl.BlockSpec((1, tm, o.shape[2]), lambda i, j: (i, j, 0)) for o in os_]
    in_specs += [pl.BlockSpec(w.shape, lambda i, j: (0, 0)) for w in ws]
    in_specs += [pl.BlockSpec((1, tm, d), lambda i, j: (i, j, 0)),
                 pl.BlockSpec((1, 1, d), lambda i, j: (i, 0, 0))]
    return pl.pallas_call(
        functools.partial(_outproj_kernel, n_in=n_in),
        grid=(b, t // tm),
        in_specs=in_specs,
        out_specs=pl.BlockSpec((1, tm, d), lambda i, j: (i, j, 0)),
        out_shape=jax.ShapeDtypeStruct((b, t, d), F32),
        compiler_params=_cparams(("parallel", "parallel"), 32),
        name="outproj_residual",
    )(*os_, *ws, x, gate)


def _na_block_tables(rows):
    kh = min(NA_WIN_H, rows)
    assert kh == NA_WIN_H and rows % NA_QROWS == 0 and rows >= NA_KROWS
    n_qb = rows // NA_QROWS
    starts = [min(max(qb * NA_QROWS - kh // 2, 0), rows - NA_KROWS) for qb in range(n_qb)]
    tbl_of, reps = [], []
    for qb in range(n_qb):
        r0 = qb * NA_QROWS
        interior = (r0 - kh // 2 >= 0) and (r0 + NA_QROWS - 1 - kh // 2 <= rows - kh) and starts[qb] == r0 - kh // 2
        key = "interior" if interior else qb
        if key not in reps:
            reps.append(key)
        tbl_of.append(reps.index(key))
    rep_qb = [next(qb for qb in range(n_qb) if tbl_of[qb] == i) for i in range(len(reps))]
    tiles = []
    for qb in rep_qb:
        tiles.append([])
        for qi in range(NA_QROWS):
            r = qb * NA_QROWS + qi
            rs = min(max(r - kh // 2, 0), rows - kh)
            tiles[-1].append([(starts[qb] + kk - r + NA_WIN_H - 1) if rs <= starts[qb] + kk < rs + kh else None
                              for kk in range(NA_KROWS)])
    return starts, tbl_of, tiles


def na_bias_tables(rpb, rows, ctx_len, mult):
    _, _, tiles = _na_block_tables(rows)
    c = np.arange(GRID_W)
    cs = np.clip(c - NA_WIN_W // 2, 0, GRID_W - NA_WIN_W)
    col_ok = (c[None, :] >= cs[:, None]) & (c[None, :] < cs[:, None] + NA_WIN_W)
    dcol = c[None, :] - c[:, None] + NA_WIN_W - 1
    pick = ((dcol[None] == np.arange(2 * NA_WIN_W - 1)[:, None, None]) & col_ok[None]).astype(np.float32)
    toe = jnp.einsum("hrd,dqk->hrqk", rpb.astype(F32), jnp.asarray(pick), precision=lax.Precision.HIGHEST)
    toe = jnp.where(jnp.asarray(col_ok), toe * mult, NEG_INF)
    neg = jnp.full((rpb.shape[0], GRID_W, GRID_W), NEG_INF, F32)
    tables = [jnp.concatenate([jnp.concatenate([neg if d is None else toe[:, d] for d in row], axis=-1)
                               for row in tbl], axis=-2) for tbl in tiles]
    loc = jnp.stack(tables, axis=1)
    return jnp.concatenate([loc, jnp.zeros(loc.shape[:3] + (ctx_len,), F32)], axis=-1)


def _scores_many(qs, ks):
    return [lax.dot_general(q, k, NT_DIMS, preferred_element_type=F32) for q, k in zip(qs, ks)]


def _softmax_pv_many(s2, vs, biases=None):
    if biases is not None:
        s2 = [s + b for s, b in zip(s2, biases)]
    ps = [jnp.exp2(s - jnp.max(s, axis=-1, keepdims=True)) for s in s2]
    inv = [1.0 / jnp.sum(p, axis=-1, keepdims=True) for p in ps]
    return [jnp.dot(p.astype(BF16), v, preferred_element_type=F32) * r for p, v, r in zip(ps, vs, inv)]


def _head_half_scales(mult):
    first = lax.broadcasted_iota(jnp.int32, (1, LANES), 1) < HEAD_DIM
    return first, jnp.where(first, mult, 0.0), jnp.where(first, 0.0, mult)


def _na_kernel(q_ref, k_ref, v_ref, qc_ref, kc_ref, vc_ref, bias_ref, o_ref, oc_ref, *, starts, tbl_of):
    first, *halves = _head_half_scales(SCORE_MULT)
    nq, nk = NA_QROWS * GRID_W, NA_KROWS * GRID_W
    kc, vc = kc_ref[0], vc_ref[0]
    n_qb = len(starts)
    assert n_qb % NA_BLOCKS_PER_STEP == 0

    def group_scores(qb0):
        qs, ks = [], []
        for qb in range(qb0, qb0 + NA_BLOCKS_PER_STEP):
            q = q_ref[0, qb * nq:(qb + 1) * nq, :].astype(F32)
            k_all = jnp.concatenate([k_ref[0, starts[qb] * GRID_W:starts[qb] * GRID_W + nk, :], kc], axis=0)
            for hh in range(2):
                qs.append((q * halves[hh]).astype(BF16))
                ks.append(k_all)
        return _scores_many(qs, ks)

    s_next = group_scores(0)
    for qb0 in range(0, n_qb, NA_BLOCKS_PER_STEP):
        blocks = range(qb0, qb0 + NA_BLOCKS_PER_STEP)
        s_cur = s_next
        if qb0 + NA_BLOCKS_PER_STEP < n_qb:
            s_next = group_scores(qb0 + NA_BLOCKS_PER_STEP)
        vs, bs = [], []
        for qb in blocks:
            v_all = jnp.concatenate([v_ref[0, starts[qb] * GRID_W:starts[qb] * GRID_W + nk, :], vc], axis=0)
            for hh in range(2):
                vs.append(v_all)
                bs.append(bias_ref[hh, tbl_of[qb]])
        outs = _softmax_pv_many(s_cur, vs, bs)
        for i, qb in enumerate(blocks):
            o_ref[0, qb * nq:(qb + 1) * nq, :] = jnp.where(first, outs[2 * i], outs[2 * i + 1]).astype(o_ref.dtype)
    qc = qc_ref[0].astype(F32)
    outs = _softmax_pv_many(_scores_many([(qc * h).astype(BF16) for h in halves], [kc, kc]), [vc, vc])
    oc_ref[0] = jnp.where(first, outs[0], outs[1]).astype(oc_ref.dtype)


def na_attention(p_lat, p_ctx, bias):
    b, t, _ = p_lat.shape
    l = p_ctx.shape[1]
    rows = t // GRID_W
    starts, tbl_of, _ = _na_block_tables(rows)
    n_pairs = NA_HEADS // 2
    kb, vb = 2 * A_W // LANES, 3 * A_W // LANES
    lat = lambda off: pl.BlockSpec((1, t, LANES), lambda j, i: (i, 0, off + j))
    cx = lambda off: pl.BlockSpec((1, l, LANES), lambda j, i: (i, 0, off + j))
    n_tbl, nq, nkc = bias.shape[1:]
    return pl.pallas_call(
        functools.partial(_na_kernel, starts=tuple(starts), tbl_of=tuple(tbl_of)),
        grid=(n_pairs, b),
        in_specs=[lat(0), lat(kb), lat(vb), cx(0), cx(kb), cx(vb),
                  pl.BlockSpec((2, n_tbl, nq, nkc), lambda j, i: (j, 0, 0, 0))],
        out_specs=[pl.BlockSpec((1, t, LANES), lambda j, i: (i, 0, j)),
                   pl.BlockSpec((1, l, LANES), lambda j, i: (i, 0, j))],
        out_shape=[jax.ShapeDtypeStruct((b, t, A_W), BF16), jax.ShapeDtypeStruct((b, l, A_W), BF16)],
        compiler_params=_cparams(("parallel", "parallel"), 48),
        name="na_attention",
    )(p_lat, p_lat, p_lat, p_ctx, p_ctx, p_ctx, bias)


def rope_tables(n):
    t = np.arange(n)
    rows, cols = (t // GRID_W).astype(np.float64), (t % GRID_W).astype(np.float64)
    n_freq = HEAD_DIM // 4
    inv = ROPE_THETA ** (-np.arange(n_freq, dtype=np.float64) / n_freq)
    ang = np.concatenate([rows[:, None] * inv, cols[:, None] * inv], axis=-1)
    cos = np.tile(np.cos(ang), (1, 4))
    sin = np.tile(np.concatenate([-np.sin(ang), np.sin(ang)], axis=-1), (1, 2))
    return jnp.asarray(cos, F32), jnp.asarray(sin, F32)


def _rope(a, cos, sin_signed):
    lane = lax.broadcasted_iota(jnp.int32, (1, LANES), 1)
    low = (lane % HEAD_DIM) < HEAD_DIM // 2
    partner = jnp.where(low, pltpu.roll(a, LANES - HEAD_DIM // 2, 1), pltpu.roll(a, HEAD_DIM // 2, 1))
    return a * cos + partner * sin_signed


def _diff_kernel(q_ref, k_ref, v_ref, qc_ref, kc_ref, vc_ref, cos_ref, sin_ref, lam_ref, gcol_ref,
                 o_ref, oc_ref, q1_scr, q2_scr, k_scr, vt_scr, *, lam_init, tq):
    t, l = q_ref.shape[1], kc_ref.shape[1]
    _, half1, half2 = _head_half_scales(SCORE_MULT)
    lv = lam_ref[...]
    lam = (jnp.exp(jnp.sum(lv[0:1] * lv[1:2], axis=-1, keepdims=True))
           - jnp.exp(jnp.sum(lv[2:3] * lv[3:4], axis=-1, keepdims=True)) + lam_init)
    cos, sin = cos_ref[...], sin_ref[...]
    q = _rope(q_ref[0].astype(F32), cos, sin)
    q1_scr[...] = (q * half1).astype(BF16)
    q2_scr[...] = (q * half2).astype(BF16)
    k_scr[0:t, :] = _rope(k_ref[0].astype(F32), cos, sin).astype(BF16)
    k_scr[t:t + l, :] = kc_ref[0]
    for i in range(t // tq):
        vt_scr[:, i * tq:(i + 1) * tq] = v_ref[0, i * tq:(i + 1) * tq, :].astype(F32).T.astype(BF16)
    vct = vc_ref[0].astype(F32).T.astype(BF16)
    vt_scr[:, t:t + l] = vct
    gain = gcol_ref[...] * (1.0 - lam_init)

    def scores(q_pairs, k):
        return [lax.dot_general(k, q, NT_DIMS, preferred_element_type=F32) for pair in q_pairs for q in pair]

    def finish(s2, vt):
        ps = [jnp.exp2(s - jnp.max(s, axis=0, keepdims=True)) for s in s2]
        inv = [1.0 / jnp.sum(p, axis=0, keepdims=True) for p in ps]
        ot = [jnp.dot(vt, p.astype(BF16), preferred_element_type=F32) * r for p, r in zip(ps, inv)]
        outs = []
        for i in range(len(s2) // 2):
            d = ot[2 * i] - lam * ot[2 * i + 1]
            d = d * lax.rsqrt(jnp.mean(d * d, axis=0, keepdims=True) + NORM_EPS) * gain
            outs.append(d.T)
        return outs

    step = tq * DIFF_BLOCKS_PER_STEP
    group = lambda r0: [(q1_scr[r:r + tq, :], q2_scr[r:r + tq, :]) for r in range(r0, r0 + step, tq)]
    k_all, vt_all = k_scr[...], vt_scr[...]
    s_next = scores(group(0), k_all)
    for r0 in range(0, t, step):
        s_cur = s_next
        if r0 + step < t:
            s_next = scores(group(r0 + step), k_all)
        for j, o in enumerate(finish(s_cur, vt_all)):
            o_ref[0, r0 + j * tq:r0 + (j + 1) * tq, :] = o.astype(o_ref.dtype)
    qc = qc_ref[0].astype(F32)
    (oc,) = finish(scores([((qc * half1).astype(BF16), (qc * half2).astype(BF16))], kc_ref[0]), vct)
    oc_ref[0] = oc.astype(oc_ref.dtype)


def diff_attention(p_lat, p_ctx, cos, sin, lam_vecs, subln_g, layer_idx):
    b, t, _ = p_lat.shape
    l = p_ctx.shape[1]
    lam_init = 0.8 - 0.6 * math.exp(-0.3 * layer_idx)
    qb, kb, vb = A_W // LANES, 4 * A_W // LANES, 5 * A_W // LANES
    lat = lambda off: pl.BlockSpec((1, t, LANES), lambda i, j: (i, 0, off + j))
    cx = lambda off: pl.BlockSpec((1, l, LANES), lambda i, j: (i, 0, off + j))
    const = lambda shape: pl.BlockSpec(shape, lambda i, j: (0, 0))
    tq = 256
    assert t % (tq * DIFF_BLOCKS_PER_STEP) == 0
    return pl.pallas_call(
        functools.partial(_diff_kernel, lam_init=lam_init, tq=tq),
        grid=(b, DIFF_HEADS),
        in_specs=[lat(qb), lat(kb), lat(vb), cx(qb), cx(kb), cx(vb),
                  const((t, LANES)), const((t, LANES)), const(lam_vecs.shape), const((LANES, 1))],
        out_specs=[pl.BlockSpec((1, t, LANES), lambda i, j: (i, 0, j)),
                   pl.BlockSpec((1, l, LANES), lambda i, j: (i, 0, j))],
        out_shape=[jax.ShapeDtypeStruct((b, t, A_W), BF16), jax.ShapeDtypeStruct((b, l, A_W), BF16)],
        scratch_shapes=[pltpu.VMEM((t, LANES), BF16), pltpu.VMEM((t, LANES), BF16),
                        pltpu.VMEM((t + l, LANES), BF16), pltpu.VMEM((LANES, t + l), BF16)],
        compiler_params=_cparams(("parallel", "parallel"), 48),
        name="diff_attention",
    )(p_lat, p_lat, p_lat, p_ctx, p_ctx, p_ctx, cos, sin, lam_vecs, subln_g.reshape(LANES, 1))


def _split2(x):
    a = x.astype(BF16)
    return a, (x - a.astype(F32)).astype(BF16)


def _rec_masks(forward):
    t = lax.broadcasted_iota(jnp.int32, (REC_CHUNK, REC_CHUNK), 0)
    s = lax.broadcasted_iota(jnp.int32, (REC_CHUNK, REC_CHUNK), 1)
    incl = (s <= t) if forward else (s >= t)
    return jnp.where(incl, 1.0, 0.0).astype(BF16), incl


def _rec_group_local(qs, fxs, vb, lbs):
    c = REC_CHUNK
    n_grp = qs.shape[0] // c
    items = [(d, j) for j in range(n_grp) for d in range(2)]
    rows = lambda a, j: a[j * c:(j + 1) * c]
    masks = [_rec_masks(True), _rec_masks(False)]
    kks, g3s = [], []
    for fx, lb in zip(fxs, lbs):
        e = jnp.exp(-jnp.abs(fx))
        r = 1.0 / (1.0 + e)
        pos = fx >= 0
        f = lb + (1.0 - lb) * jnp.where(pos, r, e * r)
        kks.append((1.0 - lb) * jnp.where(pos, e * r, r))
        g3s.append(jnp.concatenate(_split2(jnp.log(f)), axis=1))
    cs = {(d, j): jnp.dot(masks[d][0], rows(g3s[d], j), preferred_element_type=F32) for d, j in items}
    row_sub = lax.broadcasted_iota(jnp.int32, (c, 1), 0) // REC_SUB
    band = lambda a, p: a[p * REC_SUB:(p + 1) * REC_SUB]
    subs = range(REC_NSUB)
    qcat, kcat, qtil, kdec, dec = {}, {}, {}, {}, {}
    for d, j in items:
        x = cs[d, j]
        x = x[:, 0:LANES] + x[:, LANES:2 * LANES]
        bcum = x
        q, kk = rows(qs, j), rows(kks[d], j)
        if d == 0:
            ends = [bcum[(p + 1) * REC_SUB - 1:(p + 1) * REC_SUB] for p in subs]
            starts = [jnp.zeros((1, LANES), F32)] + ends[:-1]
            blast = ends[-1]
        else:
            ends = [bcum[p * REC_SUB:p * REC_SUB + 1] for p in subs]
            starts = ends[1:] + [jnp.zeros((1, LANES), F32)]
            blast = ends[0]
        ends_mat = jnp.concatenate(ends, axis=0)
        per_row = lambda bounds: jnp.concatenate([jnp.broadcast_to(b_, (REC_SUB, LANES)) for b_ in bounds], axis=0)
        qhat = q * jnp.exp(bcum - per_row(starts))
        bend = per_row(ends)
        ke = kk * jnp.exp(bend - bcum)
        qcat[d, j] = jnp.concatenate([jnp.where(row_sub == p, qhat, 0.0) for p in subs], axis=1).astype(BF16)
        grow = [jnp.exp(jnp.minimum(starts[p] - ends_mat, REC_EXP_CLAMP)) for p in subs]
        kcat[d, j] = jnp.concatenate(
            [jnp.concatenate([band(ke, s) * grow[p][s:s + 1] for s in subs], axis=0) for p in subs],
            axis=1).astype(BF16)
        from_start = jnp.exp(jnp.concatenate(starts, axis=0))
        to_last = jnp.exp(blast - ends_mat)
        qtil[d, j] = jnp.concatenate([band(qhat, p) * from_start[p:p + 1] for p in subs], axis=0).astype(BF16)
        kdec[d, j] = jnp.concatenate([band(ke, p) * to_last[p:p + 1] for p in subs], axis=0).astype(BF16)
        dec[d, j] = jnp.exp(blast)
    sc = {k: lax.dot_general(qcat[k], kcat[k], NT_DIMS, preferred_element_type=F32) for k in items}
    u2 = [lax.dot_general(rows(vb, j), jnp.concatenate([kdec[0, j], kdec[1, j]], axis=1), TN_DIMS,
                          preferred_element_type=F32) for j in range(n_grp)]
    u = {(d, j): u2[j][:, d * LANES:(d + 1) * LANES] for d, j in items}
    scb = {(d, j): jnp.where(masks[d][1], sc[d, j], 0.0).astype(BF16) for d, j in items}
    o_intra = {(d, j): jnp.dot(scb[d, j], rows(vb, j), preferred_element_type=F32) for d, j in items}
    return {k: (o_intra[k], qtil[k], u[k], dec[k]) for k in items}


def _rec_kernel(q_ref, ff_ref, fb_ref, v_ref, g_ref, qc_ref, ffc_ref, fbc_ref, vc_ref, gc_ref,
                lb_ref, gn_ref, y_ref, yc_ref, o_scr, qt_scr, u_scr, dec_scr, *, layer_idx):
    t, l = q_ref.shape[1], qc_ref.shape[1]
    c, grp = REC_CHUNK, REC_GROUP

    def lower_bound(d):
        lg = lb_ref[d]
        ex = jnp.exp(lg - jnp.max(lg, axis=0, keepdims=True))
        return jnp.sum(ex[1:layer_idx + 1], axis=0, keepdims=True) / jnp.sum(ex, axis=0, keepdims=True)

    lbs = (lower_bound(0), lower_bound(1))

    def local_pass(q_r, f_refs, v_r, n_tok, tok0, grp):
        def body(i, carry):
            r0 = pl.multiple_of(i * (grp * c), grp * c)
            q = q_r[0, pl.ds(r0, grp * c), :]
            qs = q * _sigmoid(q)
            vb = v_r[0, pl.ds(r0, grp * c), :].astype(BF16)
            fxs = [f_r[0, pl.ds(r0, grp * c), :] for f_r in f_refs]
            local = _rec_group_local(qs, fxs, vb, lbs)
            for j in range(grp):
                row = pl.multiple_of(tok0 + r0 + j * c, c)
                ci = tok0 // c + i * grp + j
                for d in range(2):
                    o_intra, qtil, u, dec = local[d, j]
                    o_scr[d, pl.ds(row, c), :] = o_intra
                    qt_scr[d, pl.ds(row, c), :] = qtil
                    u_scr[d, ci] = u
                    dec_scr[d, pl.ds(ci, 1), :] = dec
            return carry
        lax.fori_loop(0, n_tok // (grp * c), body, 0)

    def state_pass(c_lo, n, states):
        def body(i, carry):
            sts = list(carry)
            for k in range(grp):
                step = i * grp + k
                for d, ci in ((0, c_lo + step), (1, c_lo + n - 1 - step)):
                    row = pl.multiple_of(ci * c, c)
                    o_scr[d, pl.ds(row, c), :] += lax.dot_general(
                        qt_scr[d, pl.ds(row, c), :], sts[d].astype(BF16), NT_DIMS, preferred_element_type=F32)
                    sts[d] = sts[d] * dec_scr[d, pl.ds(ci, 1), :] + u_scr[d, ci]
            return tuple(sts)
        return lax.fori_loop(0, n // grp, body, states)

    def finish(tok0, n_tok, gate_ref, out_ref):
        o = o_scr[0, tok0:tok0 + n_tok, :] + o_scr[1, tok0:tok0 + n_tok, :]
        gate = gate_ref[0]
        out_ref[0] = ((_rms(o) * gn_ref[...]) * (gate * _sigmoid(gate))).astype(out_ref.dtype)

    local_pass(qc_ref, (ffc_ref, fbc_ref), vc_ref, l, 0, REC_GROUP)
    local_pass(q_ref, (ff_ref, fb_ref), v_ref, t, l, REC_GROUP_LATENT)
    zero = jnp.zeros((REC_D, REC_D), F32)
    states = state_pass(0, l // c, (zero, zero))
    state_pass(l // c, t // c, states)
    finish(0, l, gc_ref, yc_ref)
    finish(l, t, g_ref, y_ref)


def hgrn2_mixer(p_lat, p_ctx, lb_logits, gnorm_g, layer_idx):
    b, t, _ = p_lat.shape
    l = p_ctx.shape[1]
    assert t % (REC_GROUP_LATENT * REC_CHUNK) == 0 and l % (REC_GROUP * REC_CHUNK) == 0
    n_chunks = (t + l) // REC_CHUNK
    lat = lambda k: pl.BlockSpec((1, t, REC_D), lambda i, j: (i, 0, k * REC_HEADS + j))
    cx = lambda k: pl.BlockSpec((1, l, REC_D), lambda i, j: (i, 0, k * REC_HEADS + j))
    return pl.pallas_call(
        functools.partial(_rec_kernel, layer_idx=layer_idx),
        grid=(b, REC_HEADS),
        in_specs=[lat(k) for k in range(5)] + [cx(k) for k in range(5)] + [
            pl.BlockSpec((2, DEPTH, REC_D), lambda i, j: (0, 0, j)),
            pl.BlockSpec((1, REC_D), lambda i, j: (0, 0))],
        out_specs=[pl.BlockSpec((1, t, REC_D), lambda i, j: (i, 0, j)),
                   pl.BlockSpec((1, l, REC_D), lambda i, j: (i, 0, j))],
        out_shape=[jax.ShapeDtypeStruct((b, t, D_MODEL), BF16), jax.ShapeDtypeStruct((b, l, D_MODEL), BF16)],
        scratch_shapes=[pltpu.VMEM((2, t + l, REC_D), F32), pltpu.VMEM((2, t + l, REC_D), BF16),
                        pltpu.VMEM((2, n_chunks, REC_D, REC_D), F32), pltpu.VMEM((2, n_chunks, REC_D), F32)],
        compiler_params=_cparams(("parallel", "parallel"), 40),
        name="hgrn2_mixer",
    )(*([p_lat] * 5), *([p_ctx] * 5), lb_logits, gnorm_g.reshape(1, REC_D))


def _prefix_count(flags):
    e, t = flags.shape
    blk = min(PREFIX_BLOCK, t)
    ones = jnp.where(flags, 1.0, 0.0)
    s = lax.broadcasted_iota(jnp.int32, (blk, blk), 0)
    u = lax.broadcasted_iota(jnp.int32, (blk, blk), 1)
    upper = jnp.where(s < u, 1.0, 0.0).astype(BF16)
    carry = jnp.zeros((e, 1), F32)
    outs = []
    for i in range(t // blk):
        xs = ones[:, i * blk:(i + 1) * blk]
        outs.append(jnp.dot(xs.astype(BF16), upper, preferred_element_type=F32) + carry)
        carry = carry + jnp.sum(xs, axis=1, keepdims=True)
    return jnp.concatenate(outs, axis=1)


def _capacity_slots(affs, caps):
    as_float = lambda bits: lax.bitcast_convert_type(bits, F32)
    capf = [jnp.float32(c) for c in caps]
    thr = [jnp.zeros((N_EXPERTS, 1), jnp.int32) for _ in affs]
    for hi in range(30, -1, -THRESHOLD_RADIX_BITS):
        lo = max(hi - THRESHOLD_RADIX_BITS + 1, 0)
        cands = [[t | jnp.int32(m << lo) for m in range(1, 1 << (hi - lo + 1))] for t in thr]
        cnts = [[jnp.sum(jnp.where(a >= as_float(cand), 1.0, 0.0), axis=1, keepdims=True) for cand in cs]
                for a, cs in zip(affs, cands)]
        for i in range(len(affs)):
            for cand, cnt in zip(cands[i], cnts[i]):
                thr[i] = jnp.where(cnt >= capf[i], cand, thr[i])
    slots = []
    for a, t, c in zip(affs, thr, capf):
        gt = a > as_float(t)
        eq = a == as_float(t)
        need = c - jnp.sum(jnp.where(gt, 1.0, 0.0), axis=1, keepdims=True)
        sel = gt | (eq & (_prefix_count(eq) < need))
        slots.append(jnp.where(sel, _prefix_count(sel), -1.0))
    return slots


def _route_kernel(*refs, caps):
    n = len(caps)
    x_refs, g_ref = refs[:n], refs[n]
    sc_refs, sh_refs, wr_ref = refs[n + 1:2 * n + 1], refs[2 * n + 1:3 * n + 1], refs[3 * n + 1]
    outs = refs[3 * n + 2:]
    xsel_refs, slot_refs, gate_refs = outs[:n], outs[n:2 * n], outs[2 * n:3 * n]
    hb_scrs, slot_scrs = outs[3 * n:4 * n], outs[4 * n:5 * n]
    e = pl.program_id(1)

    @pl.when(e == 0)
    def _():
        w_hi, w_lo = _split2(wr_ref[...])
        w2 = jnp.concatenate([w_hi, w_lo], axis=0)
        affs = []
        for x_ref, sc_ref, sh_ref, hb_scr in zip(x_refs, sc_refs, sh_refs, hb_scrs):
            h = _norm_mod(x_ref[0], g_ref[...], sc_ref[0], sh_ref[0])
            hb = h.astype(BF16)
            hb_scr[...] = hb
            h_lo = (h - hb.astype(F32)).astype(BF16)
            hi_terms = lax.dot_general(w2, hb, NT_DIMS, preferred_element_type=F32)
            logits = (hi_terms[:N_EXPERTS] + hi_terms[N_EXPERTS:]
                      + lax.dot_general(w_hi, h_lo, NT_DIMS, preferred_element_type=F32))
            ex = jnp.exp(logits - jnp.max(logits, axis=0, keepdims=True))
            affs.append(ex / jnp.sum(ex, axis=0, keepdims=True))
        for aff, slot, slot_scr, slot_ref, gate_ref in zip(affs, _capacity_slots(affs, caps), slot_scrs, slot_refs,
                                                           gate_refs):
            slot_scr[...] = slot
            slot_ref[0] = slot
            gate_ref[0] = aff

    for cap, xsel_ref, hb_scr, slot_scr in zip(caps, xsel_refs, hb_scrs, slot_scrs):
        row = lax.broadcasted_iota(jnp.int32, (cap, hb_scr.shape[0]), 0).astype(F32)
        picks = [jnp.where(row == slot_scr[pl.ds(e * ROUTE_EXPERTS_PER_STEP + i, 1), :], 1.0, 0.0).astype(BF16)
                 for i in range(ROUTE_EXPERTS_PER_STEP)]
        for i, pick in enumerate(picks):
            xsel_ref[i] = jnp.dot(pick, hb_scr[...], preferred_element_type=F32).astype(xsel_ref.dtype)


def route_gather(xs, g, scs, shs, w_router):
    b, _, d = xs[0].shape
    ts = [x.shape[1] for x in xs]
    caps = [EC_CAPACITY_FACTOR * t // N_EXPERTS for t in ts]
    n = len(xs)
    vec = pl.BlockSpec((1, 1, d), lambda i, e: (i, 0, 0))
    per_sample = lambda t: pl.BlockSpec((1, N_EXPERTS, t), lambda i, e: (i, 0, 0))
    outs = pl.pallas_call(
        functools.partial(_route_kernel, caps=tuple(caps)),
        grid=(b, N_EXPERTS // ROUTE_EXPERTS_PER_STEP),
        in_specs=([pl.BlockSpec((1, t, d), lambda i, e: (i, 0, 0)) for t in ts]
                  + [pl.BlockSpec((1, d), lambda i, e: (0, 0))] + [vec] * (2 * n)
                  + [pl.BlockSpec((N_EXPERTS, d), lambda i, e: (0, 0))]),
        out_specs=([pl.BlockSpec((ROUTE_EXPERTS_PER_STEP, c, d), lambda i, e: (e, i, 0)) for c in caps]
                   + [per_sample(t) for t in ts] * 2),
        out_shape=([jax.ShapeDtypeStruct((N_EXPERTS, b * c, d), BF16) for c in caps]
                   + [jax.ShapeDtypeStruct((b, N_EXPERTS, t), F32) for t in ts] * 2),
        scratch_shapes=([pltpu.VMEM((t, d), BF16) for t in ts] + [pltpu.VMEM((N_EXPERTS, t), F32) for t in ts]),
        compiler_params=_cparams(("parallel", "arbitrary"), 48),
        name="route_gather",
    )(*xs, g.reshape(1, d), *scs, *shs, w_router.T)
    return [(outs[i], outs[n + i], outs[2 * n + i]) for i in range(n)]


def _ffn_kernel(*refs, n_lat_tiles, has_ctx, layer_idx):
    if has_ctx:
        xl_ref, xc_ref, wg_hbm, wu_hbm, wd_hbm, yl_ref, yc_ref, wg_s, wu_s, wd_s, stage, sem = refs
    else:
        xl_ref, wg_hbm, wu_hbm, wd_hbm, yl_ref, wg_s, wu_s, wd_s, stage, sem = refs
    e, r = pl.program_id(0), pl.program_id(1)
    half = stage.shape[0]
    w_hbm, w_s = (wg_hbm, wu_hbm, wd_hbm), (wg_s, wu_s, wd_s)

    def chunk_copy(expert, c):
        m, h = divmod(c, FFN_HALVES)
        if m < 2:
            src = w_hbm[m].at[layer_idx, expert, :, pl.ds(h * half, half)]
        else:
            src = w_hbm[m].at[layer_idx, expert, pl.ds(h * half, half), :]
        return pltpu.make_async_copy(src, stage, sem.at[0])

    def chunk_store(slot, c):
        m, h = divmod(c, FFN_HALVES)
        if m < 2:
            w_s[m][slot, :, h * half:(h + 1) * half] = stage[...].astype(BF16)
        else:
            w_s[m][slot, h * half:(h + 1) * half, :] = stage[...].astype(BF16)

    n_chunks = 3 * FFN_HALVES

    @pl.when((e == 0) & (r == 0))
    def _():
        for c in range(n_chunks):
            cp = chunk_copy(0, c)
            cp.start()
            cp.wait()
            chunk_store(0, c)

    nxt = e + 1
    prefetch = nxt < pl.num_programs(0)
    for c in range(n_chunks):
        @pl.when(prefetch & (r == c + 1))
        def _():
            chunk_copy(nxt, c).wait()
            chunk_store(nxt % 2, c)

        @pl.when(prefetch & (r == c))
        def _():
            chunk_copy(nxt, c).start()

    slot = e % 2

    def swiglu(x_ref, y_ref):
        x = x_ref[0]
        a = jnp.dot(x, wg_s[slot], preferred_element_type=F32)
        u = jnp.dot(x, wu_s[slot], preferred_element_type=F32)
        hid = ((a * _sigmoid(a)) * u).astype(BF16)
        y_ref[0] = jnp.dot(hid, wd_s[slot], preferred_element_type=F32).astype(y_ref.dtype)

    if has_ctx:
        @pl.when(r < n_lat_tiles)
        def _():
            swiglu(xl_ref, yl_ref)

        @pl.when(r == n_lat_tiles)
        def _():
            swiglu(xc_ref, yc_ref)
    else:
        swiglu(xl_ref, yl_ref)


def expert_ffn(xsels, w_gate, w_up, w_down, layer_idx):
    d, ff = w_gate.shape[2], w_gate.shape[3]
    has_ctx = len(xsels) == 2
    tm = min(FFN_ROW_TILE, xsels[1].shape[1]) if has_ctx else FFN_ROW_TILE
    while xsels[0].shape[1] // tm + has_ctx <= 3 * FFN_HALVES:
        tm //= 2
    n_lat_tiles = xsels[0].shape[1] // tm
    n_tiles = n_lat_tiles + (1 if has_ctx else 0)
    assert xsels[0].shape[1] % tm == 0 and ff == FFN_HALVES * d and n_tiles > 3 * FFN_HALVES
    if has_ctx:
        assert xsels[1].shape[1] == tm
    lat_spec = pl.BlockSpec((1, tm, d), lambda e, r: (e, jnp.minimum(r, n_lat_tiles - 1), 0))
    ctx_spec = pl.BlockSpec((1, tm, d), lambda e, r: (e, 0, 0))
    io_specs = [lat_spec, ctx_spec] if has_ctx else [lat_spec]
    hbm = pl.BlockSpec(memory_space=pl.ANY)
    return pl.pallas_call(
        functools.partial(_ffn_kernel, n_lat_tiles=n_lat_tiles, has_ctx=has_ctx, layer_idx=layer_idx),
        grid=(N_EXPERTS, n_tiles),
        in_specs=io_specs + [hbm, hbm, hbm],
        out_specs=io_specs,
        out_shape=[jax.ShapeDtypeStruct(x.shape, BF16) for x in xsels],
        scratch_shapes=[pltpu.VMEM((2, d, ff), BF16), pltpu.VMEM((2, d, ff), BF16), pltpu.VMEM((2, ff, d), BF16),
                        pltpu.VMEM((d, d), F32), pltpu.SemaphoreType.DMA((1,))],
        compiler_params=_cparams(("arbitrary", "arbitrary"), 56),
        name="expert_ffn",
    )(*xsels, w_gate, w_up, w_down)


def _combine_kernel(y_ref, slot_ref, gate_ref, x_ref, g2_ref, fg_ref, out_ref, *, final_norm):
    n_e, cap, d = y_ref.shape
    tt = x_ref.shape[1]
    slot, gate = slot_ref[0], gate_ref[0]
    row = lax.broadcasted_iota(jnp.int32, (cap, tt), 0).astype(F32)
    pg = jnp.concatenate([jnp.where(row == slot[e:e + 1], gate[e:e + 1], 0.0).astype(BF16) for e in range(n_e)],
                         axis=0)
    moe = lax.dot_general(pg, y_ref[...].reshape(n_e * cap, d), TN_DIMS, preferred_element_type=F32)
    xn = x_ref[0] + g2_ref[0] * moe
    if final_norm:
        xn = _rms(xn) * fg_ref[...]
    out_ref[0] = xn


def combine_residual(y_sel, slot, gate, x, g2, final_g=None):
    b, t, d = x.shape
    cap = EC_CAPACITY_FACTOR * t // N_EXPERTS
    tt = min(t, 1024)
    final_norm = final_g is not None
    fg = (final_g if final_norm else jnp.ones((d,), F32)).reshape(1, d)
    return pl.pallas_call(
        functools.partial(_combine_kernel, final_norm=final_norm),
        grid=(b, t // tt),
        in_specs=[
            pl.BlockSpec((N_EXPERTS, cap, d), lambda i, j: (0, i, 0)),
            pl.BlockSpec((1, N_EXPERTS, tt), lambda i, j: (i, 0, j)),
            pl.BlockSpec((1, N_EXPERTS, tt), lambda i, j: (i, 0, j)),
            pl.BlockSpec((1, tt, d), lambda i, j: (i, j, 0)),
            pl.BlockSpec((1, 1, d), lambda i, j: (i, 0, 0)),
            pl.BlockSpec((1, d), lambda i, j: (0, 0)),
        ],
        out_specs=pl.BlockSpec((1, tt, d), lambda i, j: (i, j, 0)),
        out_shape=jax.ShapeDtypeStruct((b, t, d), F32),
        compiler_params=_cparams(("parallel", "parallel"), 48),
        name="combine_residual",
    )(y_sel, slot, gate, x, g2, fg)


def kernel(x, c, ctx, c_ctx, w_mod, b_mod, norm_g, att_w_in, att_w_out, na_rpb, diff_lambda, diff_subln_g,
           rec_w_in, rec_w_out, rec_lb_logits, rec_gnorm_g, moe_router, moe_w_gate, moe_w_up, moe_w_down,
           final_g):
    b, t, d = x.shape
    l_ctx = ctx.shape[1]
    n_c = b + 1
    r_pad = -(-n_c // 8) * 8
    c_all = jnp.concatenate([c, c_ctx[None, :], jnp.zeros((r_pad - n_c, d), F32)], axis=0)
    mods = modulation_all(c_all, w_mod, b_mod)
    cos, sin = rope_tables(t)

    for l in range(DEPTH):
        last = l == DEPTH - 1
        lat_m = [mods[l, :b, i * d:(i + 1) * d][:, None, :] for i in range(6)]
        ctx_m = [jnp.broadcast_to(mods[l, b, i * d:(i + 1) * d][None, None, :], (b, 1, d)) for i in range(6)]
        sh1, sc1, g1, sh2, sc2, g2 = lat_m
        csh1, csc1, cg1, csh2, csc2, cg2 = ctx_m

        if l % 2 == 0:
            e = l // 2
            w_in = att_w_in[e].astype(BF16)
            w_out = att_w_out[e].astype(BF16)
            p_lat = norm_mod_proj(x, norm_g[l, 0], sc1, sh1, w_in, BF16)
            p_ctx = norm_mod_proj(ctx, norm_g[l, 0], csc1, csh1, w_in, BF16)
            bias = na_bias_tables(na_rpb[e], t // GRID_W, l_ctx, LOG2E)
            oa_lat, oa_ctx = na_attention(p_lat, p_ctx, bias)
            ob_lat, ob_ctx = diff_attention(p_lat, p_ctx, cos, sin, diff_lambda[e], diff_subln_g[e], l)
            ws = [w_out[:A_W], w_out[A_W:]]
            x = outproj_residual([oa_lat, ob_lat], ws, x, g1)
            if not last:
                ctx = outproj_residual([oa_ctx, ob_ctx], ws, ctx, cg1)
        else:
            o = l // 2
            w_in = rec_w_in[o].astype(BF16)
            w_out = rec_w_out[o].astype(BF16)
            p_lat = norm_mod_proj(x, norm_g[l, 0], sc1, sh1, w_in, F32)
            p_ctx = norm_mod_proj(ctx, norm_g[l, 0], csc1, csh1, w_in, F32)
            y_lat, y_ctx = hgrn2_mixer(p_lat, p_ctx, rec_lb_logits, rec_gnorm_g[o], l)
            x = outproj_residual([y_lat], [w_out], x, g1)
            if not last:
                ctx = outproj_residual([y_ctx], [w_out], ctx, cg1)

        if last:
            ((xsel_l, slot_l, gate_l),) = route_gather([x], norm_g[l, 1], [sc2], [sh2], moe_router[l])
            (y_l,) = expert_ffn([xsel_l], moe_w_gate, moe_w_up, moe_w_down, l)
            x = combine_residual(y_l, slot_l, gate_l, x, g2, final_g)
        else:
            (xsel_l, slot_l, gate_l), (xsel_c, slot_c, gate_c) = route_gather(
                [x, ctx], norm_g[l, 1], [sc2, csc2], [sh2, csh2], moe_router[l])
            y_l, y_c = expert_ffn([xsel_l, xsel_c], moe_w_gate, moe_w_up, moe_w_down, l)
            x = combine_residual(y_l, slot_l, gate_l, x, g2)
            ctx = combine_residual(y_c, slot_c, gate_c, ctx, cg2)
    return x
```
